```python
import math
import jax, jax.numpy as jnp
from jax import lax
import numpy as np

D_MODEL = 1024
BATCH = 8
SEQ = 4096
DEPTH = 2

MEM_LEN = 256
HEAD_DIM = 64
ROPE_THETA = 10000.0
BLOCK_Q = 128
EPS = 1e-5
A_HEADS = 4
A_QK = A_HEADS * 2 * HEAD_DIM
A_V = A_HEADS * 2 * HEAD_DIM
A_OUT = A_V
B_HEADS = 8
B_W = B_HEADS * HEAD_DIM
B_OUT = B_W
FORGET_BIAS_LO = 1.0
FORGET_BIAS_HI = 5.0
C_PATTERNS = ((128, 1), (512, 4), (2048, 16))
C_GROUPS = len(C_PATTERNS)
C_SLOTS = 4
C_W = C_GROUPS * C_SLOTS * HEAD_DIM
C_OUT = C_SLOTS * HEAD_DIM
N_BRANCH = 3
IN_SIZES = (A_QK, A_QK, A_V, B_W, B_W, B_W, B_HEADS, C_W, C_W, C_W)
IN_COLS = sum(IN_SIZES)
IN_SPLITS = np.cumsum(IN_SIZES)[:-1].tolist()
X_HEADS = 4
X_HEAD_DIM = D_MODEL // X_HEADS
N_GROUPS = 4
EXPERTS_PER_GROUP = 4
EXPERT_TOPK = 2
D_EXPERT = D_MODEL // 4
DN_ALPHA = (2 * DEPTH) ** 0.25
DN_BETA = (8 * DEPTH) ** -0.25
N_SUBLAYERS = 3

kernel_name = "hybrid_diff_fox_dilated_hmoe_deepnorm"


def layer_norm(x, g, b):
    xf = x.astype(jnp.float32)
    mu = jnp.mean(xf, axis=-1, keepdims=True)
    var = jnp.mean(jnp.square(xf - mu), axis=-1, keepdims=True)
    return ((xf - mu) * lax.rsqrt(var + EPS) * g + b).astype(x.dtype)


def rms_norm(x, g):
    xf = x.astype(jnp.float32)
    return (xf * lax.rsqrt(jnp.mean(jnp.square(xf), axis=-1, keepdims=True) + EPS) * g).astype(x.dtype)


def rope(x, positions):
    half = x.shape[-1] // 2
    inv = ROPE_THETA ** (-jnp.arange(half, dtype=jnp.float32) / half)
    ang = positions.astype(jnp.float32)[:, :, None, None] * inv
    cos, sin = jnp.cos(ang), jnp.sin(ang)
    xf = x.astype(jnp.float32)
    x1, x2 = xf[..., :half], xf[..., half:]
    return jnp.concatenate([x1 * cos - x2 * sin, x2 * cos + x1 * sin], axis=-1).astype(x.dtype)


def _to_blocks(t):
    b, s = t.shape[:2]
    return jnp.moveaxis(t.reshape((b, s // BLOCK_Q, BLOCK_Q) + t.shape[2:]), 1, 0)


def _from_blocks(t):
    n, b, q = t.shape[:3]
    return jnp.moveaxis(t, 0, 1).reshape((b, n * q) + t.shape[3:])


def diff_attention(q1, q2, k1, k2, v, lam):
    seq = q1.shape[1]
    scale = HEAD_DIM ** -0.5
    kpos = jnp.arange(seq)

    def block(args):
        i, qb1, qb2 = args
        qpos = i * BLOCK_Q + jnp.arange(BLOCK_Q)
        causal = kpos[None, :] <= qpos[:, None]

        def probs(qb, k):
            sc = jnp.einsum('bqhd,bkhd->bhqk', qb, k).astype(jnp.float32) * scale
            return jax.nn.softmax(jnp.where(causal, sc, -jnp.inf), axis=-1)

        p = probs(qb1, k1) - lam * probs(qb2, k2)
        return jnp.einsum('bhqk,bkhe->bqhe', p.astype(v.dtype), v)

    out = lax.map(block, (jnp.arange(seq // BLOCK_Q), _to_blocks(q1), _to_blocks(q2)))
    return _from_blocks(out)


def forgetting_attention(q, k, v, logf):
    seq = q.shape[1]
    scale = HEAD_DIM ** -0.5
    c = jnp.cumsum(logf, axis=1)
    c_keys = jnp.transpose(c, (0, 2, 1))
    kpos = jnp.arange(seq)

    def block(args):
        i, qb, cb = args
        qpos = i * BLOCK_Q + jnp.arange(BLOCK_Q)
        causal = kpos[None, :] <= qpos[:, None]
        decay = jnp.transpose(cb, (0, 2, 1))[..., :, None] - c_keys[:, :, None, :]
        sc = jnp.einsum('bqhd,bkhd->bhqk', qb, k).astype(jnp.float32) * scale + decay
        p = jax.nn.softmax(jnp.where(causal, sc, -jnp.inf), axis=-1)
        return jnp.einsum('bhqk,bkhe->bqhe', p.astype(v.dtype), v)

    out = lax.map(block, (jnp.arange(seq // BLOCK_Q), _to_blocks(q), _to_blocks(c)))
    return _from_blocks(out)


def dilated_window_attention(q, k, v, window, dilation):
    b, seq, h, dh = q.shape
    n = window // dilation
    span = n * dilation
    padded = -(-seq // span) * span
    nb = padded // span
    m_len = padded // dilation
    scale = HEAD_DIM ** -0.5

    def strided(t):
        t = jnp.pad(t, ((0, 0), (0, padded - seq), (0, 0), (0, 0)))
        t = jnp.transpose(t.reshape(b, m_len, dilation, h, dh), (0, 2, 1, 3, 4))
        return t.reshape(b, dilation, nb, n, h, dh)

    def band(t):
        prev = jnp.pad(t, ((0, 0), (0, 0), (1, 0), (0, 0), (0, 0), (0, 0)))[:, :, :-1]
        return jnp.concatenate([prev, t], axis=3)

    qs, kb, vb = strided(q), band(strided(k)), band(strided(v))
    sc = jnp.einsum('bcnqhe,bcnkhe->bcnhqk', qs, kb).astype(jnp.float32) * scale
    qi = jnp.arange(n)[:, None]
    kj = jnp.arange(2 * n)[None, :]
    dist = n + qi - kj
    in_window = (dist >= 0) & (dist <= n)
    has_prev = (jnp.arange(nb)[:, None, None] > 0) | (kj[None] >= n)
    mask = (in_window[None] & has_prev)[:, None]
    sc = jnp.where(mask, sc, -jnp.inf)
    lse = jax.nn.logsumexp(sc, axis=-1)
    p = jnp.exp(sc - lse[..., None])
    o = jnp.einsum('bcnhqk,bcnkhe->bcnqhe', p.astype(v.dtype), vb)
    o = jnp.transpose(o.reshape(b, dilation, m_len, h, dh), (0, 2, 1, 3, 4)).reshape(b, padded, h, dh)[:, :seq]
    lse = jnp.transpose(lse, (0, 1, 2, 4, 3)).reshape(b, dilation, m_len, h)
    lse = jnp.transpose(lse, (0, 2, 1, 3)).reshape(b, padded, h)[:, :seq]
    return o, lse


def dilated_mixture(q, k, v):
    outs, lses = [], []
    for g, (window, dilation) in enumerate(C_PATTERNS):
        o, l = dilated_window_attention(q[:, :, g], k[:, :, g], v[:, :, g], window, dilation)
        outs.append(o)
        lses.append(l)
    wts = jax.nn.softmax(jnp.stack(lses, axis=0), axis=0)
    return jnp.einsum('gbsh,gbshe->bshe', wts.astype(q.dtype), jnp.stack(outs, axis=0))


def hybrid_mixer(x, positions, w_in, b_forget, diff_lambda, diff_subln, lam_init,
                 w_branch_a, w_branch_b, w_branch_c, w_gate, b_gate, w_out):
    b, s, _ = x.shape
    proj = x @ w_in
    qa, ka, va, qb, kb, vb, fb, qc, kc, vc = jnp.split(proj, IN_SPLITS, axis=-1)
    qa = rope(qa.reshape(b, s, 2 * A_HEADS, HEAD_DIM), positions).reshape(b, s, A_HEADS, 2, HEAD_DIM)
    ka = rope(ka.reshape(b, s, 2 * A_HEADS, HEAD_DIM), positions).reshape(b, s, A_HEADS, 2, HEAD_DIM)
    va = va.reshape(b, s, A_HEADS, 2 * HEAD_DIM)
    lf = diff_lambda.astype(jnp.float32)
    lam = jnp.exp(jnp.dot(lf[0], lf[1])) - jnp.exp(jnp.dot(lf[2], lf[3])) + lam_init
    oa = diff_attention(qa[:, :, :, 0], qa[:, :, :, 1], ka[:, :, :, 0], ka[:, :, :, 1], va, lam)
    oa = (rms_norm(oa, diff_subln) * (1.0 - lam_init)).reshape(b, s, A_OUT)
    logf = jax.nn.log_sigmoid(fb.astype(jnp.float32) + b_forget.astype(jnp.float32))
    ob = forgetting_attention(qb.reshape(b, s, B_HEADS, HEAD_DIM), kb.reshape(b, s, B_HEADS, HEAD_DIM),
                              vb.reshape(b, s, B_HEADS, HEAD_DIM), logf).reshape(b, s, B_OUT)
    ch = C_GROUPS * C_SLOTS
    qc = rope(qc.reshape(b, s, ch, HEAD_DIM), positions).reshape(b, s, C_GROUPS, C_SLOTS, HEAD_DIM)
    kc = rope(kc.reshape(b, s, ch, HEAD_DIM), positions).reshape(b, s, C_GROUPS, C_SLOTS, HEAD_DIM)
    vc = vc.reshape(b, s, C_GROUPS, C_SLOTS, HEAD_DIM)
    oc = dilated_mixture(qc, kc, vc).reshape(b, s, C_OUT)
    ga, gb, gc = jnp.split(jax.nn.sigmoid(x @ w_gate + b_gate), N_BRANCH, axis=-1)
    merged = ga * (oa @ w_branch_a) + gb * (ob @ w_branch_b) + gc * (oc @ w_branch_c)
    return merged @ w_out


def memory_cross_attention(x, mem, w_q, w_k, w_v, w_o):
    b, s, _ = x.shape
    m = mem.shape[1]
    q = (x @ w_q).reshape(b, s, X_HEADS, X_HEAD_DIM)
    k = (mem @ w_k).reshape(b, m, X_HEADS, X_HEAD_DIM)
    v = (mem @ w_v).reshape(b, m, X_HEADS, X_HEAD_DIM)
    sc = jnp.einsum('bqhd,bkhd->bhqk', q, k).astype(jnp.float32) * (X_HEAD_DIM ** -0.5)
    p = jax.nn.softmax(sc, axis=-1)
    o = jnp.einsum('bhqk,bkhd->bqhd', p.astype(v.dtype), v).reshape(b, s, D_MODEL)
    return o @ w_o


def hierarchical_moe(x, w_rg, b_rg, w_re, b_re, w_gate_up, w_up, w_down):
    b, s, d = x.shape
    t = x.reshape(b * s, d)
    g_logits = (t @ w_rg).astype(jnp.float32) + b_rg
    g_probs = jax.nn.softmax(g_logits, axis=-1)
    g_top, g_idx = lax.top_k(g_logits, 1)
    g_idx = g_idx[:, 0]
    g_gate = jnp.take_along_axis(g_probs, g_idx[:, None], axis=-1)[:, 0]
    e_logits = jnp.einsum('td,gde->tge', t, w_re).astype(jnp.float32) + b_re
    e_sel = jnp.take_along_axis(e_logits, g_idx[:, None, None], axis=1)[:, 0]
    top_v, top_i = lax.top_k(e_sel, EXPERT_TOPK)
    top_w = jax.nn.softmax(top_v, axis=-1)
    e_w = jnp.sum(jax.nn.one_hot(top_i, EXPERTS_PER_GROUP, dtype=jnp.float32) * top_w[..., None], axis=1)
    gate = jax.nn.one_hot(g_idx, N_GROUPS, dtype=jnp.float32)[:, :, None] * (g_gate[:, None] * e_w)[:, None, :]
    y = jnp.zeros_like(t)
    for g in range(N_GROUPS):
        h = jax.nn.silu(jnp.einsum('td,edf->tef', t, w_gate_up[g])) * jnp.einsum('td,edf->tef', t, w_up[g])
        y = y + jnp.einsum('tef,efd->td', h * gate[:, g, :, None].astype(h.dtype), w_down[g])
    return y.reshape(b, s, d)


def setup_inputs(seed: int = 0) -> dict:
    key = jax.random.key(seed)
    ks = iter(jax.random.split(key, 32))

    def nrm(shape, scale):
        return jax.random.normal(next(ks), shape, jnp.float32) * scale

    fan = D_MODEL ** -0.5
    L = DEPTH
    x = nrm((BATCH, SEQ, D_MODEL), 1.0)
    mem = nrm((BATCH, MEM_LEN, D_MODEL), 1.0)
    offsets = jax.random.randint(next(ks), (BATCH, 1), 0, 1024, dtype=jnp.int32)
    positions = offsets + jnp.arange(SEQ, dtype=jnp.int32)[None, :]
    return {
        "x": x,
        "mem": mem,
        "positions": positions,
        "w_in": nrm((L, D_MODEL, IN_COLS), fan),
        "b_forget": jax.random.uniform(next(ks), (L, B_HEADS), jnp.float32, FORGET_BIAS_LO, FORGET_BIAS_HI),
        "diff_lambda": nrm((L, 4, HEAD_DIM), 0.1),
        "diff_subln": 1.0 + nrm((L, 2 * HEAD_DIM), 0.02),
        "w_branch_a": nrm((L, A_OUT, D_MODEL), A_OUT ** -0.5 * DN_BETA),
        "w_branch_b": nrm((L, B_OUT, D_MODEL), B_OUT ** -0.5 * DN_BETA),
        "w_branch_c": nrm((L, C_OUT, D_MODEL), C_OUT ** -0.5 * DN_BETA),
        "w_gate": nrm((L, D_MODEL, N_BRANCH * D_MODEL), fan),
        "b_gate": nrm((L, N_BRANCH * D_MODEL), 0.02),
        "w_out": nrm((L, D_MODEL, D_MODEL), fan * DN_BETA),
        "w_xq": nrm((L, D_MODEL, D_MODEL), fan),
        "w_xk": nrm((L, D_MODEL, D_MODEL), fan),
        "w_xv": nrm((L, D_MODEL, D_MODEL), fan),
        "w_xo": nrm((L, D_MODEL, D_MODEL), fan * DN_BETA),
        "w_route_group": nrm((L, D_MODEL, N_GROUPS), fan),
        "b_route_group": nrm((L, N_GROUPS), 0.01),
        "w_route_expert": nrm((L, N_GROUPS, D_MODEL, EXPERTS_PER_GROUP), fan),
        "b_route_expert": nrm((L, N_GROUPS, EXPERTS_PER_GROUP), 0.01),
        "w_expert_gate": nrm((L, N_GROUPS, EXPERTS_PER_GROUP, D_MODEL, D_EXPERT), fan),
        "w_expert_up": nrm((L, N_GROUPS, EXPERTS_PER_GROUP, D_MODEL, D_EXPERT), fan),
        "w_expert_down": nrm((L, N_GROUPS, EXPERTS_PER_GROUP, D_EXPERT, D_MODEL), D_EXPERT ** -0.5 * DN_BETA),
        "ln_g": 1.0 + nrm((L, N_SUBLAYERS, D_MODEL), 0.02),
        "ln_b": nrm((L, N_SUBLAYERS, D_MODEL), 0.02),
    }


def reference(x, mem, positions, w_in, b_forget, diff_lambda, diff_subln, w_branch_a, w_branch_b,
              w_branch_c, w_gate, b_gate, w_out, w_xq, w_xk, w_xv, w_xo, w_route_group, b_route_group,
              w_route_expert, b_route_expert, w_expert_gate, w_expert_up, w_expert_down, ln_g, ln_b):
    for l in range(DEPTH):
        lam_init = 0.8 - 0.6 * math.exp(-0.3 * l)
        h = hybrid_mixer(x, positions, w_in[l], b_forget[l], diff_lambda[l], diff_subln[l], lam_init,
                         w_branch_a[l], w_branch_b[l], w_branch_c[l], w_gate[l], b_gate[l], w_out[l])
        x = layer_norm(DN_ALPHA * x + h, ln_g[l, 0], ln_b[l, 0])
        h = memory_cross_attention(x, mem, w_xq[l], w_xk[l], w_xv[l], w_xo[l])
        x = layer_norm(DN_ALPHA * x + h, ln_g[l, 1], ln_b[l, 1])
        h = hierarchical_moe(x, w_route_group[l], b_route_group[l], w_route_expert[l], b_route_expert[l],
                             w_expert_gate[l], w_expert_up[l], w_expert_down[l])
        x = layer_norm(DN_ALPHA * x + h, ln_g[l, 2], ln_b[l, 2])
    return x
```

```python
import functools
import math

import jax
import jax.numpy as jnp
from jax import lax
from jax.experimental import pallas as pl
from jax.experimental.pallas import tpu as pltpu

F32 = jnp.float32
BF16 = jnp.bfloat16

D_MODEL = 1024
HEAD_DIM = 64
ROPE_THETA = 10000.0
EPS = 1e-5
MEM_LEN = 256
A_HEADS = 4
B_HEADS = 8
C_PATTERNS = ((128, 1), (512, 4), (2048, 16))
C_GROUPS = 3
C_SLOTS = 4
C_BLOCK = 128
X_HEADS = 4
X_HEAD_DIM = D_MODEL // X_HEADS
N_GROUPS = 4
EXPERTS_PER_GROUP = 4
N_EXPERTS = N_GROUPS * EXPERTS_PER_GROUP
D_EXPERT = D_MODEL // 4
DEPTH = 2
DN_ALPHA = (2 * DEPTH) ** 0.25

LANES = 128
NEG = -1e30
VMEM_LIMIT = 56 * 1024 * 1024

A_W = A_HEADS * 2 * HEAD_DIM
B_W = B_HEADS * HEAD_DIM
C_GW = C_SLOTS * HEAD_DIM
C_W = C_GROUPS * C_GW


def _params(sem):
    return pltpu.CompilerParams(dimension_semantics=sem, vmem_limit_bytes=VMEM_LIMIT)


def _const_spec(shape):
    nd = len(shape)
    return pl.BlockSpec(shape, lambda *_: (0,) * nd)


def _layer_norm(y, g, b):
    mu = jnp.mean(y, axis=-1, keepdims=True)
    yc = y - mu
    var = jnp.mean(yc * yc, axis=-1, keepdims=True)
    return yc * lax.rsqrt(var + EPS) * g + b


def _dot(a, b):
    return jnp.dot(a, b, preferred_element_type=F32)


def _dot_nt(a, b):
    return lax.dot_general(a, b, (((1,), (1,)), ((), ())), preferred_element_type=F32)


def _inproj_kernel(x_ref, cos_ref, sin_ref, bf_ref,
                   w_qa, w_ka, w_va, w_qb, w_kb, w_vb, w_fb, w_qc, w_kc, w_vc,
                   o_qa, o_ka, o_va, o_qb, o_kb, o_vb, o_lf,
                   o_qc0, o_qc1, o_qc2, o_kc0, o_kc1, o_kc2, o_vc0, o_vc1, o_vc2):
    xb = x_ref[...].astype(BF16)
    cos = cos_ref[...]
    sin = sin_ref[...]
    lane = lax.broadcasted_iota(jnp.int32, cos.shape, 1)
    lower = (lane & (HEAD_DIM - 1)) < HEAD_DIM // 2

    def rope(t):
        sw = jnp.where(lower, pltpu.roll(t, LANES - HEAD_DIM // 2, 1), pltpu.roll(t, HEAD_DIM // 2, 1))
        return t * cos + sw * sin

    def roped(w_ref, outs, width):
        y = _dot(xb, w_ref[...])
        per_out = width // LANES
        for c in range(y.shape[1] // LANES):
            o = outs[c // per_out]
            lo = (c % per_out) * LANES
            o[:, lo:lo + LANES] = rope(y[:, c * LANES:(c + 1) * LANES]).astype(BF16)

    def plain(w_ref, outs, width):
        y = _dot(xb, w_ref[...])
        for c, o in enumerate(outs):
            o[...] = y[:, c * width:(c + 1) * width].astype(BF16)

    roped(w_qa, (o_qa,), A_W)
    roped(w_ka, (o_ka,), A_W)
    plain(w_va, (o_va,), A_W)
    plain(w_qb, (o_qb,), B_W)
    plain(w_kb, (o_kb,), B_W)
    plain(w_vb, (o_vb,), B_W)
    roped(w_qc, (o_qc0, o_qc1, o_qc2), C_GW)
    roped(w_kc, (o_kc0, o_kc1, o_kc2), C_GW)
    plain(w_vc, (o_vc0, o_vc1, o_vc2), C_GW)
    z = _dot(xb, w_fb[...]) + bf_ref[...]
    o_lf[...] = jnp.minimum(z, 0.0) - jnp.log(1.0 + jnp.exp(-jnp.abs(z)))


def _inproj(x2d, cos, sin, bfp, ws, tm):
    t = x2d.shape[0]
    row = lambda n: pl.BlockSpec((tm, n), lambda i: (i, 0))
    widths = [A_W, A_W, A_W, B_W, B_W, B_W]
    out_shape = [jax.ShapeDtypeStruct((t, w), BF16) for w in widths]
    out_shape.append(jax.ShapeDtypeStruct((t, LANES), F32))
    out_shape += [jax.ShapeDtypeStruct((t, C_GW), BF16)] * 9
    out_specs = [row(w) for w in widths] + [row(LANES)] + [row(C_GW)] * 9
    in_specs = [row(D_MODEL), row(LANES), row(LANES), _const_spec((1, LANES))]
    in_specs += [_const_spec(w.shape) for w in ws]
    return pl.pallas_call(
        _inproj_kernel, out_shape=out_shape, grid=(t // tm,), in_specs=in_specs, out_specs=out_specs,
        compiler_params=_params(("parallel",)), name="inproj")(x2d, cos, sin, bfp, *ws)


def _flash_kernel(forget, bq, *refs):
    if forget:
        q_ref, k_ref, v_ref, c_ref, o_ref = refs
    else:
        q_ref, k_ref, v_ref, lam_ref, gain_ref, o_ref = refs
    i = pl.program_id(2)
    q = q_ref[...]
    lane = lax.broadcasted_iota(jnp.int32, (bq, LANES), 1)
    low = lane < HEAD_DIM
    zero = jnp.zeros_like(q)
    qs = (jnp.where(low, q, zero), jnp.where(low, zero, q))
    row = lax.broadcasted_iota(jnp.int32, (bq, bq), 0)
    col = lax.broadcasted_iota(jnp.int32, (bq, bq), 1)
    causal = col <= row
    q0 = pl.multiple_of(i * bq, bq)

    def chunk(j, carry, diag):
        off = pl.multiple_of(j * bq, bq)
        kc = k_ref[pl.ds(off, bq), :]
        vc = v_ref[pl.ds(off, bq), :]
        new = []
        for s in range(2):
            m, l, acc = carry[s]
            st = _dot_nt(qs[s], kc)
            if forget:
                cq = c_ref[s:s + 1, pl.ds(q0, LANES)][:, 0:1]
                st = st + (cq - c_ref[s:s + 1, pl.ds(off, bq)])
            if diag:
                st = jnp.where(causal, st, NEG)
            m_new = jnp.maximum(m, jnp.max(st, axis=-1, keepdims=True))
            a = jnp.exp(m - m_new)
            p = jnp.exp(st - m_new)
            l = a * l + jnp.sum(p, axis=-1, keepdims=True)
            acc = a * acc + _dot(p.astype(BF16), vc)
            new.append((m_new, l, acc))
        return tuple(new)

    init = tuple((jnp.full((bq, 1), NEG, F32), jnp.zeros((bq, 1), F32), jnp.zeros((bq, LANES), F32))
                 for _ in range(2))
    carry = lax.fori_loop(0, i, lambda j, c: chunk(j, c, False), init)
    (_, l0, a0), (_, l1, a1) = chunk(i, carry, True)
    o0 = a0 / l0
    o1 = a1 / l1
    if forget:
        o_ref[...] = jnp.where(low, o0, o1).astype(o_ref.dtype)
    else:
        o = o0 - lam_ref[...] * o1
        ms = jnp.mean(o * o, axis=-1, keepdims=True)
        o_ref[...] = (o * lax.rsqrt(ms + EPS) * gain_ref[...]).astype(o_ref.dtype)


def _flash(q, k, v, extra, forget, bq):
    b, s, w = q.shape
    g = w // LANES
    qspec = pl.BlockSpec((None, bq, LANES), lambda bi, gi, i: (bi, i, gi))
    kspec = pl.BlockSpec((None, s, LANES), lambda bi, gi, i: (bi, 0, gi))
    if forget:
        (c_t,) = extra
        especs = [pl.BlockSpec((None, None, 2, s), lambda bi, gi, i: (bi, gi, 0, 0))]
    else:
        especs = [_const_spec((1, LANES)), _const_spec((1, LANES))]
    return pl.pallas_call(
        functools.partial(_flash_kernel, forget, bq),
        out_shape=jax.ShapeDtypeStruct((b, s, w), BF16),
        grid=(b, g, s // bq), in_specs=[qspec, kspec, kspec] + especs, out_specs=qspec,
        compiler_params=_params(("parallel", "parallel", "arbitrary")),
        name="fox_attn" if forget else "diff_attn")(q, k, v, *extra)


def _dilated_kernel(q_ref, kp_ref, kc_ref, vp_ref, vc_ref, o_ref, lse_ref):
    n = C_BLOCK
    j = pl.program_id(2)
    q = q_ref[...]
    kcat = jnp.concatenate([kp_ref[...], kc_ref[...]], axis=0)
    vcat = jnp.concatenate([vp_ref[...], vc_ref[...]], axis=0)
    qi = lax.broadcasted_iota(jnp.int32, (n, 2 * n), 0)
    kj = lax.broadcasted_iota(jnp.int32, (n, 2 * n), 1)
    dist = n + qi - kj
    first_key = jnp.where(j > 0, 0, n)
    mask = (dist >= 0) & (dist <= n) & (kj >= first_key)
    lane = lax.broadcasted_iota(jnp.int32, (n, LANES), 1)
    low = lane < HEAD_DIM
    for p in range(C_GW // LANES):
        sl = slice(p * LANES, (p + 1) * LANES)
        q128, k128, v128 = q[:, sl], kcat[:, sl], vcat[:, sl]
        zero = jnp.zeros_like(q128)
        os, ls = [], []
        for s in range(2):
            qm = jnp.where(low, q128, zero) if s == 0 else jnp.where(low, zero, q128)
            st = jnp.where(mask, _dot_nt(qm, k128), NEG)
            m = jnp.max(st, axis=-1, keepdims=True)
            e = jnp.exp(st - m)
            l = jnp.sum(e, axis=-1, keepdims=True)
            os.append(_dot(e.astype(BF16), v128) / l)
            ls.append(m + jnp.log(l))
        o_ref[:, sl] = jnp.where(low, os[0], os[1])
        lse_ref[:, sl] = jnp.where(low, ls[0], ls[1])


def _dilated(q, k, v, dilation):
    b, s, w = q.shape
    n = C_BLOCK
    m_len = s // dilation
    nb = m_len // n
    view = lambda t: t.reshape(b, m_len, dilation * w)
    cur = pl.BlockSpec((None, n, w), lambda bi, r, j: (bi, j, r))
    prev = pl.BlockSpec((None, n, w), lambda bi, r, j: (bi, jnp.maximum(j - 1, 0), r))
    o, lse = pl.pallas_call(
        _dilated_kernel,
        out_shape=[jax.ShapeDtypeStruct((b, m_len, dilation * w), F32)] * 2,
        grid=(b, dilation, nb), in_specs=[cur, prev, cur, prev, cur], out_specs=[cur, cur],
        compiler_params=_params(("parallel", "parallel", "arbitrary")),
        name=f"dilated_d{dilation}")(view(q), view(k), view(k), view(v), view(v))
    return o.reshape(b * s, w), lse.reshape(b * s, w)


def _merge_kernel(x_ref, oa_ref, ob_ref, oc0, oc1, oc2, ls0, ls1, ls2,
                  wg_ref, bg_ref, wa_ref, wb_ref, wc_ref, wo_ref, g_ref, b_ref, o_ref):
    x = x_ref[...]
    xb = x.astype(BF16)
    l0, l1, l2 = ls0[...], ls1[...], ls2[...]
    m = jnp.maximum(jnp.maximum(l0, l1), l2)
    e0, e1, e2 = jnp.exp(l0 - m), jnp.exp(l1 - m), jnp.exp(l2 - m)
    oc = (e0 * oc0[...] + e1 * oc1[...] + e2 * oc2[...]) / (e0 + e1 + e2)
    branches = ((oa_ref[...], wa_ref), (ob_ref[...], wb_ref), (oc.astype(BF16), wc_ref))
    merged = None
    for k, (ob, w_ref) in enumerate(branches):
        sl = slice(k * D_MODEL, (k + 1) * D_MODEL)
        z = _dot(xb, wg_ref[:, sl]) + bg_ref[:, sl]
        gate = 1.0 / (1.0 + jnp.exp(-z))
        term = gate * _dot(ob, w_ref[...])
        merged = term if merged is None else merged + term
    h = _dot(merged.astype(BF16), wo_ref[...])
    o_ref[...] = _layer_norm(DN_ALPHA * x + h, g_ref[...], b_ref[...])


def _merge(x2d, oa, ob, ocs, lses, wg, bg, wa, wb, wc, wo, g, bta, tm):
    t = x2d.shape[0]
    row = lambda n: pl.BlockSpec((tm, n), lambda i: (i, 0))
    consts = (wg, bg, wa, wb, wc, wo, g, bta)
    in_specs = [row(D_MODEL), row(A_W), row(B_W)] + [row(C_GW)] * 6 + [_const_spec(c.shape) for c in consts]
    return pl.pallas_call(
        _merge_kernel, out_shape=jax.ShapeDtypeStruct((t, D_MODEL), F32), grid=(t // tm,),
        in_specs=in_specs, out_specs=row(D_MODEL),
        compiler_params=_params(("parallel",)), name="merge")(x2d, oa, ob, *ocs, *lses, *consts)


def _memkv_kernel(m_ref, wk_ref, wv_ref, k_ref, v_ref):
    mb = m_ref[...].astype(BF16)
    k_ref[...] = _dot(mb, wk_ref[...]).astype(BF16)
    v_ref[...] = _dot(mb, wv_ref[...]).astype(BF16)


def _memkv(mem2d, wk, wv, tm):
    t = mem2d.shape[0]
    row = pl.BlockSpec((tm, D_MODEL), lambda i: (i, 0))
    return pl.pallas_call(
        _memkv_kernel, out_shape=[jax.ShapeDtypeStruct((t, D_MODEL), BF16)] * 2, grid=(t // tm,),
        in_specs=[row, _const_spec(wk.shape), _const_spec(wv.shape)], out_specs=[row, row],
        compiler_params=_params(("parallel",)), name="memkv")(mem2d, wk, wv)


def _xattn_kernel(x_ref, k_ref, v_ref, wq_ref, wo_ref, g_ref, b_ref, o_ref):
    x = x_ref[...]
    q = _dot(x.astype(BF16), wq_ref[...]).astype(BF16)
    heads = []
    for h in range(X_HEADS):
        sl = slice(h * X_HEAD_DIM, (h + 1) * X_HEAD_DIM)
        st = _dot_nt(q[:, sl], k_ref[:, sl])
        m = jnp.max(st, axis=-1, keepdims=True)
        e = jnp.exp(st - m)
        l = jnp.sum(e, axis=-1, keepdims=True)
        heads.append((_dot(e.astype(BF16), v_ref[:, sl]) / l).astype(BF16))
    o = jnp.concatenate(heads, axis=-1)
    h_out = _dot(o, wo_ref[...])
    o_ref[...] = _layer_norm(DN_ALPHA * x + h_out, g_ref[...], b_ref[...])


def _xattn(x3d, k3d, v3d, wq, wo, g, bta, tm):
    b, s, d = x3d.shape
    xspec = pl.BlockSpec((None, tm, d), lambda bi, i: (bi, i, 0))
    mspec = pl.BlockSpec((None, MEM_LEN, d), lambda bi, i: (bi, 0, 0))
    consts = (wq, wo, g, bta)
    return pl.pallas_call(
        _xattn_kernel, out_shape=jax.ShapeDtypeStruct((b, s, d), F32), grid=(b, s // tm),
        in_specs=[xspec, mspec, mspec] + [_const_spec(c.shape) for c in consts], out_specs=xspec,
        compiler_params=_params(("parallel", "parallel")), name="xattn")(x3d, k3d, v3d, *consts)


MOE_EPB = 2
ROUTE_LANE0 = N_GROUPS


def _moe_kernel(x_ref, wrh_ref, wrl_ref, br_ref, wg_ref, wu_ref, wd_ref, g_ref, b_ref, o_ref,
                xb_ref, gate_ref, acc_ref):
    eb = pl.program_id(1)
    tm = x_ref.shape[0]
    lane = lax.broadcasted_iota(jnp.int32, (tm, LANES), 1)

    @pl.when(eb == 0)
    def _route():
        x = x_ref[...]
        lanef = lane.astype(F32)
        xh = x.astype(BF16)
        xl = (x - xh.astype(F32)).astype(BF16)
        xb_ref[...] = xh
        logits = (_dot(xh, wrh_ref[...]) + _dot(xl, wrh_ref[...]) + _dot(xh, wrl_ref[...])) + br_ref[...]
        gl = jnp.where(lane < N_GROUPS, logits, NEG)
        gmax = jnp.max(gl, axis=-1, keepdims=True)
        g_gate = 1.0 / jnp.sum(jnp.exp(gl - gmax), axis=-1, keepdims=True)
        g_idx = jnp.min(jnp.where(gl == gmax, lanef, float(LANES)), axis=-1, keepdims=True)
        e_group = ((lane - ROUTE_LANE0) >> 2).astype(F32)
        sel = (lane >= ROUTE_LANE0) & (e_group == g_idx)
        el = jnp.where(sel, logits, NEG)
        top1 = jnp.max(el, axis=-1, keepdims=True)
        i1 = jnp.min(jnp.where(el == top1, lanef, float(LANES)), axis=-1, keepdims=True)
        el2 = jnp.where(lanef == i1, NEG, el)
        top2 = jnp.max(el2, axis=-1, keepdims=True)
        i2 = jnp.min(jnp.where(el2 == top2, lanef, float(LANES)), axis=-1, keepdims=True)
        r = jnp.exp(top2 - top1)
        w1 = 1.0 / (1.0 + r)
        w2 = r * w1
        gate_ref[...] = jnp.where(lanef == i1, w1, jnp.where(lanef == i2, w2, 0.0)) * g_gate
        acc_ref[...] = jnp.zeros_like(acc_ref)

    xb = xb_ref[...]
    gates = gate_ref[...]
    for k in range(MOE_EPB):
        e_lane = ROUTE_LANE0 + eb * MOE_EPB + k
        gcol = jnp.sum(jnp.where(lane == e_lane, gates, 0.0), axis=-1, keepdims=True)
        a = _dot(xb, wg_ref[k])
        u = _dot(xb, wu_ref[k])
        h = (a / (1.0 + jnp.exp(-a))) * u * gcol
        acc_ref[...] += _dot(h.astype(BF16), wd_ref[k])

    @pl.when(eb == pl.num_programs(1) - 1)
    def _finish():
        o_ref[...] = _layer_norm(DN_ALPHA * x_ref[...] + acc_ref[...], g_ref[...], b_ref[...])


def _moe(x2d, wrh, wrl, br, wg, wu, wd, g, bta, tm):
    t = x2d.shape[0]
    row = pl.BlockSpec((tm, D_MODEL), lambda i, e: (i, 0))
    wspec = lambda shape: pl.BlockSpec((MOE_EPB,) + shape, lambda i, e: (e, 0, 0))
    in_specs = [row, _const_spec(wrh.shape), _const_spec(wrl.shape), _const_spec(br.shape),
                wspec((D_MODEL, D_EXPERT)), wspec((D_MODEL, D_EXPERT)), wspec((D_EXPERT, D_MODEL)),
                _const_spec(g.shape), _const_spec(bta.shape)]
    return pl.pallas_call(
        _moe_kernel, out_shape=jax.ShapeDtypeStruct((t, D_MODEL), F32),
        grid=(t // tm, N_EXPERTS // MOE_EPB), in_specs=in_specs, out_specs=row,
        scratch_shapes=[pltpu.VMEM((tm, D_MODEL), BF16), pltpu.VMEM((tm, LANES), F32),
                        pltpu.VMEM((tm, D_MODEL), F32)],
        compiler_params=_params(("parallel", "arbitrary")), name="moe")(x2d, wrh, wrl, br, wg, wu, wd, g, bta)


def _pad_cols(w, n):
    return jnp.pad(w, ((0, 0), (0, n - w.shape[1])))


def kernel(x, mem, positions, w_in, b_forget, diff_lambda, diff_subln, w_branch_a, w_branch_b, w_branch_c,
           w_gate, b_gate, w_out, w_xq, w_xk, w_xv, w_xo, w_route_group, b_route_group, w_route_expert,
           b_route_expert, w_expert_gate, w_expert_up, w_expert_down, ln_g, ln_b):
    b, s, d = x.shape
    t = b * s
    depth = w_in.shape[0]
    scale = HEAD_DIM ** -0.5

    half = HEAD_DIM // 2
    inv = ROPE_THETA ** (-jnp.arange(half, dtype=F32) / half)
    ang = positions.astype(F32).reshape(t, 1) * inv
    cos, sin = jnp.cos(ang), jnp.sin(ang)
    cos_t = jnp.concatenate([cos, cos, cos, cos], axis=-1)
    sin_t = jnp.concatenate([-sin, sin, -sin, sin], axis=-1)

    mem2d = mem.reshape(b * MEM_LEN, d)
    x2d = x.reshape(t, d)
    row = lambda v: v.reshape(1, -1).astype(F32)
    for l in range(depth):
        lam_init = 0.8 - 0.6 * math.exp(-0.3 * l)
        offs = [0, A_W, 2 * A_W, 3 * A_W, 3 * A_W + B_W, 3 * A_W + 2 * B_W, 3 * A_W + 3 * B_W]
        offs.append(offs[-1] + B_HEADS)
        offs += [offs[-1] + C_W, offs[-1] + 2 * C_W, offs[-1] + 3 * C_W]
        wl = w_in[l]
        seg = [wl[:, offs[k]:offs[k + 1]] for k in range(10)]
        qa, ka, va, qb, kb, vb, fb, qc, kc, vc = seg
        ws = [(qa * scale), ka, va, (qb * scale), kb, vb, _pad_cols(fb, LANES), (qc * scale), kc, vc]
        ws = [w.astype(BF16) for w in ws]
        bfp = _pad_cols(row(b_forget[l]), LANES)

        outs = _inproj(x2d, cos_t, sin_t, bfp, ws, tm=512)
        o_qa, o_ka, o_va, o_qb, o_kb, o_vb, o_lf = outs[:7]
        o_qc, o_kc, o_vc = outs[7:10], outs[10:13], outs[13:16]

        lf = diff_lambda[l].astype(F32)
        lam = jnp.exp(jnp.dot(lf[0], lf[1])) - jnp.exp(jnp.dot(lf[2], lf[3])) + lam_init
        lam_row = jnp.full((1, LANES), lam, F32)
        gain = row(diff_subln[l]) * (1.0 - lam_init)
        r3 = lambda v: v.reshape(b, s, v.shape[-1])
        oa = _flash(r3(o_qa), r3(o_ka), r3(o_va), (lam_row, gain), forget=False, bq=256)
        c = jnp.cumsum(o_lf.reshape(b, s, LANES)[:, :, :B_HEADS], axis=1)
        c_t = jnp.transpose(c, (0, 2, 1)).reshape(b, B_HEADS // 2, 2, s)
        ob = _flash(r3(o_qb), r3(o_kb), r3(o_vb), (c_t,), forget=True, bq=256)
        ocs, lses = [], []
        for gi, (_, dil) in enumerate(C_PATTERNS):
            o_g, lse_g = _dilated(r3(o_qc[gi]), r3(o_kc[gi]), r3(o_vc[gi]), dil)
            ocs.append(o_g)
            lses.append(lse_g)
        x2d = _merge(x2d, oa.reshape(t, A_W), ob.reshape(t, B_W), ocs, lses,
                     w_gate[l].astype(BF16), row(b_gate[l]), w_branch_a[l].astype(BF16),
                     w_branch_b[l].astype(BF16), w_branch_c[l].astype(BF16), w_out[l].astype(BF16),
                     row(ln_g[l, 0]), row(ln_b[l, 0]), tm=256)
        xscale = X_HEAD_DIM ** -0.5
        mk, mv = _memkv(mem2d, w_xk[l].astype(BF16), w_xv[l].astype(BF16), tm=MEM_LEN)
        x2d = _xattn(x2d.reshape(b, s, d), mk.reshape(b, MEM_LEN, d), mv.reshape(b, MEM_LEN, d),
                     (w_xq[l] * xscale).astype(BF16), w_xo[l].astype(BF16),
                     row(ln_g[l, 1]), row(ln_b[l, 1]), tm=512).reshape(t, d)
        wr = jnp.concatenate([w_route_group[l]] + [w_route_expert[l, gi] for gi in range(N_GROUPS)], axis=1)
        wr = _pad_cols(wr.astype(F32), LANES)
        wrh = wr.astype(BF16)
        wrl = (wr - wrh.astype(F32)).astype(BF16)
        br = _pad_cols(row(jnp.concatenate([b_route_group[l], b_route_expert[l].reshape(-1)])), LANES)
        ne = (N_EXPERTS,)
        x2d = _moe(x2d, wrh, wrl, br,
                   w_expert_gate[l].reshape(ne + w_expert_gate.shape[3:]).astype(BF16),
                   w_expert_up[l].reshape(ne + w_expert_up.shape[3:]).astype(BF16),
                   w_expert_down[l].reshape(ne + w_expert_down.shape[3:]).astype(BF16),
                   row(ln_g[l, 2]), row(ln_b[l, 2]), tm=1024)
    return x2d.reshape(b, s, d)
```

```python
import functools
import math

import jax
import jax.numpy as jnp
from jax import lax
from jax.experimental import pallas as pl
from jax.experimental.pallas import tpu as pltpu

F32 = jnp.float32
BF16 = jnp.bfloat16

D_MODEL = 1024
HEAD_DIM = 64
ROPE_THETA = 10000.0
EPS = 1e-5
MEM_LEN = 256
A_HEADS = 4
B_HEADS = 8
C_PATTERNS = ((128, 1), (512, 4), (2048, 16))
C_GROUPS = 3
C_SLOTS = 4
C_BLOCK = 128
X_HEADS = 4
X_HEAD_DIM = D_MODEL // X_HEADS
N_GROUPS = 4
EXPERTS_PER_GROUP = 4
N_EXPERTS = N_GROUPS * EXPERTS_PER_GROUP
D_EXPERT = D_MODEL // 4
DEPTH = 2
DN_ALPHA = (2 * DEPTH) ** 0.25

LANES = 128
NEG = -1e30
VMEM_LIMIT = 56 * 1024 * 1024

A_W = A_HEADS * 2 * HEAD_DIM
B_W = B_HEADS * HEAD_DIM
C_GW = C_SLOTS * HEAD_DIM
C_W = C_GROUPS * C_GW
FLASH_BK = 512


def _params(sem):
    return pltpu.CompilerParams(dimension_semantics=sem, vmem_limit_bytes=VMEM_LIMIT)


def _const_spec(shape):
    nd = len(shape)
    return pl.BlockSpec(shape, lambda *_: (0,) * nd)


def _layer_norm(y, g, b):
    mu = jnp.mean(y, axis=-1, keepdims=True)
    yc = y - mu
    var = jnp.mean(yc * yc, axis=-1, keepdims=True)
    return yc * lax.rsqrt(var + EPS) * g + b


def _dot(a, b):
    return jnp.dot(a, b, preferred_element_type=F32)


def _dot_nt(a, b):
    return lax.dot_general(a, b, (((1,), (1,)), ((), ())), preferred_element_type=F32)


def _inproj_kernel(x_ref, cos_ref, sin_ref, bf_ref,
                   w_qa, w_ka, w_va, w_qb, w_kb, w_vb, w_fb, w_qc, w_kc, w_vc,
                   o_qa, o_ka, o_va, o_qb, o_kb, o_vb, o_lf,
                   o_qc0, o_qc1, o_qc2, o_kc0, o_kc1, o_kc2, o_vc0, o_vc1, o_vc2):
    xb = x_ref[...].astype(BF16)
    cos = cos_ref[...]
    sin = sin_ref[...]
    lane = lax.broadcasted_iota(jnp.int32, cos.shape, 1)
    lower = (lane & (HEAD_DIM - 1)) < HEAD_DIM // 2

    def rope(t):
        sw = jnp.where(lower, pltpu.roll(t, LANES - HEAD_DIM // 2, 1), pltpu.roll(t, HEAD_DIM // 2, 1))
        return t * cos + sw * sin

    def roped(w_ref, outs, width):
        y = _dot(xb, w_ref[...])
        per_out = width // LANES
        for c in range(y.shape[1] // LANES):
            o = outs[c // per_out]
            lo = (c % per_out) * LANES
            o[:, lo:lo + LANES] = rope(y[:, c * LANES:(c + 1) * LANES]).astype(BF16)

    def plain(w_ref, outs, width):
        y = _dot(xb, w_ref[...])
        for c, o in enumerate(outs):
            o[...] = y[:, c * width:(c + 1) * width].astype(BF16)

    def transposed(wt_ref, o):
        yt = _dot_nt(wt_ref[...], xb)
        for c in range(o.shape[0]):
            o[c] = yt[:, c * FLASH_BK:(c + 1) * FLASH_BK].astype(BF16)

    roped(w_qa, (o_qa,), A_W)
    roped(w_ka, (o_ka,), A_W)
    transposed(w_va, o_va)
    plain(w_qb, (o_qb,), B_W)
    plain(w_kb, (o_kb,), B_W)
    transposed(w_vb, o_vb)
    roped(w_qc, (o_qc0, o_qc1, o_qc2), C_GW)
    roped(w_kc, (o_kc0, o_kc1, o_kc2), C_GW)
    plain(w_vc, (o_vc0, o_vc1, o_vc2), C_GW)
    z = _dot(xb, w_fb[...]) + bf_ref[...]
    o_lf[...] = jnp.minimum(z, 0.0) - jnp.log(1.0 + jnp.exp(-jnp.abs(z)))


def _inproj(x2d, cos, sin, bfp, ws, tm):
    t = x2d.shape[0]
    row = lambda n: pl.BlockSpec((tm, n), lambda i: (i, 0))
    tshape = jax.ShapeDtypeStruct((t // FLASH_BK, A_W, FLASH_BK), BF16)
    tspec = pl.BlockSpec((tm // FLASH_BK, A_W, FLASH_BK), lambda i: (i, 0, 0))
    rshape = lambda w: jax.ShapeDtypeStruct((t, w), BF16)
    out_shape = [rshape(A_W), rshape(A_W), tshape, rshape(B_W), rshape(B_W), tshape]
    out_shape.append(jax.ShapeDtypeStruct((t, LANES), F32))
    out_shape += [rshape(C_GW)] * 9
    out_specs = [row(A_W), row(A_W), tspec, row(B_W), row(B_W), tspec, row(LANES)] + [row(C_GW)] * 9
    in_specs = [row(D_MODEL), row(LANES), row(LANES), _const_spec((1, LANES))]
    in_specs += [_const_spec(w.shape) for w in ws]
    return pl.pallas_call(
        _inproj_kernel, out_shape=out_shape, grid=(t // tm,), in_specs=in_specs, out_specs=out_specs,
        compiler_params=_params(("parallel",)), name="inproj")(x2d, cos, sin, bfp, *ws)


def _flash_kernel(forget, bq, *refs):
    bk = FLASH_BK
    if forget:
        q_ref, k_ref, vt_ref, c_ref, o_ref, m_ref, l_ref, acc_ref, st_a, st_b, bias_ref = refs
    else:
        q_ref, k_ref, vt_ref, lam_ref, gain_ref, o_ref, m_ref, l_ref, acc_ref, st_a, st_b = refs
    dv = acc_ref.shape[1]
    i = pl.program_id(2)
    q = q_ref[...].astype(F32)
    lane = lax.broadcasted_iota(jnp.int32, (bq, LANES), 1)
    low = lane < HEAD_DIM
    qts = (jnp.where(low, q, 0.0).T.astype(BF16), jnp.where(low, 0.0, q).T.astype(BF16))
    key = lax.broadcasted_iota(jnp.int32, (bk, bq), 0)
    qry = lax.broadcasted_iota(jnp.int32, (bk, bq), 1)

    if forget:
        @pl.when(i == 0)
        def _bias():
            for s in range(2):
                bias_ref[s] = jnp.broadcast_to(-c_ref[:, s:s + 1], bias_ref.shape[1:])

    m_ref[...] = jnp.full(m_ref.shape, NEG, F32)
    l_ref[...] = jnp.zeros(l_ref.shape, F32)
    acc_ref[...] = jnp.zeros(acc_ref.shape, F32)

    def scores(j, st_ref):
        off = pl.multiple_of(j * bk, bk)
        kc = k_ref[pl.ds(off, bk), :]
        for s in range(2):
            st_ref[s] = _dot(kc, qts[s])

    def absorb(j, st_ref, diagonal):
        off = pl.multiple_of(j * bk, bk)
        for s in range(2):
            st = st_ref[s]
            if forget:
                bb = bias_ref[s, pl.ds(off, bk), :]
                st = st + jnp.concatenate([bb] * (bq // LANES), axis=1)
            if diagonal:
                st = jnp.where(key <= qry, st, NEG)
            m_prev = m_ref[s]
            m_new = jnp.maximum(m_prev, jnp.max(st, axis=0, keepdims=True))
            a = jnp.exp(m_prev - m_new)
            p = jnp.exp(st - m_new)
            l_ref[s] = a * l_ref[s] + jnp.sum(p, axis=0, keepdims=True)
            m_ref[s] = m_new
            vt = vt_ref[j, s * dv:(s + 1) * dv, :] if forget else vt_ref[j]
            acc_ref[s] = a * acc_ref[s] + _dot(vt, p.astype(BF16))

    scores(0, st_a)

    def pair(t, c):
        scores(2 * t + 1, st_b)
        absorb(2 * t, st_a, False)
        scores(2 * t + 2, st_a)
        absorb(2 * t + 1, st_b, False)
        return c

    lax.fori_loop(0, i // 2, pair, 0)

    @pl.when(i % 2 == 0)
    def _even_tail():
        absorb(i, st_a, True)

    @pl.when(i % 2 == 1)
    def _odd_tail():
        scores(i, st_b)
        absorb(i - 1, st_a, False)
        absorb(i, st_b, True)

    o0 = acc_ref[0] / l_ref[0]
    o1 = acc_ref[1] / l_ref[1]
    if forget:
        o_ref[...] = jnp.concatenate([o0, o1], axis=0).T.astype(o_ref.dtype)
    else:
        o = (o0 - lam_ref[:, 0:1] * o1).T
        ms = jnp.mean(o * o, axis=-1, keepdims=True)
        o_ref[...] = (o * lax.rsqrt(ms + EPS) * gain_ref[...]).astype(o_ref.dtype)


def _flash(q, k, vt, extra, forget, bq):
    b, s, w = q.shape
    g = w // LANES
    nk = s // FLASH_BK
    dv = HEAD_DIM if forget else LANES
    qspec = pl.BlockSpec((None, bq, LANES), lambda bi, gi, i: (bi, i, gi))
    kspec = pl.BlockSpec((None, s, LANES), lambda bi, gi, i: (bi, 0, gi))
    vspec = pl.BlockSpec((nk, LANES, FLASH_BK), lambda bi, gi, i: (bi, gi, 0))
    assert bq == FLASH_BK
    scratch = [pltpu.VMEM((2, 1, bq), F32), pltpu.VMEM((2, 1, bq), F32), pltpu.VMEM((2, dv, bq), F32),
               pltpu.VMEM((2, FLASH_BK, bq), F32), pltpu.VMEM((2, FLASH_BK, bq), F32)]
    if forget:
        especs = [pl.BlockSpec((None, None, s, 2), lambda bi, gi, i: (bi, gi, 0, 0))]
        scratch.append(pltpu.VMEM((2, s, LANES), F32))
    else:
        especs = [_const_spec((1, LANES)), _const_spec((1, LANES))]
    return pl.pallas_call(
        functools.partial(_flash_kernel, forget, bq),
        out_shape=jax.ShapeDtypeStruct((b, s, w), BF16),
        grid=(b, g, s // bq), in_specs=[qspec, kspec, vspec] + especs, out_specs=qspec,
        scratch_shapes=scratch,
        compiler_params=_params(("parallel", "parallel", "arbitrary")),
        name="fox_attn" if forget else "diff_attn")(q, k, vt, *extra)


def _dilated_kernel(q_ref, kp_ref, kc_ref, vp_ref, vc_ref, o_ref, lse_ref):
    n = C_BLOCK
    j = pl.program_id(2)
    q = q_ref[...]
    kcat = jnp.concatenate([kp_ref[...], kc_ref[...]], axis=0)
    vcat = jnp.concatenate([vp_ref[...], vc_ref[...]], axis=0)
    qi = lax.broadcasted_iota(jnp.int32, (n, 2 * n), 0)
    kj = lax.broadcasted_iota(jnp.int32, (n, 2 * n), 1)
    dist = n + qi - kj
    first_key = jnp.where(j > 0, 0, n)
    mask = (dist >= 0) & (dist <= n) & (kj >= first_key)
    lane = lax.broadcasted_iota(jnp.int32, (n, LANES), 1)
    low = lane < HEAD_DIM
    for p in range(C_GW // LANES):
        sl = slice(p * LANES, (p + 1) * LANES)
        q128, k128, v128 = q[:, sl], kcat[:, sl], vcat[:, sl]
        zero = jnp.zeros_like(q128)
        os, ls = [], []
        for s in range(2):
            qm = jnp.where(low, q128, zero) if s == 0 else jnp.where(low, zero, q128)
            st = jnp.where(mask, _dot_nt(qm, k128), NEG)
            m = jnp.max(st, axis=-1, keepdims=True)
            e = jnp.exp(st - m)
            l = jnp.sum(e, axis=-1, keepdims=True)
            os.append(_dot(e.astype(BF16), v128) / l)
            ls.append(m + jnp.log(l))
        o_ref[:, sl] = jnp.where(low, os[0], os[1])
        lse_ref[:, sl] = jnp.where(low, ls[0], ls[1])


def _dilated(q, k, v, dilation):
    b, s, w = q.shape
    n = C_BLOCK
    m_len = s // dilation
    nb = m_len // n
    view = lambda t: t.reshape(b, m_len, dilation * w)
    cur = pl.BlockSpec((None, n, w), lambda bi, r, j: (bi, j, r))
    prev = pl.BlockSpec((None, n, w), lambda bi, r, j: (bi, jnp.maximum(j - 1, 0), r))
    o, lse = pl.pallas_call(
        _dilated_kernel,
        out_shape=[jax.ShapeDtypeStruct((b, m_len, dilation * w), F32)] * 2,
        grid=(b, dilation, nb), in_specs=[cur, prev, cur, prev, cur], out_specs=[cur, cur],
        compiler_params=_params(("parallel", "parallel", "arbitrary")),
        name=f"dilated_d{dilation}")(view(q), view(k), view(k), view(v), view(v))
    return o.reshape(b * s, w), lse.reshape(b * s, w)


def _merge_kernel(x_ref, oa_ref, ob_ref, oc0, oc1, oc2, ls0, ls1, ls2,
                  wg_ref, bg_ref, wa_ref, wb_ref, wc_ref, wo_ref, g_ref, b_ref, o_ref):
    x = x_ref[...]
    xb = x.astype(BF16)
    l0, l1, l2 = ls0[...], ls1[...], ls2[...]
    m = jnp.maximum(jnp.maximum(l0, l1), l2)
    e0, e1, e2 = jnp.exp(l0 - m), jnp.exp(l1 - m), jnp.exp(l2 - m)
    oc = (e0 * oc0[...] + e1 * oc1[...] + e2 * oc2[...]) / (e0 + e1 + e2)
    branches = ((oa_ref[...], wa_ref), (ob_ref[...], wb_ref), (oc.astype(BF16), wc_ref))
    merged = None
    for k, (ob, w_ref) in enumerate(branches):
        sl = slice(k * D_MODEL, (k + 1) * D_MODEL)
        z = _dot(xb, wg_ref[:, sl]) + bg_ref[:, sl]
        gate = 1.0 / (1.0 + jnp.exp(-z))
        term = gate * _dot(ob, w_ref[...])
        merged = term if merged is None else merged + term
    h = _dot(merged.astype(BF16), wo_ref[...])
    o_ref[...] = _layer_norm(DN_ALPHA * x + h, g_ref[...], b_ref[...])


def _merge(x2d, oa, ob, ocs, lses, wg, bg, wa, wb, wc, wo, g, bta, tm):
    t = x2d.shape[0]
    row = lambda n: pl.BlockSpec((tm, n), lambda i: (i, 0))
    consts = (wg, bg, wa, wb, wc, wo, g, bta)
    in_specs = [row(D_MODEL), row(A_W), row(B_W)] + [row(C_GW)] * 6 + [_const_spec(c.shape) for c in consts]
    return pl.pallas_call(
        _merge_kernel, out_shape=jax.ShapeDtypeStruct((t, D_MODEL), F32), grid=(t // tm,),
        in_specs=in_specs, out_specs=row(D_MODEL),
        compiler_params=_params(("parallel",)), name="merge")(x2d, oa, ob, *ocs, *lses, *consts)


def _memkv_kernel(m_ref, wk_ref, wv_ref, k_ref, v_ref):
    mb = m_ref[...].astype(BF16)
    k_ref[...] = _dot(mb, wk_ref[...]).astype(BF16)
    v_ref[...] = _dot(mb, wv_ref[...]).astype(BF16)


def _memkv(mem2d, wk, wv, tm):
    t = mem2d.shape[0]
    row = pl.BlockSpec((tm, D_MODEL), lambda i: (i, 0))
    return pl.pallas_call(
        _memkv_kernel, out_shape=[jax.ShapeDtypeStruct((t, D_MODEL), BF16)] * 2, grid=(t // tm,),
        in_specs=[row, _const_spec(wk.shape), _const_spec(wv.shape)], out_specs=[row, row],
        compiler_params=_params(("parallel",)), name="memkv")(mem2d, wk, wv)


def _xattn_kernel(x_ref, k_ref, v_ref, wq_ref, wo_ref, g_ref, b_ref, o_ref):
    x = x_ref[...]
    q = _dot(x.astype(BF16), wq_ref[...]).astype(BF16)
    heads = []
    for h in range(X_HEADS):
        sl = slice(h * X_HEAD_DIM, (h + 1) * X_HEAD_DIM)
        st = _dot_nt(q[:, sl], k_ref[:, sl])
        m = jnp.max(st, axis=-1, keepdims=True)
        e = jnp.exp(st - m)
        l = jnp.sum(e, axis=-1, keepdims=True)
        heads.append((_dot(e.astype(BF16), v_ref[:, sl]) / l).astype(BF16))
    o = jnp.concatenate(heads, axis=-1)
    h_out = _dot(o, wo_ref[...])
    o_ref[...] = _layer_norm(DN_ALPHA * x + h_out, g_ref[...], b_ref[...])


def _xattn(x3d, k3d, v3d, wq, wo, g, bta, tm):
    b, s, d = x3d.shape
    xspec = pl.BlockSpec((None, tm, d), lambda bi, i: (bi, i, 0))
    mspec = pl.BlockSpec((None, MEM_LEN, d), lambda bi, i: (bi, 0, 0))
    consts = (wq, wo, g, bta)
    return pl.pallas_call(
        _xattn_kernel, out_shape=jax.ShapeDtypeStruct((b, s, d), F32), grid=(b, s // tm),
        in_specs=[xspec, mspec, mspec] + [_const_spec(c.shape) for c in consts], out_specs=xspec,
        compiler_params=_params(("parallel", "parallel")), name="xattn")(x3d, k3d, v3d, *consts)


MOE_EPB = 2
ROUTE_LANE0 = N_GROUPS


def _moe_kernel(x_ref, wrh_ref, wrl_ref, br_ref, wg_ref, wu_ref, wd_ref, g_ref, b_ref, o_ref,
                xb_ref, gate_ref, acc_ref):
    eb = pl.program_id(1)
    tm = x_ref.shape[0]
    lane = lax.broadcasted_iota(jnp.int32, (tm, LANES), 1)

    @pl.when(eb == 0)
    def _route():
        x = x_ref[...]
        lanef = lane.astype(F32)
        xh = x.astype(BF16)
        xl = (x - xh.astype(F32)).astype(BF16)
        xb_ref[...] = xh
        logits = (_dot(xh, wrh_ref[...]) + _dot(xl, wrh_ref[...]) + _dot(xh, wrl_ref[...])) + br_ref[...]
        gl = jnp.where(lane < N_GROUPS, logits, NEG)
        gmax = jnp.max(gl, axis=-1, keepdims=True)
        g_gate = 1.0 / jnp.sum(jnp.exp(gl - gmax), axis=-1, keepdims=True)
        g_idx = jnp.min(jnp.where(gl == gmax, lanef, float(LANES)), axis=-1, keepdims=True)
        e_group = ((lane - ROUTE_LANE0) >> 2).astype(F32)
        sel = (lane >= ROUTE_LANE0) & (e_group == g_idx)
        el = jnp.where(sel, logits, NEG)
        top1 = jnp.max(el, axis=-1, keepdims=True)
        i1 = jnp.min(jnp.where(el == top1, lanef, float(LANES)), axis=-1, keepdims=True)
        el2 = jnp.where(lanef == i1, NEG, el)
        top2 = jnp.max(el2, axis=-1, keepdims=True)
        i2 = jnp.min(jnp.where(el2 == top2, lanef, float(LANES)), axis=-1, keepdims=True)
        r = jnp.exp(top2 - top1)
        w1 = 1.0 / (1.0 + r)
        w2 = r * w1
        gate_ref[...] = jnp.where(lanef == i1, w1, jnp.where(lanef == i2, w2, 0.0)) * g_gate
        acc_ref[...] = jnp.zeros_like(acc_ref)

    xb = xb_ref[...]
    gates = gate_ref[...]
    for k in range(MOE_EPB):
        e_lane = ROUTE_LANE0 + eb * MOE_EPB + k
        gcol = jnp.sum(jnp.where(lane == e_lane, gates, 0.0), axis=-1, keepdims=True)
        a = _dot(xb, wg_ref[k])
        u = _dot(xb, wu_ref[k])
        h = (a / (1.0 + jnp.exp(-a))) * u * gcol
        acc_ref[...] += _dot(h.astype(BF16), wd_ref[k])

    @pl.when(eb == pl.num_programs(1) - 1)
    def _finish():
        o_ref[...] = _layer_norm(DN_ALPHA * x_ref[...] + acc_ref[...], g_ref[...], b_ref[...])


def _moe(x2d, wrh, wrl, br, wg, wu, wd, g, bta, tm):
    t = x2d.shape[0]
    row = pl.BlockSpec((tm, D_MODEL), lambda i, e: (i, 0))
    wspec = lambda shape: pl.BlockSpec((MOE_EPB,) + shape, lambda i, e: (e, 0, 0))
    in_specs = [row, _const_spec(wrh.shape), _const_spec(wrl.shape), _const_spec(br.shape),
                wspec((D_MODEL, D_EXPERT)), wspec((D_MODEL, D_EXPERT)), wspec((D_EXPERT, D_MODEL)),
                _const_spec(g.shape), _const_spec(bta.shape)]
    return pl.pallas_call(
        _moe_kernel, out_shape=jax.ShapeDtypeStruct((t, D_MODEL), F32),
        grid=(t // tm, N_EXPERTS // MOE_EPB), in_specs=in_specs, out_specs=row,
        scratch_shapes=[pltpu.VMEM((tm, D_MODEL), BF16), pltpu.VMEM((tm, LANES), F32),
                        pltpu.VMEM((tm, D_MODEL), F32)],
        compiler_params=_params(("parallel", "arbitrary")), name="moe")(x2d, wrh, wrl, br, wg, wu, wd, g, bta)


def _pad_cols(w, n):
    return jnp.pad(w, ((0, 0), (0, n - w.shape[1])))


def kernel(x, mem, positions, w_in, b_forget, diff_lambda, diff_subln, w_branch_a, w_branch_b, w_branch_c,
           w_gate, b_gate, w_out, w_xq, w_xk, w_xv, w_xo, w_route_group, b_route_group, w_route_expert,
           b_route_expert, w_expert_gate, w_expert_up, w_expert_down, ln_g, ln_b):
    b, s, d = x.shape
    t = b * s
    depth = w_in.shape[0]
    scale = HEAD_DIM ** -0.5

    half = HEAD_DIM // 2
    inv = ROPE_THETA ** (-jnp.arange(half, dtype=F32) / half)
    ang = positions.astype(F32).reshape(t, 1) * inv
    cos, sin = jnp.cos(ang), jnp.sin(ang)
    cos_t = jnp.concatenate([cos, cos, cos, cos], axis=-1)
    sin_t = jnp.concatenate([-sin, sin, -sin, sin], axis=-1)

    mem2d = mem.reshape(b * MEM_LEN, d)
    x2d = x.reshape(t, d)
    row = lambda v: v.reshape(1, -1).astype(F32)
    for l in range(depth):
        lam_init = 0.8 - 0.6 * math.exp(-0.3 * l)
        offs = [0, A_W, 2 * A_W, 3 * A_W, 3 * A_W + B_W, 3 * A_W + 2 * B_W, 3 * A_W + 3 * B_W]
        offs.append(offs[-1] + B_HEADS)
        offs += [offs[-1] + C_W, offs[-1] + 2 * C_W, offs[-1] + 3 * C_W]
        wl = w_in[l]
        seg = [wl[:, offs[k]:offs[k + 1]] for k in range(10)]
        qa, ka, va, qb, kb, vb, fb, qc, kc, vc = seg
        ws = [(qa * scale), ka, va.T, (qb * scale), kb, vb.T, _pad_cols(fb, LANES), (qc * scale), kc, vc]
        ws = [w.astype(BF16) for w in ws]
        bfp = _pad_cols(row(b_forget[l]), LANES)

        outs = _inproj(x2d, cos_t, sin_t, bfp, ws, tm=512)
        o_qa, o_ka, o_va, o_qb, o_kb, o_vb, o_lf = outs[:7]
        o_qc, o_kc, o_vc = outs[7:10], outs[10:13], outs[13:16]

        lf = diff_lambda[l].astype(F32)
        lam = jnp.exp(jnp.dot(lf[0], lf[1])) - jnp.exp(jnp.dot(lf[2], lf[3])) + lam_init
        lam_row = jnp.full((1, LANES), lam, F32)
        gain = row(diff_subln[l]) * (1.0 - lam_init)
        r3 = lambda v: v.reshape(b, s, v.shape[-1])
        oa = _flash(r3(o_qa), r3(o_ka), o_va, (lam_row, gain), forget=False, bq=FLASH_BK)
        c = jnp.cumsum(o_lf.reshape(b, s, LANES)[:, :, :B_HEADS], axis=1)
        c_t = jnp.transpose(c.reshape(b, s, B_HEADS // 2, 2), (0, 2, 1, 3))
        ob = _flash(r3(o_qb), r3(o_kb), o_vb, (c_t,), forget=True, bq=FLASH_BK)
        ocs, lses = [], []
        for gi, (_, dil) in enumerate(C_PATTERNS):
            o_g, lse_g = _dilated(r3(o_qc[gi]), r3(o_kc[gi]), r3(o_vc[gi]), dil)
            ocs.append(o_g)
            lses.append(lse_g)
        x2d = _merge(x2d, oa.reshape(t, A_W), ob.reshape(t, B_W), ocs, lses,
                     w_gate[l].astype(BF16), row(b_gate[l]), w_branch_a[l].astype(BF16),
                     w_branch_b[l].astype(BF16), w_branch_c[l].astype(BF16), w_out[l].astype(BF16),
                     row(ln_g[l, 0]), row(ln_b[l, 0]), tm=256)
        xscale = X_HEAD_DIM ** -0.5
        mk, mv = _memkv(mem2d, w_xk[l].astype(BF16), w_xv[l].astype(BF16), tm=MEM_LEN)
        x2d = _xattn(x2d.reshape(b, s, d), mk.reshape(b, MEM_LEN, d), mv.reshape(b, MEM_LEN, d),
                     (w_xq[l] * xscale).astype(BF16), w_xo[l].astype(BF16),
                     row(ln_g[l, 1]), row(ln_b[l, 1]), tm=512).reshape(t, d)
        wr = jnp.concatenate([w_route_group[l]] + [w_route_expert[l, gi] for gi in range(N_GROUPS)], axis=1)
        wr = _pad_cols(wr.astype(F32), LANES)
        wrh = wr.astype(BF16)
        wrl = (wr - wrh.astype(F32)).astype(BF16)
        br = _pad_cols(row(jnp.concatenate([b_route_group[l], b_route_expert[l].reshape(-1)])), LANES)
        ne = (N_EXPERTS,)
        x2d = _moe(x2d, wrh, wrl, br,
                   w_expert_gate[l].reshape(ne + w_expert_gate.shape[3:]).astype(BF16),
                   w_expert_up[l].reshape(ne + w_expert_up.shape[3:]).astype(BF16),
                   w_expert_down[l].reshape(ne + w_expert_down.shape[3:]).astype(BF16),
                   row(ln_g[l, 2]), row(ln_b[l, 2]), tm=1024)
    return x2d.reshape(b, s, d)
```

```python
import functools
import math

import jax
import jax.numpy as jnp
from jax import lax
from jax.experimental import pallas as pl
from jax.experimental.pallas import tpu as pltpu

F32 = jnp.float32
BF16 = jnp.bfloat16

D_MODEL = 1024
HEAD_DIM = 64
ROPE_THETA = 10000.0
EPS = 1e-5
MEM_LEN = 256
A_HEADS = 4
B_HEADS = 8
C_PATTERNS = ((128, 1), (512, 4), (2048, 16))
C_GROUPS = 3
C_SLOTS = 4
C_BLOCK = 128
X_HEADS = 4
X_HEAD_DIM = D_MODEL // X_HEADS
N_GROUPS = 4
EXPERTS_PER_GROUP = 4
N_EXPERTS = N_GROUPS * EXPERTS_PER_GROUP
D_EXPERT = D_MODEL // 4
DEPTH = 2
DN_ALPHA = (2 * DEPTH) ** 0.25

LANES = 128
NEG = -1e30
LOG2E = math.log2(math.e)
VMEM_LIMIT = 56 * 1024 * 1024

A_W = A_HEADS * 2 * HEAD_DIM
B_W = B_HEADS * HEAD_DIM
C_GW = C_SLOTS * HEAD_DIM
C_W = C_GROUPS * C_GW
FLASH_BK = 512


def _params(sem):
    return pltpu.CompilerParams(dimension_semantics=sem, vmem_limit_bytes=VMEM_LIMIT)


def _const_spec(shape):
    nd = len(shape)
    return pl.BlockSpec(shape, lambda *_: (0,) * nd)


def _layer_norm(y, g, b):
    mu = jnp.mean(y, axis=-1, keepdims=True)
    yc = y - mu
    var = jnp.mean(yc * yc, axis=-1, keepdims=True)
    return yc * lax.rsqrt(var + EPS) * g + b


def _dot(a, b):
    return jnp.dot(a, b, preferred_element_type=F32)


def _dot_nt(a, b):
    return lax.dot_general(a, b, (((1,), (1,)), ((), ())), preferred_element_type=F32)


def _inproj_kernel(x_ref, cos_ref, sin_ref, bf_ref,
                   w_qa, w_ka, w_va, w_qb, w_kb, w_vb, w_fb, w_qc, w_kc, w_vc,
                   o_qa, o_ka, o_va, o_qb, o_kb, o_vb, o_lf,
                   o_qc0, o_qc1, o_qc2, o_kc0, o_kc1, o_kc2, o_vc0, o_vc1, o_vc2, perm_ref):
    xb = x_ref[...].astype(BF16)
    cos = cos_ref[...]
    sin = sin_ref[...]
    lane = lax.broadcasted_iota(jnp.int32, cos.shape, 1)
    lower = (lane & (HEAD_DIM - 1)) < HEAD_DIM // 2

    def rope(t):
        sw = jnp.where(lower, pltpu.roll(t, LANES - HEAD_DIM // 2, 1), pltpu.roll(t, HEAD_DIM // 2, 1))
        return t * cos + sw * sin

    def roped(w_ref, outs, width):
        y = _dot(xb, w_ref[...])
        per_out = width // LANES
        for c in range(y.shape[1] // LANES):
            o = outs[c // per_out]
            lo = (c % per_out) * LANES
            o[:, lo:lo + LANES] = rope(y[:, c * LANES:(c + 1) * LANES]).astype(BF16)

    def plain(w_ref, outs, width):
        y = _dot(xb, w_ref[...])
        for c, o in enumerate(outs):
            o[...] = y[:, c * width:(c + 1) * width].astype(BF16)

    def transposed(wt_ref, o):
        yt = _dot_nt(wt_ref[...], xb)
        for c in range(o.shape[0]):
            o[c] = yt[:, c * FLASH_BK:(c + 1) * FLASH_BK].astype(BF16)

    roped(w_qa, (o_qa,), A_W)
    roped(w_ka, (o_ka,), A_W)
    transposed(w_va, o_va)
    plain(w_qb, (o_qb,), B_W)
    plain(w_kb, (o_kb,), B_W)
    transposed(w_vb, o_vb)
    def dilated(w_ref, outs, slot, use_rope):
        y = _dot(xb, w_ref[...])
        tm = y.shape[0]
        for gi, (_, d) in enumerate(C_PATTERNS):
            for h in range(C_GW // LANES):
                c = gi * (C_GW // LANES) + h
                t = y[:, c * LANES:(c + 1) * LANES]
                t = rope(t) if use_rope else t
                if d == 1:
                    outs[gi][:, h * LANES:(h + 1) * LANES] = t.astype(BF16)
                    continue
                sl = slot + (gi - 1) * 2 + h
                perm_ref[sl] = t
                for r in range(d):
                    lo = r * C_GW + h * LANES
                    outs[gi][:, lo:lo + LANES] = perm_ref.at[sl][pl.ds(r, tm // d, stride=d), :].astype(BF16)

    dilated(w_qc, (o_qc0, o_qc1, o_qc2), 0, True)
    dilated(w_kc, (o_kc0, o_kc1, o_kc2), 4, True)
    dilated(w_vc, (o_vc0, o_vc1, o_vc2), 8, False)
    z =_dot(xb, w_fb[...]) + bf_ref[...]
    o_lf[...] = jnp.minimum(z, 0.0) - jnp.log(1.0 + jnp.exp(-jnp.abs(z)))


def _inproj(x2d, cos, sin, bfp, ws, tm):
    t = x2d.shape[0]
    row = lambda n: pl.BlockSpec((tm, n), lambda i: (i, 0))
    tshape = jax.ShapeDtypeStruct((t // FLASH_BK, A_W, FLASH_BK), BF16)
    tspec = pl.BlockSpec((tm // FLASH_BK, A_W, FLASH_BK), lambda i: (i, 0, 0))
    rshape = lambda w: jax.ShapeDtypeStruct((t, w), BF16)
    out_shape = [rshape(A_W), rshape(A_W), tshape, rshape(B_W), rshape(B_W), tshape]
    out_shape.append(jax.ShapeDtypeStruct((t, LANES), F32))
    out_specs = [row(A_W), row(A_W), tspec, row(B_W), row(B_W), tspec, row(LANES)]
    for _ in range(3):
        for _, d in C_PATTERNS:
            out_shape.append(jax.ShapeDtypeStruct((t // d, d * C_GW), BF16))
            out_specs.append(pl.BlockSpec((tm // d, d * C_GW), lambda i: (i, 0)))
    in_specs = [row(D_MODEL), row(LANES), row(LANES), _const_spec((1, LANES))]
    in_specs += [_const_spec(w.shape) for w in ws]
    n_perm = 3 * (C_GROUPS - 1) * (C_GW // LANES)
    return pl.pallas_call(
        _inproj_kernel, out_shape=out_shape, grid=(t // tm,), in_specs=in_specs, out_specs=out_specs,
        scratch_shapes=[pltpu.VMEM((n_perm, tm, LANES), F32)],
        compiler_params=_params(("parallel",)), name="inproj")(x2d, cos, sin, bfp, *ws)


def _flash_kernel(forget, bq, *refs):
    bk = FLASH_BK
    if forget:
        q_ref, k_ref, vt_ref, c_ref, o_ref, m_ref, l_ref, acc_ref, st_a, st_b, bias_ref = refs
    else:
        q_ref, k_ref, vt_ref, lam_ref, gain_ref, o_ref, m_ref, l_ref, acc_ref, st_a, st_b = refs
    dv = acc_ref.shape[1]
    i = pl.program_id(2)
    q = q_ref[...].astype(F32)
    lane = lax.broadcasted_iota(jnp.int32, (bq, LANES), 1)
    low = lane < HEAD_DIM
    qts = (jnp.where(low, q, 0.0).T.astype(BF16), jnp.where(low, 0.0, q).T.astype(BF16))
    key = lax.broadcasted_iota(jnp.int32, (bk, bq), 0)
    qry = lax.broadcasted_iota(jnp.int32, (bk, bq), 1)

    if forget:
        @pl.when(i == 0)
        def _bias():
            for s in range(2):
                bias_ref[s] = jnp.broadcast_to(-LOG2E * c_ref[:, s:s + 1], bias_ref.shape[1:])

    m_ref[...] = jnp.full(m_ref.shape, NEG, F32)
    l_ref[...] = jnp.zeros(l_ref.shape, F32)
    acc_ref[...] = jnp.zeros(acc_ref.shape, F32)

    def scores(j, st_ref):
        off = pl.multiple_of(j * bk, bk)
        kc = k_ref[pl.ds(off, bk), :]
        for s in range(2):
            st_ref[s] = _dot(kc, qts[s])

    def absorb(j, st_ref, diagonal):
        off = pl.multiple_of(j * bk, bk)
        for s in range(2):
            st = st_ref[s]
            if forget:
                bb = bias_ref[s, pl.ds(off, bk), :]
                st = st + jnp.concatenate([bb] * (bq // LANES), axis=1)
            if diagonal:
                st = jnp.where(key <= qry, st, NEG)
            m_prev = m_ref[s]
            m_new = jnp.maximum(m_prev, jnp.max(st, axis=0, keepdims=True))
            a = jnp.exp2(m_prev - m_new)
            p = jnp.exp2(st - m_new)
            l_ref[s] = a * l_ref[s] + jnp.sum(p, axis=0, keepdims=True)
            m_ref[s] = m_new
            vt = vt_ref[j, s * dv:(s + 1) * dv, :] if forget else vt_ref[j]
            acc_ref[s] = a * acc_ref[s] + _dot(vt, p.astype(BF16))

    scores(0, st_a)

    def pair(t, c):
        scores(2 * t + 1, st_b)
        absorb(2 * t, st_a, False)
        scores(2 * t + 2, st_a)
        absorb(2 * t + 1, st_b, False)
        return c

    lax.fori_loop(0, i // 2, pair, 0)

    @pl.when(i % 2 == 0)
    def _even_tail():
        absorb(i, st_a, True)

    @pl.when(i % 2 == 1)
    def _odd_tail():
        scores(i, st_b)
        absorb(i - 1, st_a, False)
        absorb(i, st_b, True)

    o0 = acc_ref[0] / l_ref[0]
    o1 = acc_ref[1] / l_ref[1]
    if forget:
        o_ref[...] = jnp.concatenate([o0, o1], axis=0).T.astype(o_ref.dtype)
    else:
        o = (o0 - lam_ref[:, 0:1] * o1).T
        ms = jnp.mean(o * o, axis=-1, keepdims=True)
        o_ref[...] = (o * lax.rsqrt(ms + EPS) * gain_ref[...]).astype(o_ref.dtype)


def _flash(q, k, vt, extra, forget, bq):
    b, s, w = q.shape
    g = w // LANES
    nk = s // FLASH_BK
    dv = HEAD_DIM if forget else LANES
    qspec = pl.BlockSpec((None, bq, LANES), lambda bi, gi, i: (bi, i, gi))
    kspec = pl.BlockSpec((None, s, LANES), lambda bi, gi, i: (bi, 0, gi))
    vspec = pl.BlockSpec((nk, LANES, FLASH_BK), lambda bi, gi, i: (bi, gi, 0))
    assert bq == FLASH_BK
    scratch = [pltpu.VMEM((2, 1, bq), F32), pltpu.VMEM((2, 1, bq), F32), pltpu.VMEM((2, dv, bq), F32),
               pltpu.VMEM((2, FLASH_BK, bq), F32), pltpu.VMEM((2, FLASH_BK, bq), F32)]
    if forget:
        especs = [pl.BlockSpec((None, None, s, 2), lambda bi, gi, i: (bi, gi, 0, 0))]
        scratch.append(pltpu.VMEM((2, s, LANES), F32))
    else:
        especs = [_const_spec((1, LANES)), _const_spec((1, LANES))]
    return pl.pallas_call(
        functools.partial(_flash_kernel, forget, bq),
        out_shape=jax.ShapeDtypeStruct((b, s, w), BF16),
        grid=(b, g, s // bq), in_specs=[qspec, kspec, vspec] + especs, out_specs=qspec,
        scratch_shapes=scratch,
        compiler_params=_params(("parallel", "parallel", "arbitrary")),
        name="fox_attn" if forget else "diff_attn")(q, k, vt, *extra)


def _dilated_kernel(nq, q_ref, kp_ref, kc_ref, vp_ref, vc_ref, o_ref, lse_ref):
    n = C_BLOCK
    j = pl.program_id(2)
    kall = jnp.concatenate([kp_ref[...], kc_ref[...]], axis=0)
    vall = jnp.concatenate([vp_ref[...], vc_ref[...]], axis=0)
    qi = lax.broadcasted_iota(jnp.int32, (n, 2 * n), 0)
    kj = lax.broadcasted_iota(jnp.int32, (n, 2 * n), 1)
    dist = n + qi - kj
    window = (dist >= 0) & (dist <= n)
    first_key = jnp.where(j > 0, 0, n)
    lane = lax.broadcasted_iota(jnp.int32, (n, LANES), 1)
    low = lane < HEAD_DIM
    units = [(t, p, s) for t in range(nq) for p in range(C_GW // LANES) for s in range(2)]
    scores = {}
    for t, p, s in units:
        sl = slice(p * LANES, (p + 1) * LANES)
        q128 = q_ref[t * n:(t + 1) * n, sl]
        zero = jnp.zeros_like(q128)
        qm = jnp.where(low, q128, zero) if s == 0 else jnp.where(low, zero, q128)
        mask = (window & (kj >= first_key)) if t == 0 else window
        scores[t, p, s] = jnp.where(mask, _dot_nt(qm, kall[t * n:(t + 2) * n, sl]), NEG)
    probs = {}
    for u in units:
        m = jnp.max(scores[u], axis=-1, keepdims=True)
        e = jnp.exp(scores[u] - m)
        l = jnp.sum(e, axis=-1, keepdims=True)
        probs[u] = (e.astype(BF16), l, m + jnp.log(l))
    for t in range(nq):
        for p in range(C_GW // LANES):
            sl = slice(p * LANES, (p + 1) * LANES)
            os = [_dot(probs[t, p, s][0], vall[t * n:(t + 2) * n, sl]) / probs[t, p, s][1] for s in range(2)]
            o_ref[t * n:(t + 1) * n, sl] = jnp.where(low, os[0], os[1])
            lse_ref[t * n:(t + 1) * n, sl] = jnp.where(low, probs[t, p, 0][2], probs[t, p, 1][2])


def _dilated(q, k, v, b, dilation, nq):
    w = C_GW
    n = C_BLOCK
    m_len = q.shape[0] // b
    nb = m_len // (n * nq)
    view = lambda t: t.reshape(b, m_len, dilation * w)
    cur = pl.BlockSpec((None, n * nq, w), lambda bi, r, j: (bi, j, r))
    prev = pl.BlockSpec((None, n, w), lambda bi, r, j: (bi, jnp.maximum(j * nq - 1, 0), r))
    o, lse = pl.pallas_call(
        functools.partial(_dilated_kernel, nq),
        out_shape=[jax.ShapeDtypeStruct((b, m_len, dilation * w), F32)] * 2,
        grid=(b, dilation, nb), in_specs=[cur, prev, cur, prev, cur], out_specs=[cur, cur],
        compiler_params=_params(("parallel", "parallel", "arbitrary")),
        name=f"dilated_d{dilation}")(view(q), view(k), view(k), view(v), view(v))
    return o.reshape(b * m_len, dilation * w), lse.reshape(b * m_len, dilation * w)


def _merge_kernel(x_ref, oa_ref, ob_ref, oc0, oc1, oc2, ls0, ls1, ls2,
                  wg_ref, bg_ref, wa_ref, wb_ref, wc_ref, wo_ref, g_ref, b_ref, o_ref, perm_ref):
    x = x_ref[...]
    xb = x.astype(BF16)
    tm = x.shape[0]

    def token_major(blk_ref, gi, slot):
        d = C_PATTERNS[gi][1]
        if d == 1:
            return blk_ref[...]
        for h in range(C_GW // LANES):
            for r in range(d):
                lo = r * C_GW + h * LANES
                perm_ref.at[slot + h][pl.ds(r, tm // d, stride=d), :] = blk_ref[:, lo:lo + LANES]
        return jnp.concatenate([perm_ref[slot + h] for h in range(C_GW // LANES)], axis=1)

    o0, o1, o2 = (token_major(r, gi, 2 * gi) for gi, r in enumerate((oc0, oc1, oc2)))
    l0, l1, l2 = (token_major(r, gi, 2 * C_GROUPS + 2 * gi) for gi, r in enumerate((ls0, ls1, ls2)))
    m = jnp.maximum(jnp.maximum(l0, l1), l2)
    e0, e1, e2 = jnp.exp(l0 - m), jnp.exp(l1 - m), jnp.exp(l2 - m)
    oc = (e0 * o0 + e1 * o1 + e2 * o2) / (e0 + e1 + e2)
    branches = ((oa_ref[...], wa_ref), (ob_ref[...], wb_ref), (oc.astype(BF16), wc_ref))
    merged = None
    for k, (ob, w_ref) in enumerate(branches):
        sl = slice(k * D_MODEL, (k + 1) * D_MODEL)
        z = _dot(xb, wg_ref[:, sl]) + bg_ref[:, sl]
        gate = 1.0 / (1.0 + jnp.exp(-z))
        term = gate * _dot(ob, w_ref[...])
        merged = term if merged is None else merged + term
    h = _dot(merged.astype(BF16), wo_ref[...])
    o_ref[...] = _layer_norm(DN_ALPHA * x + h, g_ref[...], b_ref[...])


def _merge(x2d, oa, ob, ocs, lses, wg, bg, wa, wb, wc, wo, g, bta, tm):
    t = x2d.shape[0]
    row = lambda n: pl.BlockSpec((tm, n), lambda i: (i, 0))
    consts = (wg, bg, wa, wb, wc, wo, g, bta)
    cspecs = [pl.BlockSpec((tm // d, d * C_GW), lambda i: (i, 0)) for _, d in C_PATTERNS]
    in_specs = [row(D_MODEL), row(A_W), row(B_W)] + cspecs * 2 + [_const_spec(c.shape) for c in consts]
    return pl.pallas_call(
        _merge_kernel, out_shape=jax.ShapeDtypeStruct((t, D_MODEL), F32), grid=(t // tm,),
        in_specs=in_specs, out_specs=row(D_MODEL),
        scratch_shapes=[pltpu.VMEM((4 * C_GROUPS, tm, LANES), F32)],
        compiler_params=_params(("parallel",)), name="merge")(x2d, oa, ob, *ocs, *lses, *consts)


def _memkv_kernel(m_ref, wk_ref, wv_ref, k_ref, v_ref):
    mb = m_ref[...].astype(BF16)
    k_ref[...] = _dot(mb, wk_ref[...]).astype(BF16)
    v_ref[...] = _dot(mb, wv_ref[...]).astype(BF16)


def _memkv(mem2d, wk, wv, tm):
    t = mem2d.shape[0]
    row = pl.BlockSpec((tm, D_MODEL), lambda i: (i, 0))
    return pl.pallas_call(
        _memkv_kernel, out_shape=[jax.ShapeDtypeStruct((t, D_MODEL), BF16)] * 2, grid=(t // tm,),
        in_specs=[row, _const_spec(wk.shape), _const_spec(wv.shape)], out_specs=[row, row],
        compiler_params=_params(("parallel",)), name="memkv")(mem2d, wk, wv)


def _xattn_kernel(x_ref, k_ref, v_ref, wq_ref, wo_ref, g_ref, b_ref, o_ref):
    x = x_ref[...]
    q = _dot(x.astype(BF16), wq_ref[...]).astype(BF16)
    heads = []
    for h in range(X_HEADS):
        sl = slice(h * X_HEAD_DIM, (h + 1) * X_HEAD_DIM)
        st = _dot_nt(q[:, sl], k_ref[:, sl])
        m = jnp.max(st, axis=-1, keepdims=True)
        e = jnp.exp(st - m)
        l = jnp.sum(e, axis=-1, keepdims=True)
        heads.append((_dot(e.astype(BF16), v_ref[:, sl]) / l).astype(BF16))
    o = jnp.concatenate(heads, axis=-1)
    h_out = _dot(o, wo_ref[...])
    o_ref[...] = _layer_norm(DN_ALPHA * x + h_out, g_ref[...], b_ref[...])


def _xattn(x3d, k3d, v3d, wq, wo, g, bta, tm):
    b, s, d = x3d.shape
    xspec = pl.BlockSpec((None, tm, d), lambda bi, i: (bi, i, 0))
    mspec = pl.BlockSpec((None, MEM_LEN, d), lambda bi, i: (bi, 0, 0))
    consts = (wq, wo, g, bta)
    return pl.pallas_call(
        _xattn_kernel, out_shape=jax.ShapeDtypeStruct((b, s, d), F32), grid=(b, s // tm),
        in_specs=[xspec, mspec, mspec] + [_const_spec(c.shape) for c in consts], out_specs=xspec,
        compiler_params=_params(("parallel", "parallel")), name="xattn")(x3d, k3d, v3d, *consts)


MOE_EPB = 2
ROUTE_LANE0 = N_GROUPS


def _moe_kernel(x_ref, wrh_ref, wrl_ref, br_ref, wg_ref, wu_ref, wd_ref, g_ref, b_ref, o_ref,
                xb_ref, gate_ref, acc_ref):
    eb = pl.program_id(1)
    tm = x_ref.shape[0]
    lane = lax.broadcasted_iota(jnp.int32, (tm, LANES), 1)

    @pl.when(eb == 0)
    def _route():
        x = x_ref[...]
        lanef = lane.astype(F32)
        xh = x.astype(BF16)
        xl = (x - xh.astype(F32)).astype(BF16)
        xb_ref[...] = xh
        logits = (_dot(xh, wrh_ref[...]) + _dot(xl, wrh_ref[...]) + _dot(xh, wrl_ref[...])) + br_ref[...]
        gl = jnp.where(lane < N_GROUPS, logits, NEG)
        gmax = jnp.max(gl, axis=-1, keepdims=True)
        g_gate = 1.0 / jnp.sum(jnp.exp(gl - gmax), axis=-1, keepdims=True)
        g_idx = jnp.min(jnp.where(gl == gmax, lanef, float(LANES)), axis=-1, keepdims=True)
        e_group = ((lane - ROUTE_LANE0) >> 2).astype(F32)
        sel = (lane >= ROUTE_LANE0) & (e_group == g_idx)
        el = jnp.where(sel, logits, NEG)
        top1 = jnp.max(el, axis=-1, keepdims=True)
        i1 = jnp.min(jnp.where(el == top1, lanef, float(LANES)), axis=-1, keepdims=True)
        el2 = jnp.where(lanef == i1, NEG, el)
        top2 = jnp.max(el2, axis=-1, keepdims=True)
        i2 = jnp.min(jnp.where(el2 == top2, lanef, float(LANES)), axis=-1, keepdims=True)
        r = jnp.exp(top2 - top1)
        w1 = 1.0 / (1.0 + r)
        w2 = r * w1
        gate_ref[...] = jnp.where(lanef == i1, w1, jnp.where(lanef == i2, w2, 0.0)) * g_gate
        acc_ref[...] = jnp.zeros_like(acc_ref)

    xb = xb_ref[...]
    gates = gate_ref[...]
    for k in range(MOE_EPB):
        e_lane = ROUTE_LANE0 + eb * MOE_EPB + k
        gcol = jnp.sum(jnp.where(lane == e_lane, gates, 0.0), axis=-1, keepdims=True)
        a = _dot(xb, wg_ref[k])
        u = _dot(xb, wu_ref[k])
        h = (a / (1.0 + jnp.exp(-a))) * u * gcol
        acc_ref[...] += _dot(h.astype(BF16), wd_ref[k])

    @pl.when(eb == pl.num_programs(1) - 1)
    def _finish():
        o_ref[...] = _layer_norm(DN_ALPHA * x_ref[...] + acc_ref[...], g_ref[...], b_ref[...])


def _moe(x2d, wrh, wrl, br, wg, wu, wd, g, bta, tm):
    t = x2d.shape[0]
    row = pl.BlockSpec((tm, D_MODEL), lambda i, e: (i, 0))
    wspec = lambda shape: pl.BlockSpec((MOE_EPB,) + shape, lambda i, e: (e, 0, 0))
    in_specs = [row, _const_spec(wrh.shape), _const_spec(wrl.shape), _const_spec(br.shape),
                wspec((D_MODEL, D_EXPERT)), wspec((D_MODEL, D_EXPERT)), wspec((D_EXPERT, D_MODEL)),
                _const_spec(g.shape), _const_spec(bta.shape)]
    return pl.pallas_call(
        _moe_kernel, out_shape=jax.ShapeDtypeStruct((t, D_MODEL), F32),
        grid=(t // tm, N_EXPERTS // MOE_EPB), in_specs=in_specs, out_specs=row,
        scratch_shapes=[pltpu.VMEM((tm, D_MODEL), BF16), pltpu.VMEM((tm, LANES), F32),
                        pltpu.VMEM((tm, D_MODEL), F32)],
        compiler_params=_params(("parallel", "arbitrary")), name="moe")(x2d, wrh, wrl, br, wg, wu, wd, g, bta)


def _pad_cols(w, n):
    return jnp.pad(w, ((0, 0), (0, n - w.shape[1])))


def kernel(x, mem, positions, w_in, b_forget, diff_lambda, diff_subln, w_branch_a, w_branch_b, w_branch_c,
           w_gate, b_gate, w_out, w_xq, w_xk, w_xv, w_xo, w_route_group, b_route_group, w_route_expert,
           b_route_expert, w_expert_gate, w_expert_up, w_expert_down, ln_g, ln_b):
    b, s, d = x.shape
    t = b * s
    depth = w_in.shape[0]
    scale = HEAD_DIM ** -0.5

    half = HEAD_DIM // 2
    inv = ROPE_THETA ** (-jnp.arange(half, dtype=F32) / half)
    ang = positions.astype(F32).reshape(t, 1) * inv
    cos, sin = jnp.cos(ang), jnp.sin(ang)
    cos_t = jnp.concatenate([cos, cos, cos, cos], axis=-1)
    sin_t = jnp.concatenate([-sin, sin, -sin, sin], axis=-1)

    mem2d = mem.reshape(b * MEM_LEN, d)
    x2d = x.reshape(t, d)
    row = lambda v: v.reshape(1, -1).astype(F32)
    for l in range(depth):
        lam_init = 0.8 - 0.6 * math.exp(-0.3 * l)
        offs = [0, A_W, 2 * A_W, 3 * A_W, 3 * A_W + B_W, 3 * A_W + 2 * B_W, 3 * A_W + 3 * B_W]
        offs.append(offs[-1] + B_HEADS)
        offs += [offs[-1] + C_W, offs[-1] + 2 * C_W, offs[-1] + 3 * C_W]
        wl = w_in[l]
        seg = [wl[:, offs[k]:offs[k + 1]] for k in range(10)]
        qa, ka, va, qb, kb, vb, fb, qc, kc, vc = seg
        ws = [(qa * (scale * LOG2E)), ka, va.T, (qb * (scale * LOG2E)), kb, vb.T, _pad_cols(fb, LANES),
              (qc * scale), kc, vc]
        ws = [w.astype(BF16) for w in ws]
        bfp = _pad_cols(row(b_forget[l]), LANES)

        outs = _inproj(x2d, cos_t, sin_t, bfp, ws, tm=512)
        o_qa, o_ka, o_va, o_qb, o_kb, o_vb, o_lf = outs[:7]
        o_qc, o_kc, o_vc = outs[7:10], outs[10:13], outs[13:16]

        lf = diff_lambda[l].astype(F32)
        lam = jnp.exp(jnp.dot(lf[0], lf[1])) - jnp.exp(jnp.dot(lf[2], lf[3])) + lam_init
        lam_row = jnp.full((1, LANES), lam, F32)
        gain = row(diff_subln[l]) * (1.0 - lam_init)
        r3 = lambda v: v.reshape(b, s, v.shape[-1])
        oa = _flash(r3(o_qa), r3(o_ka), o_va, (lam_row, gain), forget=False, bq=FLASH_BK)
        c = jnp.cumsum(o_lf.reshape(b, s, LANES)[:, :, :B_HEADS], axis=1)
        c_t = jnp.transpose(c.reshape(b, s, B_HEADS // 2, 2), (0, 2, 1, 3))
        ob = _flash(r3(o_qb), r3(o_kb), o_vb, (c_t,), forget=True, bq=FLASH_BK)
        ocs, lses = [], []
        for gi, (_, dil) in enumerate(C_PATTERNS):
            o_g, lse_g = _dilated(o_qc[gi], o_kc[gi], o_vc[gi], b, dil, nq=min(4, s // (dil * C_BLOCK)))
            ocs.append(o_g)
            lses.append(lse_g)
        x2d = _merge(x2d, oa.reshape(t, A_W), ob.reshape(t, B_W), ocs, lses,
                     w_gate[l].astype(BF16), row(b_gate[l]), w_branch_a[l].astype(BF16),
                     w_branch_b[l].astype(BF16), w_branch_c[l].astype(BF16), w_out[l].astype(BF16),
                     row(ln_g[l, 0]), row(ln_b[l, 0]), tm=256)
        xscale = X_HEAD_DIM ** -0.5
        mk, mv = _memkv(mem2d, w_xk[l].astype(BF16), w_xv[l].astype(BF16), tm=MEM_LEN)
        x2d = _xattn(x2d.reshape(b, s, d), mk.reshape(b, MEM_LEN, d), mv.reshape(b, MEM_LEN, d),
                     (w_xq[l] * xscale).astype(BF16), w_xo[l].astype(BF16),
                     row(ln_g[l, 1]), row(ln_b[l, 1]), tm=512).reshape(t, d)
        wr = jnp.concatenate([w_route_group[l]] + [w_route_expert[l, gi] for gi in range(N_GROUPS)], axis=1)
        wr = _pad_cols(wr.astype(F32), LANES)
        wrh = wr.astype(BF16)
        wrl = (wr - wrh.astype(F32)).astype(BF16)
        br = _pad_cols(row(jnp.concatenate([b_route_group[l], b_route_expert[l].reshape(-1)])), LANES)
        ne = (N_EXPERTS,)
        x2d = _moe(x2d, wrh, wrl, br,
                   w_expert_gate[l].reshape(ne + w_expert_gate.shape[3:]).astype(BF16),
                   w_expert_up[l].reshape(ne + w_expert_up.shape[3:]).astype(BF16),
                   w_expert_down[l].reshape(ne + w_expert_down.shape[3:]).astype(BF16),
                   row(ln_g[l, 2]), row(ln_b[l, 2]), tm=1024)
    return x2d.reshape(b, s, d)
```

```python
import functools
import math

import jax
import jax.numpy as jnp
from jax import lax
from jax.experimental import pallas as pl
from jax.experimental.pallas import tpu as pltpu

F32 = jnp.float32
BF16 = jnp.bfloat16

D_MODEL = 1024
HEAD_DIM = 64
ROPE_THETA = 10000.0
EPS = 1e-5
MEM_LEN = 256
A_HEADS = 4
B_HEADS = 8
C_PATTERNS = ((128, 1), (512, 4), (2048, 16))
C_GROUPS = 3
C_SLOTS = 4
C_BLOCK = 128
X_HEADS = 4
X_HEAD_DIM = D_MODEL // X_HEADS
N_GROUPS = 4
EXPERTS_PER_GROUP = 4
N_EXPERTS = N_GROUPS * EXPERTS_PER_GROUP
D_EXPERT = D_MODEL // 4
DEPTH = 2
DN_ALPHA = (2 * DEPTH) ** 0.25

LANES = 128
NEG = -1e30
LOG2E = math.log2(math.e)
VMEM_LIMIT = 56 * 1024 * 1024

A_W = A_HEADS * 2 * HEAD_DIM
B_W = B_HEADS * HEAD_DIM
C_GW = C_SLOTS * HEAD_DIM
C_W = C_GROUPS * C_GW
FLASH_BK = 512


def _params(sem):
    return pltpu.CompilerParams(dimension_semantics=sem, vmem_limit_bytes=VMEM_LIMIT)


def _const_spec(shape):
    nd = len(shape)
    return pl.BlockSpec(shape, lambda *_: (0,) * nd)


def _layer_norm(y, g, b):
    mu = jnp.mean(y, axis=-1, keepdims=True)
    yc = y - mu
    var = jnp.mean(yc * yc, axis=-1, keepdims=True)
    return yc * lax.rsqrt(var + EPS) * g + b


def _dot(a, b):
    return jnp.dot(a, b, preferred_element_type=F32)


def _dot_nt(a, b):
    return lax.dot_general(a, b, (((1,), (1,)), ((), ())), preferred_element_type=F32)


def _inproj_kernel(x_ref, cos_ref, sin_ref, bf_ref,
                   w_qa, w_ka, w_va, w_qb, w_kb, w_vb, w_fb, w_qc, w_kc, w_vc,
                   o_qa, o_ka, o_va, o_qb, o_kb, o_vb, o_lf,
                   o_qc0, o_qc1, o_qc2, o_kc0, o_kc1, o_kc2, o_vc0, o_vc1, o_vc2, perm_ref):
    xb = x_ref[...].astype(BF16)
    cos = cos_ref[...]
    sin = sin_ref[...]
    lane = lax.broadcasted_iota(jnp.int32, cos.shape, 1)
    lower = (lane & (HEAD_DIM - 1)) < HEAD_DIM // 2

    def rope(t):
        sw = jnp.where(lower, pltpu.roll(t, LANES - HEAD_DIM // 2, 1), pltpu.roll(t, HEAD_DIM // 2, 1))
        return t * cos + sw * sin

    def roped(w_ref, outs, width):
        y = _dot(xb, w_ref[...])
        per_out = width // LANES
        for c in range(y.shape[1] // LANES):
            o = outs[c // per_out]
            lo = (c % per_out) * LANES
            o[:, lo:lo + LANES] = rope(y[:, c * LANES:(c + 1) * LANES]).astype(BF16)

    def plain(w_ref, outs, width):
        y = _dot(xb, w_ref[...])
        for c, o in enumerate(outs):
            o[...] = y[:, c * width:(c + 1) * width].astype(BF16)

    def transposed(wt_ref, o):
        yt = _dot_nt(wt_ref[...], xb)
        for c in range(o.shape[0]):
            o[c] = yt[:, c * FLASH_BK:(c + 1) * FLASH_BK].astype(BF16)

    roped(w_qa, (o_qa,), A_W)
    roped(w_ka, (o_ka,), A_W)
    transposed(w_va, o_va)
    plain(w_qb, (o_qb,), B_W)
    plain(w_kb, (o_kb,), B_W)
    transposed(w_vb, o_vb)
    def dilated(w_ref, outs, slot, use_rope):
        y = _dot(xb, w_ref[...])
        tm = y.shape[0]
        for gi, (_, d) in enumerate(C_PATTERNS):
            for h in range(C_GW // LANES):
                c = gi * (C_GW // LANES) + h
                t = y[:, c * LANES:(c + 1) * LANES]
                t = rope(t) if use_rope else t
                if d == 1:
                    outs[gi][:, h * LANES:(h + 1) * LANES] = t.astype(BF16)
                    continue
                sl = slot + (gi - 1) * 2 + h
                perm_ref[sl] = t
                for r in range(d):
                    lo = r * C_GW + h * LANES
                    outs[gi][:, lo:lo + LANES] = perm_ref.at[sl][pl.ds(r, tm // d, stride=d), :].astype(BF16)

    dilated(w_qc, (o_qc0, o_qc1, o_qc2), 0, True)
    dilated(w_kc, (o_kc0, o_kc1, o_kc2), 4, True)
    dilated(w_vc, (o_vc0, o_vc1, o_vc2), 8, False)
    z =_dot(xb, w_fb[...]) + bf_ref[...]
    o_lf[...] = jnp.minimum(z, 0.0) - jnp.log(1.0 + jnp.exp(-jnp.abs(z)))


def _inproj(x2d, cos, sin, bfp, ws, tm):
    t = x2d.shape[0]
    row = lambda n: pl.BlockSpec((tm, n), lambda i: (i, 0))
    tshape = jax.ShapeDtypeStruct((t // FLASH_BK, A_W, FLASH_BK), BF16)
    tspec = pl.BlockSpec((tm // FLASH_BK, A_W, FLASH_BK), lambda i: (i, 0, 0))
    rshape = lambda w: jax.ShapeDtypeStruct((t, w), BF16)
    out_shape = [rshape(A_W), rshape(A_W), tshape, rshape(B_W), rshape(B_W), tshape]
    out_shape.append(jax.ShapeDtypeStruct((t, LANES), F32))
    out_specs = [row(A_W), row(A_W), tspec, row(B_W), row(B_W), tspec, row(LANES)]
    for _ in range(3):
        for _, d in C_PATTERNS:
            out_shape.append(jax.ShapeDtypeStruct((t // d, d * C_GW), BF16))
            out_specs.append(pl.BlockSpec((tm // d, d * C_GW), lambda i: (i, 0)))
    in_specs = [row(D_MODEL), row(LANES), row(LANES), _const_spec((1, LANES))]
    in_specs += [_const_spec(w.shape) for w in ws]
    n_perm = 3 * (C_GROUPS - 1) * (C_GW // LANES)
    return pl.pallas_call(
        _inproj_kernel, out_shape=out_shape, grid=(t // tm,), in_specs=in_specs, out_specs=out_specs,
        scratch_shapes=[pltpu.VMEM((n_perm, tm, LANES), F32)],
        compiler_params=_params(("parallel",)), name="inproj")(x2d, cos, sin, bfp, *ws)


def _flash_kernel(forget, bq, *refs):
    bk = FLASH_BK
    if forget:
        q_ref, k_ref, vt_ref, c_ref, o_ref, m_ref, l_ref, acc_ref, st_a, st_b, bias_ref = refs
    else:
        q_ref, k_ref, vt_ref, lam_ref, gain_ref, o_ref, m_ref, l_ref, acc_ref, st_a, st_b = refs
    dv = acc_ref.shape[2]
    i = pl.program_id(2)
    q = q_ref[...].astype(F32)
    lane = lax.broadcasted_iota(jnp.int32, (bq, LANES), 1)
    low = lane < HEAD_DIM
    qts = (jnp.where(low, q, 0.0).T.astype(BF16), jnp.where(low, 0.0, q).T.astype(BF16))
    key = lax.broadcasted_iota(jnp.int32, (bk, bk), 0)
    qry = lax.broadcasted_iota(jnp.int32, (bk, bk), 1)
    halves = bq // bk

    if forget:
        @pl.when(i == 0)
        def _bias():
            for s in range(2):
                bias_ref[s] = jnp.broadcast_to(-LOG2E * c_ref[:, s:s + 1], bias_ref.shape[1:])

    m_ref[...] = jnp.full(m_ref.shape, NEG, F32)
    l_ref[...] = jnp.zeros(l_ref.shape, F32)
    acc_ref[...] = jnp.zeros(acc_ref.shape, F32)

    def scores(j, st_ref, first_half=0):
        off = pl.multiple_of(j * bk, bk)
        kc = k_ref[pl.ds(off, bk), :]
        for s in range(2):
            for h in range(first_half, halves):
                st_ref[s, h] = _dot(kc, qts[s][:, h * bk:(h + 1) * bk])

    def absorb(j, st_ref, first_half=0, diagonal=False):
        off = pl.multiple_of(j * bk, bk)
        for s in range(2):
            for h in range(first_half, halves):
                st = st_ref[s, h]
                if forget:
                    bb = bias_ref[s, pl.ds(off, bk), :]
                    st = st + jnp.concatenate([bb] * (bk // LANES), axis=1)
                if diagonal and h == first_half:
                    st = jnp.where(key <= qry, st, NEG)
                m_prev = m_ref[s, h]
                m_new = jnp.maximum(m_prev, jnp.max(st, axis=0, keepdims=True))
                a = jnp.exp2(m_prev - m_new)
                p = jnp.exp2(st - m_new)
                l_ref[s, h] = a * l_ref[s, h] + jnp.sum(p, axis=0, keepdims=True)
                m_ref[s, h] = m_new
                vt = vt_ref[j, s * dv:(s + 1) * dv, :] if forget else vt_ref[j]
                acc_ref[s, h] = a * acc_ref[s, h] + _dot(vt, p.astype(BF16))

    scores(0, st_a)

    def pair(t, c):
        scores(2 * t + 1, st_b)
        absorb(2 * t, st_a)
        scores(2 * t + 2, st_a)
        absorb(2 * t + 1, st_b)
        return c

    lax.fori_loop(0, i, pair, 0)
    scores(2 * i + 1, st_b, 1)
    absorb(2 * i, st_a, 0, True)
    absorb(2 * i + 1, st_b, 1, True)

    for h in range(halves):
        o0 = acc_ref[0, h] / l_ref[0, h]
        o1 = acc_ref[1, h] / l_ref[1, h]
        rows = slice(h * bk, (h + 1) * bk)
        if forget:
            o_ref[rows, :] = jnp.concatenate([o0, o1], axis=0).T.astype(o_ref.dtype)
        else:
            o = (o0 - lam_ref[:, 0:1] * o1).T
            ms = jnp.mean(o * o, axis=-1, keepdims=True)
            o_ref[rows, :] = (o * lax.rsqrt(ms + EPS) * gain_ref[...]).astype(o_ref.dtype)


def _flash(q, k, vt, extra, forget, bq):
    b, s, w = q.shape
    g = w // LANES
    nk = s // FLASH_BK
    dv = HEAD_DIM if forget else LANES
    qspec = pl.BlockSpec((None, bq, LANES), lambda bi, gi, i: (bi, i, gi))
    kspec = pl.BlockSpec((None, s, LANES), lambda bi, gi, i: (bi, 0, gi))
    vspec = pl.BlockSpec((nk, LANES, FLASH_BK), lambda bi, gi, i: (bi, gi, 0))
    assert bq == 2 * FLASH_BK
    bk = FLASH_BK
    scratch = [pltpu.VMEM((2, 2, 1, bk), F32), pltpu.VMEM((2, 2, 1, bk), F32), pltpu.VMEM((2, 2, dv, bk), F32),
               pltpu.VMEM((2, 2, bk, bk), F32), pltpu.VMEM((2, 2, bk, bk), F32)]
    if forget:
        especs = [pl.BlockSpec((None, None, s, 2), lambda bi, gi, i: (bi, gi, 0, 0))]
        scratch.append(pltpu.VMEM((2, s, LANES), F32))
    else:
        especs = [_const_spec((1, LANES)), _const_spec((1, LANES))]
    return pl.pallas_call(
        functools.partial(_flash_kernel, forget, bq),
        out_shape=jax.ShapeDtypeStruct((b, s, w), BF16),
        grid=(b, g, s // bq), in_specs=[qspec, kspec, vspec] + especs, out_specs=qspec,
        scratch_shapes=scratch,
        compiler_params=_params(("parallel", "parallel", "arbitrary")),
        name="fox_attn" if forget else "diff_attn")(q, k, vt, *extra)


def _dilated_kernel(nq, q_ref, kp_ref, kc_ref, vp_ref, vc_ref, o_ref, lse_ref):
    n = C_BLOCK
    j = pl.program_id(2)
    kall = jnp.concatenate([kp_ref[...], kc_ref[...]], axis=0)
    vall = jnp.concatenate([vp_ref[...], vc_ref[...]], axis=0)
    qi = lax.broadcasted_iota(jnp.int32, (n, 2 * n), 0)
    kj = lax.broadcasted_iota(jnp.int32, (n, 2 * n), 1)
    dist = n + qi - kj
    window = (dist >= 0) & (dist <= n)
    first_key = jnp.where(j > 0, 0, n)
    lane = lax.broadcasted_iota(jnp.int32, (n, LANES), 1)
    low = lane < HEAD_DIM
    units = [(t, p, s) for t in range(nq) for p in range(C_GW // LANES) for s in range(2)]
    scores = {}
    for t, p, s in units:
        sl = slice(p * LANES, (p + 1) * LANES)
        q128 = q_ref[t * n:(t + 1) * n, sl]
        zero = jnp.zeros_like(q128)
        qm = jnp.where(low, q128, zero) if s == 0 else jnp.where(low, zero, q128)
        mask = (window & (kj >= first_key)) if t == 0 else window
        scores[t, p, s] = jnp.where(mask, _dot_nt(qm, kall[t * n:(t + 2) * n, sl]), NEG)
    probs = {}
    for u in units:
        m = jnp.max(scores[u], axis=-1, keepdims=True)
        e = jnp.exp(scores[u] - m)
        l = jnp.sum(e, axis=-1, keepdims=True)
        probs[u] = (e.astype(BF16), l, m + jnp.log(l))
    for t in range(nq):
        for p in range(C_GW // LANES):
            sl = slice(p * LANES, (p + 1) * LANES)
            os = [_dot(probs[t, p, s][0], vall[t * n:(t + 2) * n, sl]) / probs[t, p, s][1] for s in range(2)]
            o_ref[t * n:(t + 1) * n, sl] = jnp.where(low, os[0], os[1])
            lse_ref[t * n:(t + 1) * n, sl] = jnp.where(low, probs[t, p, 0][2], probs[t, p, 1][2])


def _dilated(q, k, v, b, dilation, nq):
    w = C_GW
    n = C_BLOCK
    m_len = q.shape[0] // b
    nb = m_len // (n * nq)
    view = lambda t: t.reshape(b, m_len, dilation * w)
    cur = pl.BlockSpec((None, n * nq, w), lambda bi, r, j: (bi, j, r))
    prev = pl.BlockSpec((None, n, w), lambda bi, r, j: (bi, jnp.maximum(j * nq - 1, 0), r))
    o, lse = pl.pallas_call(
        functools.partial(_dilated_kernel, nq),
        out_shape=[jax.ShapeDtypeStruct((b, m_len, dilation * w), F32)] * 2,
        grid=(b, dilation, nb), in_specs=[cur, prev, cur, prev, cur], out_specs=[cur, cur],
        compiler_params=_params(("parallel", "parallel", "arbitrary")),
        name=f"dilated_d{dilation}")(view(q), view(k), view(k), view(v), view(v))
    return o.reshape(b * m_len, dilation * w), lse.reshape(b * m_len, dilation * w)


def _merge_kernel(x_ref, oa_ref, ob_ref, oc0, oc1, oc2, ls0, ls1, ls2,
                  wg_ref, bg_ref, wa_ref, wb_ref, wc_ref, wo_ref, g_ref, b_ref, o_ref, perm_ref):
    x = x_ref[...]
    xb = x.astype(BF16)
    tm = x.shape[0]

    def token_major(blk_ref, gi, slot):
        d = C_PATTERNS[gi][1]
        if d == 1:
            return blk_ref[...]
        for h in range(C_GW // LANES):
            for r in range(d):
                lo = r * C_GW + h * LANES
                perm_ref.at[slot + h][pl.ds(r, tm // d, stride=d), :] = blk_ref[:, lo:lo + LANES]
        return jnp.concatenate([perm_ref[slot + h] for h in range(C_GW // LANES)], axis=1)

    o0, o1, o2 = (token_major(r, gi, 2 * gi) for gi, r in enumerate((oc0, oc1, oc2)))
    l0, l1, l2 = (token_major(r, gi, 2 * C_GROUPS + 2 * gi) for gi, r in enumerate((ls0, ls1, ls2)))
    m = jnp.maximum(jnp.maximum(l0, l1), l2)
    e0, e1, e2 = jnp.exp(l0 - m), jnp.exp(l1 - m), jnp.exp(l2 - m)
    oc = (e0 * o0 + e1 * o1 + e2 * o2) / (e0 + e1 + e2)
    branches = ((oa_ref[...], wa_ref), (ob_ref[...], wb_ref), (oc.astype(BF16), wc_ref))
    merged = None
    for k, (ob, w_ref) in enumerate(branches):
        sl = slice(k * D_MODEL, (k + 1) * D_MODEL)
        z = _dot(xb, wg_ref[:, sl]) + bg_ref[:, sl]
        gate = 1.0 / (1.0 + jnp.exp(-z))
        term = gate * _dot(ob, w_ref[...])
        merged = term if merged is None else merged + term
    h = _dot(merged.astype(BF16), wo_ref[...])
    o_ref[...] = _layer_norm(DN_ALPHA * x + h, g_ref[...], b_ref[...])


def _merge(x2d, oa, ob, ocs, lses, wg, bg, wa, wb, wc, wo, g, bta, tm):
    t = x2d.shape[0]
    row = lambda n: pl.BlockSpec((tm, n), lambda i: (i, 0))
    consts = (wg, bg, wa, wb, wc, wo, g, bta)
    cspecs = [pl.BlockSpec((tm // d, d * C_GW), lambda i: (i, 0)) for _, d in C_PATTERNS]
    in_specs = [row(D_MODEL), row(A_W), row(B_W)] + cspecs * 2 + [_const_spec(c.shape) for c in consts]
    return pl.pallas_call(
        _merge_kernel, out_shape=jax.ShapeDtypeStruct((t, D_MODEL), F32), grid=(t // tm,),
        in_specs=in_specs, out_specs=row(D_MODEL),
        scratch_shapes=[pltpu.VMEM((4 * C_GROUPS, tm, LANES), F32)],
        compiler_params=_params(("parallel",)), name="merge")(x2d, oa, ob, *ocs, *lses, *consts)


def _memkv_kernel(m_ref, wk_ref, wv_ref, k_ref, v_ref):
    mb = m_ref[...].astype(BF16)
    k_ref[...] = _dot(mb, wk_ref[...]).astype(BF16)
    v_ref[...] = _dot(mb, wv_ref[...]).astype(BF16)


def _memkv(mem2d, wk, wv, tm):
    t = mem2d.shape[0]
    row = pl.BlockSpec((tm, D_MODEL), lambda i: (i, 0))
    return pl.pallas_call(
        _memkv_kernel, out_shape=[jax.ShapeDtypeStruct((t, D_MODEL), BF16)] * 2, grid=(t // tm,),
        in_specs=[row, _const_spec(wk.shape), _const_spec(wv.shape)], out_specs=[row, row],
        compiler_params=_params(("parallel",)), name="memkv")(mem2d, wk, wv)


def _xattn_kernel(x_ref, k_ref, v_ref, wq_ref, wo_ref, g_ref, b_ref, o_ref):
    x = x_ref[...]
    q = _dot(x.astype(BF16), wq_ref[...]).astype(BF16)
    heads = []
    for h in range(X_HEADS):
        sl = slice(h * X_HEAD_DIM, (h + 1) * X_HEAD_DIM)
        st = _dot_nt(q[:, sl], k_ref[:, sl])
        m = jnp.max(st, axis=-1, keepdims=True)
        e = jnp.exp(st - m)
        l = jnp.sum(e, axis=-1, keepdims=True)
        heads.append((_dot(e.astype(BF16), v_ref[:, sl]) / l).astype(BF16))
    o = jnp.concatenate(heads, axis=-1)
    h_out = _dot(o, wo_ref[...])
    o_ref[...] = _layer_norm(DN_ALPHA * x + h_out, g_ref[...], b_ref[...])


def _xattn(x3d, k3d, v3d, wq, wo, g, bta, tm):
    b, s, d = x3d.shape
    xspec = pl.BlockSpec((None, tm, d), lambda bi, i: (bi, i, 0))
    mspec = pl.BlockSpec((None, MEM_LEN, d), lambda bi, i: (bi, 0, 0))
    consts = (wq, wo, g, bta)
    return pl.pallas_call(
        _xattn_kernel, out_shape=jax.ShapeDtypeStruct((b, s, d), F32), grid=(b, s // tm),
        in_specs=[xspec, mspec, mspec] + [_const_spec(c.shape) for c in consts], out_specs=xspec,
        compiler_params=_params(("parallel", "parallel")), name="xattn")(x3d, k3d, v3d, *consts)


MOE_EPB = 2
ROUTE_LANE0 = N_GROUPS


def _moe_kernel(x_ref, wrh_ref, wrl_ref, br_ref, wg_ref, wu_ref, wd_ref, g_ref, b_ref, o_ref,
                xb_ref, gate_ref, acc_ref):
    eb = pl.program_id(1)
    tm = x_ref.shape[0]
    lane = lax.broadcasted_iota(jnp.int32, (tm, LANES), 1)

    @pl.when(eb == 0)
    def _route():
        x = x_ref[...]
        lanef = lane.astype(F32)
        xh = x.astype(BF16)
        xl = (x - xh.astype(F32)).astype(BF16)
        xb_ref[...] = xh
        logits = (_dot(xh, wrh_ref[...]) + _dot(xl, wrh_ref[...]) + _dot(xh, wrl_ref[...])) + br_ref[...]
        gl = jnp.where(lane < N_GROUPS, logits, NEG)
        gmax = jnp.max(gl, axis=-1, keepdims=True)
        g_gate = 1.0 / jnp.sum(jnp.exp(gl - gmax), axis=-1, keepdims=True)
        g_idx = jnp.min(jnp.where(gl == gmax, lanef, float(LANES)), axis=-1, keepdims=True)
        e_group = ((lane - ROUTE_LANE0) >> 2).astype(F32)
        sel = (lane >= ROUTE_LANE0) & (e_group == g_idx)
        el = jnp.where(sel, logits, NEG)
        top1 = jnp.max(el, axis=-1, keepdims=True)
        i1 = jnp.min(jnp.where(el == top1, lanef, float(LANES)), axis=-1, keepdims=True)
        el2 = jnp.where(lanef == i1, NEG, el)
        top2 = jnp.max(el2, axis=-1, keepdims=True)
        i2 = jnp.min(jnp.where(el2 == top2, lanef, float(LANES)), axis=-1, keepdims=True)
        r = jnp.exp(top2 - top1)
        w1 = 1.0 / (1.0 + r)
        w2 = r * w1
        gate_ref[...] = jnp.where(lanef == i1, w1, jnp.where(lanef == i2, w2, 0.0)) * g_gate
        acc_ref[...] = jnp.zeros_like(acc_ref)

    xb = xb_ref[...]
    gates = gate_ref[...]
    for k in range(MOE_EPB):
        e_lane = ROUTE_LANE0 + eb * MOE_EPB + k
        gcol = jnp.sum(jnp.where(lane == e_lane, gates, 0.0), axis=-1, keepdims=True)
        a = _dot(xb, wg_ref[k])
        u = _dot(xb, wu_ref[k])
        h = (a / (1.0 + jnp.exp(-a))) * u * gcol
        acc_ref[...] += _dot(h.astype(BF16), wd_ref[k])

    @pl.when(eb == pl.num_programs(1) - 1)
    def _finish():
        o_ref[...] = _layer_norm(DN_ALPHA * x_ref[...] + acc_ref[...], g_ref[...], b_ref[...])


def _moe(x2d, wrh, wrl, br, wg, wu, wd, g, bta, tm):
    t = x2d.shape[0]
    row = pl.BlockSpec((tm, D_MODEL), lambda i, e: (i, 0))
    wspec = lambda shape: pl.BlockSpec((MOE_EPB,) + shape, lambda i, e: (e, 0, 0))
    in_specs = [row, _const_spec(wrh.shape), _const_spec(wrl.shape), _const_spec(br.shape),
                wspec((D_MODEL, D_EXPERT)), wspec((D_MODEL, D_EXPERT)), wspec((D_EXPERT, D_MODEL)),
                _const_spec(g.shape), _const_spec(bta.shape)]
    return pl.pallas_call(
        _moe_kernel, out_shape=jax.ShapeDtypeStruct((t, D_MODEL), F32),
        grid=(t // tm, N_EXPERTS // MOE_EPB), in_specs=in_specs, out_specs=row,
        scratch_shapes=[pltpu.VMEM((tm, D_MODEL), BF16), pltpu.VMEM((tm, LANES), F32),
                        pltpu.VMEM((tm, D_MODEL), F32)],
        compiler_params=_params(("parallel", "arbitrary")), name="moe")(x2d, wrh, wrl, br, wg, wu, wd, g, bta)


def _pad_cols(w, n):
    return jnp.pad(w, ((0, 0), (0, n - w.shape[1])))


def kernel(x, mem, positions, w_in, b_forget, diff_lambda, diff_subln, w_branch_a, w_branch_b, w_branch_c,
           w_gate, b_gate, w_out, w_xq, w_xk, w_xv, w_xo, w_route_group, b_route_group, w_route_expert,
           b_route_expert, w_expert_gate, w_expert_up, w_expert_down, ln_g, ln_b):
    b, s, d = x.shape
    t = b * s
    depth = w_in.shape[0]
    scale = HEAD_DIM ** -0.5

    half = HEAD_DIM // 2
    inv = ROPE_THETA ** (-jnp.arange(half, dtype=F32) / half)
    ang = positions.astype(F32).reshape(t, 1) * inv
    cos, sin = jnp.cos(ang), jnp.sin(ang)
    cos_t = jnp.concatenate([cos, cos, cos, cos], axis=-1)
    sin_t = jnp.concatenate([-sin, sin, -sin, sin], axis=-1)

    mem2d = mem.reshape(b * MEM_LEN, d)
    x2d = x.reshape(t, d)
    row = lambda v: v.reshape(1, -1).astype(F32)
    for l in range(depth):
        lam_init = 0.8 - 0.6 * math.exp(-0.3 * l)
        offs = [0, A_W, 2 * A_W, 3 * A_W, 3 * A_W + B_W, 3 * A_W + 2 * B_W, 3 * A_W + 3 * B_W]
        offs.append(offs[-1] + B_HEADS)
        offs += [offs[-1] + C_W, offs[-1] + 2 * C_W, offs[-1] + 3 * C_W]
        wl = w_in[l]
        seg = [wl[:, offs[k]:offs[k + 1]] for k in range(10)]
        qa, ka, va, qb, kb, vb, fb, qc, kc, vc = seg
        ws = [(qa * (scale * LOG2E)), ka, va.T, (qb * (scale * LOG2E)), kb, vb.T, _pad_cols(fb, LANES),
              (qc * scale), kc, vc]
        ws = [w.astype(BF16) for w in ws]
        bfp = _pad_cols(row(b_forget[l]), LANES)

        outs = _inproj(x2d, cos_t, sin_t, bfp, ws, tm=512)
        o_qa, o_ka, o_va, o_qb, o_kb, o_vb, o_lf = outs[:7]
        o_qc, o_kc, o_vc = outs[7:10], outs[10:13], outs[13:16]

        lf = diff_lambda[l].astype(F32)
        lam = jnp.exp(jnp.dot(lf[0], lf[1])) - jnp.exp(jnp.dot(lf[2], lf[3])) + lam_init
        lam_row = jnp.full((1, LANES), lam, F32)
        gain = row(diff_subln[l]) * (1.0 - lam_init)
        r3 = lambda v: v.reshape(b, s, v.shape[-1])
        oa = _flash(r3(o_qa), r3(o_ka), o_va, (lam_row, gain), forget=False, bq=2 * FLASH_BK)
        c = jnp.cumsum(o_lf.reshape(b, s, LANES)[:, :, :B_HEADS], axis=1)
        c_t = jnp.transpose(c.reshape(b, s, B_HEADS // 2, 2), (0, 2, 1, 3))
        ob = _flash(r3(o_qb), r3(o_kb), o_vb, (c_t,), forget=True, bq=2 * FLASH_BK)
        ocs, lses = [], []
        for gi, (_, dil) in enumerate(C_PATTERNS):
            o_g, lse_g = _dilated(o_qc[gi], o_kc[gi], o_vc[gi], b, dil, nq=min(4, s // (dil * C_BLOCK)))
            ocs.append(o_g)
            lses.append(lse_g)
        x2d = _merge(x2d, oa.reshape(t, A_W), ob.reshape(t, B_W), ocs, lses,
                     w_gate[l].astype(BF16), row(b_gate[l]), w_branch_a[l].astype(BF16),
                     w_branch_b[l].astype(BF16), w_branch_c[l].astype(BF16), w_out[l].astype(BF16),
                     row(ln_g[l, 0]), row(ln_b[l, 0]), tm=256)
        xscale = X_HEAD_DIM ** -0.5
        mk, mv = _memkv(mem2d, w_xk[l].astype(BF16), w_xv[l].astype(BF16), tm=MEM_LEN)
        x2d = _xattn(x2d.reshape(b, s, d), mk.reshape(b, MEM_LEN, d), mv.reshape(b, MEM_LEN, d),
                     (w_xq[l] * xscale).astype(BF16), w_xo[l].astype(BF16),
                     row(ln_g[l, 1]), row(ln_b[l, 1]), tm=512).reshape(t, d)
        wr = jnp.concatenate([w_route_group[l]] + [w_route_expert[l, gi] for gi in range(N_GROUPS)], axis=1)
        wr = _pad_cols(wr.astype(F32), LANES)
        wrh = wr.astype(BF16)
        wrl = (wr - wrh.astype(F32)).astype(BF16)
        br = _pad_cols(row(jnp.concatenate([b_route_group[l], b_route_expert[l].reshape(-1)])), LANES)
        ne = (N_EXPERTS,)
        x2d = _moe(x2d, wrh, wrl, br,
                   w_expert_gate[l].reshape(ne + w_expert_gate.shape[3:]).astype(BF16),
                   w_expert_up[l].reshape(ne + w_expert_up.shape[3:]).astype(BF16),
                   w_expert_down[l].reshape(ne + w_expert_down.shape[3:]).astype(BF16),
                   row(ln_g[l, 2]), row(ln_b[l, 2]), tm=1024)
    return x2d.reshape(b, s, d)
```

```python
import functools
import math

import jax
import jax.numpy as jnp
from jax import lax
from jax.experimental import pallas as pl
from jax.experimental.pallas import tpu as pltpu

F32 = jnp.float32
BF16 = jnp.bfloat16

D_MODEL = 1024
HEAD_DIM = 64
ROPE_THETA = 10000.0
EPS = 1e-5
MEM_LEN = 256
A_HEADS = 4
B_HEADS = 8
C_PATTERNS = ((128, 1), (512, 4), (2048, 16))
C_GROUPS = 3
C_SLOTS = 4
C_BLOCK = 128
X_HEADS = 4
X_HEAD_DIM = D_MODEL // X_HEADS
N_GROUPS = 4
EXPERTS_PER_GROUP = 4
N_EXPERTS = N_GROUPS * EXPERTS_PER_GROUP
D_EXPERT = D_MODEL // 4
DEPTH = 2
DN_ALPHA = (2 * DEPTH) ** 0.25

LANES = 128
NEG = -1e30
LOG2E = math.log2(math.e)
VMEM_LIMIT = 56 * 1024 * 1024

A_W = A_HEADS * 2 * HEAD_DIM
B_W = B_HEADS * HEAD_DIM
C_GW = C_SLOTS * HEAD_DIM
C_W = C_GROUPS * C_GW
ONES_ROWS = 16
VA_ROWS = A_HEADS * (2 * HEAD_DIM + ONES_ROWS)
VB_ROWS = B_HEADS * (HEAD_DIM + ONES_ROWS)
FLASH_BK = 512


def _params(sem, flags=None):
    return pltpu.CompilerParams(dimension_semantics=sem, vmem_limit_bytes=VMEM_LIMIT, flags=flags)


def _const_spec(shape):
    nd = len(shape)
    return pl.BlockSpec(shape, lambda *_: (0,) * nd, pipeline_mode=pl.Buffered(1))


def _layer_norm(y, g, b):
    mu = jnp.mean(y, axis=-1, keepdims=True)
    yc = y - mu
    var = jnp.mean(yc * yc, axis=-1, keepdims=True)
    return yc * lax.rsqrt(var + EPS) * g + b


def _dot(a, b):
    return jnp.dot(a, b, preferred_element_type=F32)


def _dot_nt(a, b):
    return lax.dot_general(a, b, (((1,), (1,)), ((), ())), preferred_element_type=F32)


def _inproj_kernel(x_ref, cos_ref, sin_ref, bf_ref,
                   w_qa, w_ka, w_va, w_qb, w_kb, w_vb, w_fb, w_qc, w_kc, w_vc,
                   o_qa, o_ka, o_va, o_qb, o_kb, o_vb, o_lf,
                   o_qc0, o_qc1, o_qc2, o_kc0, o_kc1, o_kc2, o_vc0, o_vc1, o_vc2, perm_ref):
    xb = x_ref[...].astype(BF16)
    cos = cos_ref[...]
    sin = sin_ref[...]
    lane = lax.broadcasted_iota(jnp.int32, cos.shape, 1)
    lower = (lane & (HEAD_DIM - 1)) < HEAD_DIM // 2

    def rope(t):
        sw = jnp.where(lower, pltpu.roll(t, LANES - HEAD_DIM // 2, 1), pltpu.roll(t, HEAD_DIM // 2, 1))
        return t * cos + sw * sin

    def roped(w_ref, outs, width):
        y = _dot(xb, w_ref[...])
        per_out = width // LANES
        for c in range(y.shape[1] // LANES):
            o = outs[c // per_out]
            lo = (c % per_out) * LANES
            o[:, lo:lo + LANES] = rope(y[:, c * LANES:(c + 1) * LANES]).astype(BF16)

    def plain(w_ref, outs, width):
        y = _dot(xb, w_ref[...])
        for c, o in enumerate(outs):
            o[...] = y[:, c * width:(c + 1) * width].astype(BF16)

    def transposed(wt_ref, o, width):
        yt = _dot_nt(wt_ref[...], xb)
        ones = jnp.ones((ONES_ROWS, FLASH_BK), F32)
        for c in range(o.shape[0]):
            cols = slice(c * FLASH_BK, (c + 1) * FLASH_BK)
            parts = []
            for r0 in range(0, yt.shape[0], width):
                parts += [yt[r0:r0 + width, cols], ones]
            o[c] = jnp.concatenate(parts, axis=0).astype(BF16)

    roped(w_qa, (o_qa,), A_W)
    roped(w_ka, (o_ka,), A_W)
    transposed(w_va, o_va, LANES)
    plain(w_qb, (o_qb,), B_W)
    plain(w_kb, (o_kb,), B_W)
    transposed(w_vb, o_vb, HEAD_DIM)
    def dilated(w_ref, outs, slot, use_rope):
        y = _dot(xb, w_ref[...])
        tm = y.shape[0]
        for gi, (_, d) in enumerate(C_PATTERNS):
            for h in range(C_GW // LANES):
                c = gi * (C_GW // LANES) + h
                t = y[:, c * LANES:(c + 1) * LANES]
                t = rope(t) if use_rope else t
                if d == 1:
                    outs[gi][:, h * LANES:(h + 1) * LANES] = t.astype(BF16)
                    continue
                sl = slot + (gi - 1) * 2 + h
                perm_ref[sl] = t
                for r in range(d):
                    lo = r * C_GW + h * LANES
                    outs[gi][:, lo:lo + LANES] = perm_ref.at[sl][pl.ds(r, tm // d, stride=d), :].astype(BF16)

    dilated(w_qc, (o_qc0, o_qc1, o_qc2), 0, True)
    dilated(w_kc, (o_kc0, o_kc1, o_kc2), 4, True)
    dilated(w_vc, (o_vc0, o_vc1, o_vc2), 8, False)
    z =_dot(xb, w_fb[...]) + bf_ref[...]
    o_lf[...] = jnp.minimum(z, 0.0) - jnp.log(1.0 + jnp.exp(-jnp.abs(z)))


def _inproj(x2d, cos, sin, bfp, ws, tm):
    t = x2d.shape[0]
    row = lambda n: pl.BlockSpec((tm, n), lambda i: (i, 0))
    tshape = lambda rows: jax.ShapeDtypeStruct((t // FLASH_BK, rows, FLASH_BK), BF16)
    tspec = lambda rows: pl.BlockSpec((tm // FLASH_BK, rows, FLASH_BK), lambda i: (i, 0, 0))
    rshape = lambda w: jax.ShapeDtypeStruct((t, w), BF16)
    out_shape = [rshape(A_W), rshape(A_W), tshape(VA_ROWS), rshape(B_W), rshape(B_W), tshape(VB_ROWS)]
    out_shape.append(jax.ShapeDtypeStruct((t, LANES), F32))
    out_specs = [row(A_W), row(A_W), tspec(VA_ROWS), row(B_W), row(B_W), tspec(VB_ROWS), row(LANES)]
    for _ in range(3):
        for _, d in C_PATTERNS:
            out_shape.append(jax.ShapeDtypeStruct((t // d, d * C_GW), BF16))
            out_specs.append(pl.BlockSpec((tm // d, d * C_GW), lambda i: (i, 0)))
    in_specs = [row(D_MODEL), row(LANES), row(LANES), _const_spec((1, LANES))]
    in_specs += [_const_spec(w.shape) for w in ws]
    n_perm = 3 * (C_GROUPS - 1) * (C_GW // LANES)
    return pl.pallas_call(
        _inproj_kernel, out_shape=out_shape, grid=(t // tm,), in_specs=in_specs, out_specs=out_specs,
        scratch_shapes=[pltpu.VMEM((n_perm, tm, LANES), F32)],
        compiler_params=_params(("parallel",)), name="inproj")(x2d, cos, sin, bfp, *ws)


def _flash_kernel(forget, bq, *refs):
    bk = FLASH_BK
    if forget:
        q_ref, k_ref, vt_ref, c_ref, o_ref, m_ref, acc_ref, st_a, st_b, bias_ref = refs
    else:
        q_ref, k_ref, vt_ref, lam_ref, gain_ref, o_ref, m_ref, acc_ref, st_a, st_b = refs
    dvp = acc_ref.shape[2]
    dv = dvp - ONES_ROWS
    i = pl.program_id(2)
    q = q_ref[...].astype(F32)
    lane = lax.broadcasted_iota(jnp.int32, (bq, LANES), 1)
    low = lane < HEAD_DIM
    qts = (jnp.where(low, q, 0.0).T.astype(BF16), jnp.where(low, 0.0, q).T.astype(BF16))
    key = lax.broadcasted_iota(jnp.int32, (bk, bk), 0)
    qry = lax.broadcasted_iota(jnp.int32, (bk, bk), 1)
    halves = bq // bk

    if forget:
        @pl.when(i == 0)
        def _bias():
            for s in range(2):
                bias_ref[s] = jnp.broadcast_to(-LOG2E * c_ref[:, s:s + 1], bias_ref.shape[1:])

    m_ref[...] = jnp.full(m_ref.shape, NEG, F32)
    acc_ref[...] = jnp.zeros(acc_ref.shape, F32)

    def scores(j, st_ref, first_half=0):
        off = pl.multiple_of(j * bk, bk)
        kc = k_ref[pl.ds(off, bk), :]
        for s in range(2):
            for h in range(first_half, halves):
                st_ref[s, h] = _dot(kc, qts[s][:, h * bk:(h + 1) * bk])

    def absorb(j, st_ref, first_half=0, diagonal=False):
        off = pl.multiple_of(j * bk, bk)
        for s in range(2):
            for h in range(first_half, halves):
                st = st_ref[s, h]
                if forget:
                    bb = bias_ref[s, pl.ds(off, bk), :]
                    st = st + jnp.concatenate([bb] * (bk // LANES), axis=1)
                if diagonal and h == first_half:
                    st = jnp.where(key <= qry, st, NEG)
                m_prev = m_ref[s, h]
                m_new = jnp.maximum(m_prev, jnp.max(st, axis=0, keepdims=True))
                a = jnp.exp2(m_prev - m_new)
                p = jnp.exp2(st - m_new).astype(BF16)
                m_ref[s, h] = m_new
                vt = vt_ref[j, s * dvp:(s + 1) * dvp, :] if forget else vt_ref[j]
                acc_ref[s, h] = a * acc_ref[s, h] + _dot(vt, p)

    scores(0, st_a)

    def pair(t, c):
        scores(2 * t + 1, st_b)
        absorb(2 * t, st_a)
        scores(2 * t + 2, st_a)
        absorb(2 * t + 1, st_b)
        return c

    lax.fori_loop(0, i, pair, 0)
    scores(2 * i + 1, st_b, 1)
    absorb(2 * i, st_a, 0, True)
    absorb(2 * i + 1, st_b, 1, True)

    for h in range(halves):
        o0 = acc_ref[0, h, :dv, :] / acc_ref[0, h, dv:dv + 1, :]
        o1 = acc_ref[1, h, :dv, :] / acc_ref[1, h, dv:dv + 1, :]
        rows = slice(h * bk, (h + 1) * bk)
        if forget:
            o_ref[rows, :] = jnp.concatenate([o0, o1], axis=0).T.astype(o_ref.dtype)
        else:
            o = (o0 - lam_ref[:, 0:1] * o1).T
            ms = jnp.mean(o * o, axis=-1, keepdims=True)
            o_ref[rows, :] = (o * lax.rsqrt(ms + EPS) * gain_ref[...]).astype(o_ref.dtype)


def _flash(q, k, vt, extra, forget, bq):
    b, s, w = q.shape
    g = w // LANES
    nk = s // FLASH_BK
    dvp = (HEAD_DIM if forget else LANES) + ONES_ROWS
    rows = vt.shape[1] // g
    qspec = pl.BlockSpec((None, bq, LANES), lambda bi, gi, i: (bi, i, gi))
    kspec = pl.BlockSpec((None, s, LANES), lambda bi, gi, i: (bi, 0, gi))
    vspec = pl.BlockSpec((nk, rows, FLASH_BK), lambda bi, gi, i: (bi, gi, 0))
    assert bq == 2 * FLASH_BK
    bk = FLASH_BK
    scratch = [pltpu.VMEM((2, 2, 1, bk), F32), pltpu.VMEM((2, 2, dvp, bk), F32),
               pltpu.VMEM((2, 2, bk, bk), F32), pltpu.VMEM((2, 2, bk, bk), F32)]
    if forget:
        especs = [pl.BlockSpec((None, None, s, 2), lambda bi, gi, i: (bi, gi, 0, 0))]
        scratch.append(pltpu.VMEM((2, s, LANES), F32))
    else:
        especs = [_const_spec((1, LANES)), _const_spec((1, LANES))]
    return pl.pallas_call(
        functools.partial(_flash_kernel, forget, bq),
        out_shape=jax.ShapeDtypeStruct((b, s, w), BF16),
        grid=(b, g, s // bq), in_specs=[qspec, kspec, vspec] + especs, out_specs=qspec,
        scratch_shapes=scratch,
        compiler_params=_params(("parallel", "parallel", "arbitrary")),
        name="fox_attn" if forget else "diff_attn")(q, k, vt, *extra)


def _dilated_kernel(nq, q_ref, kp_ref, kc_ref, vp_ref, vc_ref, o_ref, lse_ref):
    n = C_BLOCK
    j = pl.program_id(2)
    kall = jnp.concatenate([kp_ref[...], kc_ref[...]], axis=0)
    vall = jnp.concatenate([vp_ref[...], vc_ref[...]], axis=0)
    qi = lax.broadcasted_iota(jnp.int32, (n, 2 * n), 0)
    kj = lax.broadcasted_iota(jnp.int32, (n, 2 * n), 1)
    dist = n + qi - kj
    window = (dist >= 0) & (dist <= n)
    first_key = jnp.where(j > 0, 0, n)
    lane = lax.broadcasted_iota(jnp.int32, (n, LANES), 1)
    low = lane < HEAD_DIM
    units = [(t, p, s) for t in range(nq) for p in range(C_GW // LANES) for s in range(2)]
    scores = {}
    for t, p, s in units:
        sl = slice(p * LANES, (p + 1) * LANES)
        q128 = q_ref[t * n:(t + 1) * n, sl]
        zero = jnp.zeros_like(q128)
        qm = jnp.where(low, q128, zero) if s == 0 else jnp.where(low, zero, q128)
        mask = (window & (kj >= first_key)) if t == 0 else window
        scores[t, p, s] = jnp.where(mask, _dot_nt(qm, kall[t * n:(t + 2) * n, sl]), NEG)
    probs = {}
    for u in units:
        m = jnp.max(scores[u], axis=-1, keepdims=True)
        probs[u] = (jnp.exp2(scores[u] - m).astype(BF16), m)
    low_keys = lax.broadcasted_iota(jnp.int32, (2 * n, LANES), 1) < HEAD_DIM
    for t in range(nq):
        for p in range(C_GW // LANES):
            sl = slice(p * LANES, (p + 1) * LANES)
            v128 = vall[t * n:(t + 2) * n, sl]
            one = jnp.ones_like(v128)
            os, ls = [], []
            for s in range(2):
                vs = jnp.where(low_keys, v128, one) if s == 0 else jnp.where(low_keys, one, v128)
                num = _dot(probs[t, p, s][0], vs)
                den = pltpu.roll(num, HEAD_DIM, 1)
                os.append(num / den)
                ls.append(probs[t, p, s][1] + jnp.log2(den))
            o_ref[t * n:(t + 1) * n, sl] = jnp.where(low, os[0], os[1])
            lse_ref[t * n:(t + 1) * n, sl] = jnp.where(low, ls[0], ls[1])


def _dilated(q, k, v, b, dilation, nq):
    w = C_GW
    n = C_BLOCK
    m_len = q.shape[0] // b
    nb = m_len // (n * nq)
    view = lambda t: t.reshape(b, m_len, dilation * w)
    cur = pl.BlockSpec((None, n * nq, w), lambda bi, r, j: (bi, j, r))
    prev = pl.BlockSpec((None, n, w), lambda bi, r, j: (bi, jnp.maximum(j * nq - 1, 0), r))
    o, lse = pl.pallas_call(
        functools.partial(_dilated_kernel, nq),
        out_shape=[jax.ShapeDtypeStruct((b, m_len, dilation * w), F32)] * 2,
        grid=(b, dilation, nb), in_specs=[cur, prev, cur, prev, cur], out_specs=[cur, cur],
        compiler_params=_params(("parallel", "parallel", "arbitrary")),
        name=f"dilated_d{dilation}")(view(q), view(k), view(k), view(v), view(v))
    return o.reshape(b * m_len, dilation * w), lse.reshape(b * m_len, dilation * w)


def _merge_kernel(x_ref, oa_ref, ob_ref, oc0, oc1, oc2, ls0, ls1, ls2,
                  wg_ref, bg_ref, wa_ref, wb_ref, wc_ref, wo_ref, g_ref, b_ref, o_ref, perm_ref):
    x = x_ref[...]
    xb = x.astype(BF16)
    tm = x.shape[0]

    def token_major(blk_ref, gi, slot):
        d = C_PATTERNS[gi][1]
        if d == 1:
            return blk_ref[...]
        for h in range(C_GW // LANES):
            for r in range(d):
                lo = r * C_GW + h * LANES
                perm_ref.at[slot + h][pl.ds(r, tm // d, stride=d), :] = blk_ref[:, lo:lo + LANES]
        return jnp.concatenate([perm_ref[slot + h] for h in range(C_GW // LANES)], axis=1)

    o0, o1, o2 = (token_major(r, gi, 2 * gi) for gi, r in enumerate((oc0, oc1, oc2)))
    l0, l1, l2 = (token_major(r, gi, 2 * C_GROUPS + 2 * gi) for gi, r in enumerate((ls0, ls1, ls2)))
    m = jnp.maximum(jnp.maximum(l0, l1), l2)
    e0, e1, e2 = jnp.exp2(l0 - m), jnp.exp2(l1 - m), jnp.exp2(l2 - m)
    oc = (e0 * o0 + e1 * o1 + e2 * o2) / (e0 + e1 + e2)
    branches = ((oa_ref[...], wa_ref), (ob_ref[...], wb_ref), (oc.astype(BF16), wc_ref))
    merged = None
    for k, (ob, w_ref) in enumerate(branches):
        sl = slice(k * D_MODEL, (k + 1) * D_MODEL)
        z = _dot(xb, wg_ref[:, sl]) + bg_ref[:, sl]
        gate = 1.0 / (1.0 + jnp.exp(-z))
        term = gate * _dot(ob, w_ref[...])
        merged = term if merged is None else merged + term
    h = _dot(merged.astype(BF16), wo_ref[...])
    o_ref[...] = _layer_norm(DN_ALPHA * x + h, g_ref[...], b_ref[...])


def _merge(x2d, oa, ob, ocs, lses, wg, bg, wa, wb, wc, wo, g, bta, tm):
    t = x2d.shape[0]
    row = lambda n: pl.BlockSpec((tm, n), lambda i: (i, 0))
    consts = (wg, bg, wa, wb, wc, wo, g, bta)
    cspecs = [pl.BlockSpec((tm // d, d * C_GW), lambda i: (i, 0)) for _, d in C_PATTERNS]
    in_specs = [row(D_MODEL), row(A_W), row(B_W)] + cspecs * 2 + [_const_spec(c.shape) for c in consts]
    return pl.pallas_call(
        _merge_kernel, out_shape=jax.ShapeDtypeStruct((t, D_MODEL), F32), grid=(t // tm,),
        in_specs=in_specs, out_specs=row(D_MODEL),
        scratch_shapes=[pltpu.VMEM((4 * C_GROUPS, tm, LANES), F32)],
        compiler_params=_params(("parallel",)), name="merge")(x2d, oa, ob, *ocs, *lses, *consts)


def _memkv_kernel(m_ref, wk_ref, wv_ref, k_ref, v_ref):
    mb = m_ref[...].astype(BF16)
    k_ref[...] = _dot(mb, wk_ref[...]).astype(BF16)
    v_ref[...] = _dot(mb, wv_ref[...]).astype(BF16)


def _memkv(mem2d, wk, wv, tm):
    t = mem2d.shape[0]
    row = pl.BlockSpec((tm, D_MODEL), lambda i: (i, 0))
    return pl.pallas_call(
        _memkv_kernel, out_shape=[jax.ShapeDtypeStruct((t, D_MODEL), BF16)] * 2, grid=(t // tm,),
        in_specs=[row, _const_spec(wk.shape), _const_spec(wv.shape)], out_specs=[row, row],
        compiler_params=_params(("parallel",)), name="memkv")(mem2d, wk, wv)


def _xattn_kernel(x_ref, k_ref, v_ref, wq_ref, wo_ref, g_ref, b_ref, o_ref):
    x = x_ref[...]
    q = _dot(x.astype(BF16), wq_ref[...]).astype(BF16)
    heads = []
    for h in range(X_HEADS):
        sl = slice(h * X_HEAD_DIM, (h + 1) * X_HEAD_DIM)
        st = _dot_nt(q[:, sl], k_ref[:, sl])
        m = jnp.max(st, axis=-1, keepdims=True)
        e = jnp.exp(st - m)
        l = jnp.sum(e, axis=-1, keepdims=True)
        heads.append((_dot(e.astype(BF16), v_ref[:, sl]) / l).astype(BF16))
    o = jnp.concatenate(heads, axis=-1)
    h_out = _dot(o, wo_ref[...])
    o_ref[...] = _layer_norm(DN_ALPHA * x + h_out, g_ref[...], b_ref[...])


def _xattn(x3d, k3d, v3d, wq, wo, g, bta, tm):
    b, s, d = x3d.shape
    xspec = pl.BlockSpec((None, tm, d), lambda bi, i: (bi, i, 0))
    mspec = pl.BlockSpec((None, MEM_LEN, d), lambda bi, i: (bi, 0, 0))
    consts = (wq, wo, g, bta)
    return pl.pallas_call(
        _xattn_kernel, out_shape=jax.ShapeDtypeStruct((b, s, d), F32), grid=(b, s // tm),
        in_specs=[xspec, mspec, mspec] + [_const_spec(c.shape) for c in consts], out_specs=xspec,
        compiler_params=_params(("parallel", "parallel")), name="xattn")(x3d, k3d, v3d, *consts)


MOE_EPB = 2
ROUTE_LANE0 = N_GROUPS


def _moe_kernel(x_ref, wrh_ref, wrl_ref, br_ref, wg_ref, wu_ref, wd_ref, g_ref, b_ref, o_ref,
                xb_ref, gate_ref, acc_ref):
    eb = pl.program_id(1)
    tm = x_ref.shape[0]
    lane = lax.broadcasted_iota(jnp.int32, (tm, LANES), 1)

    @pl.when(eb == 0)
    def _route():
        x = x_ref[...]
        lanef = lane.astype(F32)
        xh = x.astype(BF16)
        xl = (x - xh.astype(F32)).astype(BF16)
        xb_ref[...] = xh
        hh_hl = _dot(xh, jnp.concatenate([wrh_ref[...], wrl_ref[...]], axis=1))
        logits = (hh_hl[:, :LANES] + _dot(xl, wrh_ref[...]) + hh_hl[:, LANES:]) + br_ref[...]
        gl = jnp.where(lane < N_GROUPS, logits, NEG)
        gmax = jnp.max(gl, axis=-1, keepdims=True)
        g_gate = 1.0 / jnp.sum(jnp.exp(gl - gmax), axis=-1, keepdims=True)
        g_idx = jnp.min(jnp.where(gl == gmax, lanef, float(LANES)), axis=-1, keepdims=True)
        e_group = ((lane - ROUTE_LANE0) >> 2).astype(F32)
        sel = (lane >= ROUTE_LANE0) & (e_group == g_idx)
        el = jnp.where(sel, logits, NEG)
        top1 = jnp.max(el, axis=-1, keepdims=True)
        i1 = jnp.min(jnp.where(el == top1, lanef, float(LANES)), axis=-1, keepdims=True)
        el2 = jnp.where(lanef == i1, NEG, el)
        top2 = jnp.max(el2, axis=-1, keepdims=True)
        i2 = jnp.min(jnp.where(el2 == top2, lanef, float(LANES)), axis=-1, keepdims=True)
        r = jnp.exp(top2 - top1)
        w1 = 1.0 / (1.0 + r)
        w2 = r * w1
        gate_ref[...] = jnp.where(lanef == i1, w1, jnp.where(lanef == i2, w2, 0.0)) * g_gate
        acc_ref[...] = jnp.zeros_like(acc_ref)

    xb = xb_ref[...]
    gates = gate_ref[...]
    for k in range(MOE_EPB):
        e_lane = ROUTE_LANE0 + eb * MOE_EPB + k
        gcol = jnp.sum(jnp.where(lane == e_lane, gates, 0.0), axis=-1, keepdims=True)
        a = _dot(xb, wg_ref[k])
        u = _dot(xb, wu_ref[k])
        h = (a / (1.0 + jnp.exp(-a))) * u * gcol
        acc_ref[...] += _dot(h.astype(BF16), wd_ref[k])

    @pl.when(eb == pl.num_programs(1) - 1)
    def _finish():
        o_ref[...] = _layer_norm(DN_ALPHA * x_ref[...] + acc_ref[...], g_ref[...], b_ref[...])


def _moe(x2d, wrh, wrl, br, wg, wu, wd, g, bta, tm):
    t = x2d.shape[0]
    row = pl.BlockSpec((tm, D_MODEL), lambda i, e: (i, 0))
    wspec = lambda shape: pl.BlockSpec((MOE_EPB,) + shape, lambda i, e: (e, 0, 0))
    in_specs = [row, _const_spec(wrh.shape), _const_spec(wrl.shape), _const_spec(br.shape),
                wspec((D_MODEL, D_EXPERT)), wspec((D_MODEL, D_EXPERT)), wspec((D_EXPERT, D_MODEL)),
                _const_spec(g.shape), _const_spec(bta.shape)]
    return pl.pallas_call(
        _moe_kernel, out_shape=jax.ShapeDtypeStruct((t, D_MODEL), F32),
        grid=(t // tm, N_EXPERTS // MOE_EPB), in_specs=in_specs, out_specs=row,
        scratch_shapes=[pltpu.VMEM((tm, D_MODEL), BF16), pltpu.VMEM((tm, LANES), F32),
                        pltpu.VMEM((tm, D_MODEL), F32)],
        compiler_params=_params(("parallel", "arbitrary")), name="moe")(x2d, wrh, wrl, br, wg, wu, wd, g, bta)


def _pad_cols(w, n):
    return jnp.pad(w, ((0, 0), (0, n - w.shape[1])))


def kernel(x, mem, positions, w_in, b_forget, diff_lambda, diff_subln, w_branch_a, w_branch_b, w_branch_c,
           w_gate, b_gate, w_out, w_xq, w_xk, w_xv, w_xo, w_route_group, b_route_group, w_route_expert,
           b_route_expert, w_expert_gate, w_expert_up, w_expert_down, ln_g, ln_b):
    b, s, d = x.shape
    t = b * s
    depth = w_in.shape[0]
    scale = HEAD_DIM ** -0.5

    half = HEAD_DIM // 2
    inv = ROPE_THETA ** (-jnp.arange(half, dtype=F32) / half)
    ang = positions.astype(F32).reshape(t, 1) * inv
    cos, sin = jnp.cos(ang), jnp.sin(ang)
    cos_t = jnp.concatenate([cos, cos, cos, cos], axis=-1)
    sin_t = jnp.concatenate([-sin, sin, -sin, sin], axis=-1)

    mem2d = mem.reshape(b * MEM_LEN, d)
    x2d = x.reshape(t, d)
    row = lambda v: v.reshape(1, -1).astype(F32)
    for l in range(depth):
        lam_init = 0.8 - 0.6 * math.exp(-0.3 * l)
        offs = [0, A_W, 2 * A_W, 3 * A_W, 3 * A_W + B_W, 3 * A_W + 2 * B_W, 3 * A_W + 3 * B_W]
        offs.append(offs[-1] + B_HEADS)
        offs += [offs[-1] + C_W, offs[-1] + 2 * C_W, offs[-1] + 3 * C_W]
        wl = w_in[l]
        seg = [wl[:, offs[k]:offs[k + 1]] for k in range(10)]
        qa, ka, va, qb, kb, vb, fb, qc, kc, vc = seg
        ws = [(qa * (scale * LOG2E)), ka, va.T, (qb * (scale * LOG2E)), kb, vb.T, _pad_cols(fb, LANES),
              (qc * (scale * LOG2E)), kc, vc]
        ws = [w.astype(BF16) for w in ws]
        bfp = _pad_cols(row(b_forget[l]), LANES)

        outs = _inproj(x2d, cos_t, sin_t, bfp, ws, tm=512)
        o_qa, o_ka, o_va, o_qb, o_kb, o_vb, o_lf = outs[:7]
        o_qc, o_kc, o_vc = outs[7:10], outs[10:13], outs[13:16]

        lf = diff_lambda[l].astype(F32)
        lam = jnp.exp(jnp.dot(lf[0], lf[1])) - jnp.exp(jnp.dot(lf[2], lf[3])) + lam_init
        lam_row = jnp.full((1, LANES), lam, F32)
        gain = row(diff_subln[l]) * (1.0 - lam_init)
        r3 = lambda v: v.reshape(b, s, v.shape[-1])
        oa = _flash(r3(o_qa), r3(o_ka), o_va, (lam_row, gain), forget=False, bq=2 * FLASH_BK)
        c = jnp.cumsum(o_lf.reshape(b, s, LANES)[:, :, :B_HEADS], axis=1)
        c_t = jnp.transpose(c.reshape(b, s, B_HEADS // 2, 2), (0, 2, 1, 3))
        ob = _flash(r3(o_qb), r3(o_kb), o_vb, (c_t,), forget=True, bq=2 * FLASH_BK)
        ocs, lses = [], []
        for gi, (_, dil) in enumerate(C_PATTERNS):
            o_g, lse_g = _dilated(o_qc[gi], o_kc[gi], o_vc[gi], b, dil, nq=min(4, s // (dil * C_BLOCK)))
            ocs.append(o_g)
            lses.append(lse_g)
        x2d = _merge(x2d, oa.reshape(t, A_W), ob.reshape(t, B_W), ocs, lses,
                     w_gate[l].astype(BF16), row(b_gate[l]), w_branch_a[l].astype(BF16),
                     w_branch_b[l].astype(BF16), w_branch_c[l].astype(BF16), w_out[l].astype(BF16),
                     row(ln_g[l, 0]), row(ln_b[l, 0]), tm=512)
        xscale = X_HEAD_DIM ** -0.5
        mk, mv = _memkv(mem2d, w_xk[l].astype(BF16), w_xv[l].astype(BF16), tm=MEM_LEN)
        x2d = _xattn(x2d.reshape(b, s, d), mk.reshape(b, MEM_LEN, d), mv.reshape(b, MEM_LEN, d),
                     (w_xq[l] * xscale).astype(BF16), w_xo[l].astype(BF16),
                     row(ln_g[l, 1]), row(ln_b[l, 1]), tm=512).reshape(t, d)
        wr = jnp.concatenate([w_route_group[l]] + [w_route_expert[l, gi] for gi in range(N_GROUPS)], axis=1)
        wr = _pad_cols(wr.astype(F32), LANES)
        wrh = wr.astype(BF16)
        wrl = (wr - wrh.astype(F32)).astype(BF16)
        br = _pad_cols(row(jnp.concatenate([b_route_group[l], b_route_expert[l].reshape(-1)])), LANES)
        ne = (N_EXPERTS,)
        x2d = _moe(x2d, wrh, wrl, br,
                   w_expert_gate[l].reshape(ne + w_expert_gate.shape[3:]).astype(BF16),
                   w_expert_up[l].reshape(ne + w_expert_up.shape[3:]).astype(BF16),
                   w_expert_down[l].reshape(ne + w_expert_down.shape[3:]).astype(BF16),
                   row(ln_g[l, 2]), row(ln_b[l, 2]), tm=1024)
    return x2d.reshape(b, s, d)
```

```python
import functools
import math

import jax
import jax.numpy as jnp
from jax import lax
from jax.experimental import pallas as pl
from jax.experimental.pallas import tpu as pltpu

F32 = jnp.float32
BF16 = jnp.bfloat16

D_MODEL = 1024
HEAD_DIM = 64
ROPE_THETA = 10000.0
EPS = 1e-5
MEM_LEN = 256
A_HEADS = 4
B_HEADS = 8
C_PATTERNS = ((128, 1), (512, 4), (2048, 16))
C_GROUPS = 3
C_SLOTS = 4
C_BLOCK = 128
X_HEADS = 4
X_HEAD_DIM = D_MODEL // X_HEADS
N_GROUPS = 4
EXPERTS_PER_GROUP = 4
N_EXPERTS = N_GROUPS * EXPERTS_PER_GROUP
D_EXPERT = D_MODEL // 4
DEPTH = 2
DN_ALPHA = (2 * DEPTH) ** 0.25

LANES = 128
NEG = -1e30
LOG2E = math.log2(math.e)
VMEM_LIMIT = 56 * 1024 * 1024

A_W = A_HEADS * 2 * HEAD_DIM
B_W = B_HEADS * HEAD_DIM
C_GW = C_SLOTS * HEAD_DIM
C_W = C_GROUPS * C_GW
ONES_ROWS = 16
VA_ROWS = A_HEADS * (2 * HEAD_DIM + ONES_ROWS)
VB_ROWS = B_HEADS * (HEAD_DIM + ONES_ROWS)
FLASH_BK = 512


def _params(sem, flags=None):
    return pltpu.CompilerParams(dimension_semantics=sem, vmem_limit_bytes=VMEM_LIMIT, flags=flags)


def _const_spec(shape):
    nd = len(shape)
    return pl.BlockSpec(shape, lambda *_: (0,) * nd, pipeline_mode=pl.Buffered(1))


def _layer_norm(y, g, b):
    mu = jnp.mean(y, axis=-1, keepdims=True)
    yc = y - mu
    var = jnp.mean(yc * yc, axis=-1, keepdims=True)
    return yc * lax.rsqrt(var + EPS) * g + b


def _dot(a, b):
    return jnp.dot(a, b, preferred_element_type=F32)


def _dot_nt(a, b):
    return lax.dot_general(a, b, (((1,), (1,)), ((), ())), preferred_element_type=F32)


def _inproj_kernel(x_ref, cos_ref, sin_ref, bf_ref,
                   w_qa, w_ka, w_va, w_qb, w_kb, w_vb, w_fb, w_qc, w_kc, w_vc,
                   o_qa, o_ka, o_va, o_qb, o_kb, o_vb, o_lf,
                   o_qc0, o_qc1, o_qc2, o_kc0, o_kc1, o_kc2, o_vc0, o_vc1, o_vc2, perm_ref):
    xb = x_ref[...].astype(BF16)
    cos = cos_ref[...]
    sin = sin_ref[...]
    lane = lax.broadcasted_iota(jnp.int32, cos.shape, 1)
    lower = (lane & (HEAD_DIM - 1)) < HEAD_DIM // 2

    def rope(t):
        sw = jnp.where(lower, pltpu.roll(t, LANES - HEAD_DIM // 2, 1), pltpu.roll(t, HEAD_DIM // 2, 1))
        return t * cos + sw * sin

    def roped(w_ref, outs, width):
        y = _dot(xb, w_ref[...])
        per_out = width // LANES
        for c in range(y.shape[1] // LANES):
            o = outs[c // per_out]
            lo = (c % per_out) * LANES
            o[:, lo:lo + LANES] = rope(y[:, c * LANES:(c + 1) * LANES]).astype(BF16)

    def plain(w_ref, outs, width):
        y = _dot(xb, w_ref[...])
        for c, o in enumerate(outs):
            o[...] = y[:, c * width:(c + 1) * width].astype(BF16)

    def transposed(wt_ref, o, width):
        yt = _dot_nt(wt_ref[...], xb)
        ones = jnp.ones((ONES_ROWS, FLASH_BK), F32)
        for c in range(o.shape[0]):
            cols = slice(c * FLASH_BK, (c + 1) * FLASH_BK)
            parts = []
            for r0 in range(0, yt.shape[0], width):
                parts += [yt[r0:r0 + width, cols], ones]
            o[c] = jnp.concatenate(parts, axis=0).astype(BF16)

    roped(w_qa, (o_qa,), A_W)
    roped(w_ka, (o_ka,), A_W)
    transposed(w_va, o_va, LANES)
    plain(w_qb, (o_qb,), B_W)
    plain(w_kb, (o_kb,), B_W)
    transposed(w_vb, o_vb, HEAD_DIM)
    def dilated(w_ref, outs, slot, use_rope):
        y = _dot(xb, w_ref[...])
        tm = y.shape[0]
        for gi, (_, d) in enumerate(C_PATTERNS):
            for h in range(C_GW // LANES):
                c = gi * (C_GW // LANES) + h
                t = y[:, c * LANES:(c + 1) * LANES]
                t = rope(t) if use_rope else t
                if d == 1:
                    outs[gi][:, h * LANES:(h + 1) * LANES] = t.astype(BF16)
                    continue
                sl = slot + (gi - 1) * 2 + h
                perm_ref[sl] = t
                for r in range(d):
                    lo = r * C_GW + h * LANES
                    outs[gi][:, lo:lo + LANES] = perm_ref.at[sl][pl.ds(r, tm // d, stride=d), :].astype(BF16)

    dilated(w_qc, (o_qc0, o_qc1, o_qc2), 0, True)
    dilated(w_kc, (o_kc0, o_kc1, o_kc2), 4, True)
    dilated(w_vc, (o_vc0, o_vc1, o_vc2), 8, False)
    z =_dot(xb, w_fb[...]) + bf_ref[...]
    o_lf[...] = jnp.minimum(z, 0.0) - jnp.log(1.0 + jnp.exp(-jnp.abs(z)))


def _inproj(x2d, cos, sin, bfp, ws, tm):
    t = x2d.shape[0]
    row = lambda n: pl.BlockSpec((tm, n), lambda i: (i, 0))
    tshape = lambda rows: jax.ShapeDtypeStruct((t // FLASH_BK, rows, FLASH_BK), BF16)
    tspec = lambda rows: pl.BlockSpec((tm // FLASH_BK, rows, FLASH_BK), lambda i: (i, 0, 0))
    rshape = lambda w: jax.ShapeDtypeStruct((t, w), BF16)
    out_shape = [rshape(A_W), rshape(A_W), tshape(VA_ROWS), rshape(B_W), rshape(B_W), tshape(VB_ROWS)]
    out_shape.append(jax.ShapeDtypeStruct((t, LANES), F32))
    out_specs = [row(A_W), row(A_W), tspec(VA_ROWS), row(B_W), row(B_W), tspec(VB_ROWS), row(LANES)]
    for _ in range(3):
        for _, d in C_PATTERNS:
            out_shape.append(jax.ShapeDtypeStruct((t // d, d * C_GW), BF16))
            out_specs.append(pl.BlockSpec((tm // d, d * C_GW), lambda i: (i, 0)))
    in_specs = [row(D_MODEL), row(LANES), row(LANES), _const_spec((1, LANES))]
    in_specs += [_const_spec(w.shape) for w in ws]
    n_perm = 3 * (C_GROUPS - 1) * (C_GW // LANES)
    return pl.pallas_call(
        _inproj_kernel, out_shape=out_shape, grid=(t // tm,), in_specs=in_specs, out_specs=out_specs,
        scratch_shapes=[pltpu.VMEM((n_perm, tm, LANES), F32)],
        compiler_params=_params(("parallel",)), name="inproj")(x2d, cos, sin, bfp, *ws)


def _flash_kernel(forget, bq, *refs):
    bk = FLASH_BK
    if forget:
        q_ref, k_ref, vt_ref, c_ref, o_ref, m_ref, acc_ref, st_a, st_b, bias_ref = refs
    else:
        q_ref, k_ref, vt_ref, lam_ref, gain_ref, o_ref, m_ref, acc_ref, st_a, st_b = refs
    dvp = acc_ref.shape[2]
    dv = dvp - ONES_ROWS
    i = pl.program_id(2)
    q = q_ref[...].astype(F32)
    lane = lax.broadcasted_iota(jnp.int32, (bq, LANES), 1)
    low = lane < HEAD_DIM
    qts = (jnp.where(low, q, 0.0).T.astype(BF16), jnp.where(low, 0.0, q).T.astype(BF16))
    key = lax.broadcasted_iota(jnp.int32, (bk, bk), 0)
    qry = lax.broadcasted_iota(jnp.int32, (bk, bk), 1)
    halves = bq // bk

    if forget:
        @pl.when(i == 0)
        def _bias():
            for s in range(2):
                bias_ref[s] = jnp.broadcast_to(-LOG2E * c_ref[:, s:s + 1], bias_ref.shape[1:])

    m_ref[...] = jnp.full(m_ref.shape, NEG, F32)
    acc_ref[...] = jnp.zeros(acc_ref.shape, F32)

    def scores(j, st_ref, first_half=0):
        off = pl.multiple_of(j * bk, bk)
        kc = k_ref[pl.ds(off, bk), :]
        for s in range(2):
            for h in range(first_half, halves):
                st_ref[s, h] = _dot(kc, qts[s][:, h * bk:(h + 1) * bk])

    def absorb(j, st_ref, first_half=0, diagonal=False):
        off = pl.multiple_of(j * bk, bk)
        for s in range(2):
            for h in range(first_half, halves):
                st = st_ref[s, h]
                if forget:
                    bb = bias_ref[s, pl.ds(off, bk), :]
                    st = st + jnp.concatenate([bb] * (bk // LANES), axis=1)
                if diagonal and h == first_half:
                    st = jnp.where(key <= qry, st, NEG)
                m_prev = m_ref[s, h]
                m_new = jnp.maximum(m_prev, jnp.max(st, axis=0, keepdims=True))
                a = jnp.exp2(m_prev - m_new)
                p = jnp.exp2(st - m_new).astype(BF16)
                m_ref[s, h] = m_new
                vt = vt_ref[j, s * dvp:(s + 1) * dvp, :] if forget else vt_ref[j]
                acc_ref[s, h] = a * acc_ref[s, h] + _dot(vt, p)

    scores(0, st_a)

    def pair(t, c):
        scores(2 * t + 1, st_b)
        absorb(2 * t, st_a)
        scores(2 * t + 2, st_a)
        absorb(2 * t + 1, st_b)
        return c

    lax.fori_loop(0, i, pair, 0)
    scores(2 * i + 1, st_b, 1)
    absorb(2 * i, st_a, 0, True)
    absorb(2 * i + 1, st_b, 1, True)

    for h in range(halves):
        o0 = acc_ref[0, h, :dv, :] / acc_ref[0, h, dv:dv + 1, :]
        o1 = acc_ref[1, h, :dv, :] / acc_ref[1, h, dv:dv + 1, :]
        rows = slice(h * bk, (h + 1) * bk)
        if forget:
            o_ref[rows, :] = jnp.concatenate([o0, o1], axis=0).T.astype(o_ref.dtype)
        else:
            o = (o0 - lam_ref[:, 0:1] * o1).T
            ms = jnp.mean(o * o, axis=-1, keepdims=True)
            o_ref[rows, :] = (o * lax.rsqrt(ms + EPS) * gain_ref[...]).astype(o_ref.dtype)


def _flash(q, k, vt, extra, forget, bq):
    b, s, w = q.shape
    g = w // LANES
    nk = s // FLASH_BK
    dvp = (HEAD_DIM if forget else LANES) + ONES_ROWS
    rows = vt.shape[1] // g
    qspec = pl.BlockSpec((None, bq, LANES), lambda bi, gi, i: (bi, i, gi))
    kspec = pl.BlockSpec((None, s, LANES), lambda bi, gi, i: (bi, 0, gi))
    vspec = pl.BlockSpec((nk, rows, FLASH_BK), lambda bi, gi, i: (bi, gi, 0))
    assert bq == 2 * FLASH_BK
    bk = FLASH_BK
    scratch = [pltpu.VMEM((2, 2, 1, bk), F32), pltpu.VMEM((2, 2, dvp, bk), F32),
               pltpu.VMEM((2, 2, bk, bk), F32), pltpu.VMEM((2, 2, bk, bk), F32)]
    if forget:
        especs = [pl.BlockSpec((None, None, s, 2), lambda bi, gi, i: (bi, gi, 0, 0))]
        scratch.append(pltpu.VMEM((2, s, LANES), F32))
    else:
        especs = [_const_spec((1, LANES)), _const_spec((1, LANES))]
    return pl.pallas_call(
        functools.partial(_flash_kernel, forget, bq),
        out_shape=jax.ShapeDtypeStruct((b, s, w), BF16),
        grid=(b, g, s // bq), in_specs=[qspec, kspec, vspec] + especs, out_specs=qspec,
        scratch_shapes=scratch,
        compiler_params=_params(("parallel", "parallel", "arbitrary")),
        name="fox_attn" if forget else "diff_attn")(q, k, vt, *extra)


def _dilated_kernel(nq, q_ref, kp_ref, kc_ref, vp_ref, vc_ref, o_ref, lse_ref):
    n = C_BLOCK
    pairs = q_ref.shape[1] // LANES
    j = pl.program_id(2)
    kall = jnp.concatenate([kp_ref[...], kc_ref[...]], axis=0)
    vall = jnp.concatenate([vp_ref[...], vc_ref[...]], axis=0)
    qi = lax.broadcasted_iota(jnp.int32, (n, 2 * n), 0)
    kj = lax.broadcasted_iota(jnp.int32, (n, 2 * n), 1)
    dist = n + qi - kj
    window = (dist >= 0) & (dist <= n)
    first_key = jnp.where(j > 0, 0, n)
    lane = lax.broadcasted_iota(jnp.int32, (n, LANES), 1)
    low = lane < HEAD_DIM
    units = [(t, p, s) for t in range(nq) for p in range(pairs) for s in range(2)]
    scores = {}
    for t, p, s in units:
        sl = slice(p * LANES, (p + 1) * LANES)
        q128 = q_ref[t * n:(t + 1) * n, sl]
        zero = jnp.zeros_like(q128)
        qm = jnp.where(low, q128, zero) if s == 0 else jnp.where(low, zero, q128)
        mask = (window & (kj >= first_key)) if t == 0 else window
        scores[t, p, s] = jnp.where(mask, _dot_nt(qm, kall[t * n:(t + 2) * n, sl]), NEG)
    probs = {}
    for u in units:
        m = jnp.max(scores[u], axis=-1, keepdims=True)
        probs[u] = (jnp.exp2(scores[u] - m).astype(BF16), m)
    low_keys = lax.broadcasted_iota(jnp.int32, (2 * n, LANES), 1) < HEAD_DIM
    for t in range(nq):
        for p in range(pairs):
            sl = slice(p * LANES, (p + 1) * LANES)
            v128 = vall[t * n:(t + 2) * n, sl]
            one = jnp.ones_like(v128)
            os, ls = [], []
            for s in range(2):
                vs = jnp.where(low_keys, v128, one) if s == 0 else jnp.where(low_keys, one, v128)
                num = _dot(probs[t, p, s][0], vs)
                den = pltpu.roll(num, HEAD_DIM, 1)
                os.append(num / den)
                ls.append(probs[t, p, s][1] + jnp.log2(den))
            o_ref[t * n:(t + 1) * n, sl] = jnp.where(low, os[0], os[1])
            lse_ref[t * n:(t + 1) * n, sl] = jnp.where(low, ls[0], ls[1])


def _dilated(q, k, v, b, dilation, nq, nr):
    w = C_GW * nr
    n = C_BLOCK
    m_len = q.shape[0] // b
    nb = m_len // (n * nq)
    view = lambda t: t.reshape(b, m_len, dilation * C_GW)
    cur = pl.BlockSpec((None, n * nq, w), lambda bi, r, j: (bi, j, r))
    prev = pl.BlockSpec((None, n, w), lambda bi, r, j: (bi, jnp.maximum(j * nq - 1, 0), r))
    o, lse = pl.pallas_call(
        functools.partial(_dilated_kernel, nq),
        out_shape=[jax.ShapeDtypeStruct((b, m_len, dilation * C_GW), F32)] * 2,
        grid=(b, dilation // nr, nb), in_specs=[cur, prev, cur, prev, cur], out_specs=[cur, cur],
        compiler_params=_params(("parallel", "parallel", "arbitrary")),
        name=f"dilated_d{dilation}")(view(q), view(k), view(k), view(v), view(v))
    return o.reshape(b * m_len, dilation * C_GW), lse.reshape(b * m_len, dilation * C_GW)


def _merge_kernel(x_ref, oa_ref, ob_ref, oc0, oc1, oc2, ls0, ls1, ls2,
                  wg_ref, bg_ref, wa_ref, wb_ref, wc_ref, wo_ref, g_ref, b_ref, o_ref, perm_ref):
    x = x_ref[...]
    xb = x.astype(BF16)
    tm = x.shape[0]

    def token_major(blk_ref, gi, slot):
        d = C_PATTERNS[gi][1]
        if d == 1:
            return blk_ref[...]
        for h in range(C_GW // LANES):
            for r in range(d):
                lo = r * C_GW + h * LANES
                perm_ref.at[slot + h][pl.ds(r, tm // d, stride=d), :] = blk_ref[:, lo:lo + LANES]
        return jnp.concatenate([perm_ref[slot + h] for h in range(C_GW // LANES)], axis=1)

    o0, o1, o2 = (token_major(r, gi, 2 * gi) for gi, r in enumerate((oc0, oc1, oc2)))
    l0, l1, l2 = (token_major(r, gi, 2 * C_GROUPS + 2 * gi) for gi, r in enumerate((ls0, ls1, ls2)))
    m = jnp.maximum(jnp.maximum(l0, l1), l2)
    e0, e1, e2 = jnp.exp2(l0 - m), jnp.exp2(l1 - m), jnp.exp2(l2 - m)
    oc = (e0 * o0 + e1 * o1 + e2 * o2) / (e0 + e1 + e2)
    branches = ((oa_ref[...], wa_ref), (ob_ref[...], wb_ref), (oc.astype(BF16), wc_ref))
    merged = None
    for k, (ob, w_ref) in enumerate(branches):
        sl = slice(k * D_MODEL, (k + 1) * D_MODEL)
        z = _dot(xb, wg_ref[:, sl]) + bg_ref[:, sl]
        gate = 1.0 / (1.0 + jnp.exp(-z))
        term = gate * _dot(ob, w_ref[...])
        merged = term if merged is None else merged + term
    h = _dot(merged.astype(BF16), wo_ref[...])
    o_ref[...] = _layer_norm(DN_ALPHA * x + h, g_ref[...], b_ref[...])


def _merge(x2d, oa, ob, ocs, lses, wg, bg, wa, wb, wc, wo, g, bta, tm):
    t = x2d.shape[0]
    row = lambda n: pl.BlockSpec((tm, n), lambda i: (i, 0))
    consts = (wg, bg, wa, wb, wc, wo, g, bta)
    cspecs = [pl.BlockSpec((tm // d, d * C_GW), lambda i: (i, 0)) for _, d in C_PATTERNS]
    in_specs = [row(D_MODEL), row(A_W), row(B_W)] + cspecs * 2 + [_const_spec(c.shape) for c in consts]
    return pl.pallas_call(
        _merge_kernel, out_shape=jax.ShapeDtypeStruct((t, D_MODEL), F32), grid=(t // tm,),
        in_specs=in_specs, out_specs=row(D_MODEL),
        scratch_shapes=[pltpu.VMEM((4 * C_GROUPS, tm, LANES), F32)],
        compiler_params=_params(("parallel",)), name="merge")(x2d, oa, ob, *ocs, *lses, *consts)


def _memkv_kernel(m_ref, wk_ref, wv_ref, k_ref, v_ref):
    mb = m_ref[...].astype(BF16)
    k_ref[...] = _dot(mb, wk_ref[...]).astype(BF16)
    v_ref[...] = _dot(mb, wv_ref[...]).astype(BF16)


def _memkv(mem2d, wk, wv, tm):
    t = mem2d.shape[0]
    row = pl.BlockSpec((tm, D_MODEL), lambda i: (i, 0))
    return pl.pallas_call(
        _memkv_kernel, out_shape=[jax.ShapeDtypeStruct((t, D_MODEL), BF16)] * 2, grid=(t // tm,),
        in_specs=[row, _const_spec(wk.shape), _const_spec(wv.shape)], out_specs=[row, row],
        compiler_params=_params(("parallel",)), name="memkv")(mem2d, wk, wv)


def _xattn_kernel(x_ref, k_ref, v_ref, wq_ref, wo_ref, g_ref, b_ref, o_ref):
    x = x_ref[...]
    q = _dot(x.astype(BF16), wq_ref[...]).astype(BF16)
    heads = []
    for h in range(X_HEADS):
        sl = slice(h * X_HEAD_DIM, (h + 1) * X_HEAD_DIM)
        st = _dot_nt(q[:, sl], k_ref[:, sl])
        m = jnp.max(st, axis=-1, keepdims=True)
        e = jnp.exp(st - m)
        l = jnp.sum(e, axis=-1, keepdims=True)
        heads.append((_dot(e.astype(BF16), v_ref[:, sl]) / l).astype(BF16))
    o = jnp.concatenate(heads, axis=-1)
    h_out = _dot(o, wo_ref[...])
    o_ref[...] = _layer_norm(DN_ALPHA * x + h_out, g_ref[...], b_ref[...])


def _xattn(x3d, k3d, v3d, wq, wo, g, bta, tm):
    b, s, d = x3d.shape
    xspec = pl.BlockSpec((None, tm, d), lambda bi, i: (bi, i, 0))
    mspec = pl.BlockSpec((None, MEM_LEN, d), lambda bi, i: (bi, 0, 0))
    consts = (wq, wo, g, bta)
    return pl.pallas_call(
        _xattn_kernel, out_shape=jax.ShapeDtypeStruct((b, s, d), F32), grid=(b, s // tm),
        in_specs=[xspec, mspec, mspec] + [_const_spec(c.shape) for c in consts], out_specs=xspec,
        compiler_params=_params(("parallel", "parallel")), name="xattn")(x3d, k3d, v3d, *consts)


MOE_EPB = 2
ROUTE_LANE0 = N_GROUPS


def _moe_kernel(x_ref, wrh_ref, wrl_ref, br_ref, wg_ref, wu_ref, wd_ref, g_ref, b_ref, o_ref,
                xb_ref, gate_ref):
    tm = x_ref.shape[0]
    lane = lax.broadcasted_iota(jnp.int32, (tm, LANES), 1)
    x = x_ref[...]
    lanef = lane.astype(F32)
    xh = x.astype(BF16)
    xl = (x - xh.astype(F32)).astype(BF16)
    xb_ref[...] = xh
    hh_hl = _dot(xh, jnp.concatenate([wrh_ref[...], wrl_ref[...]], axis=1))
    logits = (hh_hl[:, :LANES] + _dot(xl, wrh_ref[...]) + hh_hl[:, LANES:]) + br_ref[...]
    gl = jnp.where(lane < N_GROUPS, logits, NEG)
    gmax = jnp.max(gl, axis=-1, keepdims=True)
    g_gate = 1.0 / jnp.sum(jnp.exp(gl - gmax), axis=-1, keepdims=True)
    g_idx = jnp.min(jnp.where(gl == gmax, lanef, float(LANES)), axis=-1, keepdims=True)
    e_group = ((lane - ROUTE_LANE0) >> 2).astype(F32)
    sel = (lane >= ROUTE_LANE0) & (e_group == g_idx)
    el = jnp.where(sel, logits, NEG)
    top1 = jnp.max(el, axis=-1, keepdims=True)
    i1 = jnp.min(jnp.where(el == top1, lanef, float(LANES)), axis=-1, keepdims=True)
    el2 = jnp.where(lanef == i1, NEG, el)
    top2 = jnp.max(el2, axis=-1, keepdims=True)
    i2 = jnp.min(jnp.where(el2 == top2, lanef, float(LANES)), axis=-1, keepdims=True)
    r = jnp.exp(top2 - top1)
    w1 = 1.0 / (1.0 + r)
    w2 = r * w1
    gate_ref[...] = jnp.where(lanef == i1, w1, jnp.where(lanef == i2, w2, 0.0)) * g_gate
    o_ref[...] = jnp.zeros_like(o_ref)

    def experts(eb, carry):
        xb = xb_ref[...]
        gates = gate_ref[...]
        for k in range(MOE_EPB):
            e = eb * MOE_EPB + k
            gcol = jnp.sum(jnp.where(lane == ROUTE_LANE0 + e, gates, 0.0), axis=-1, keepdims=True)
            a = _dot(xb, wg_ref[e])
            u = _dot(xb, wu_ref[e])
            h = (a / (1.0 + jnp.exp(-a))) * u * gcol
            o_ref[...] += _dot(h.astype(BF16), wd_ref[e])
        return carry

    lax.fori_loop(0, N_EXPERTS // MOE_EPB, experts, 0)
    o_ref[...] = _layer_norm(DN_ALPHA * x_ref[...] + o_ref[...], g_ref[...], b_ref[...])


def _moe(x2d, wrh, wrl, br, wg, wu, wd, g, bta, tm):
    t = x2d.shape[0]
    row = pl.BlockSpec((tm, D_MODEL), lambda i: (i, 0))
    consts = (wrh, wrl, br, wg, wu, wd, g, bta)
    return pl.pallas_call(
        _moe_kernel, out_shape=jax.ShapeDtypeStruct((t, D_MODEL), F32),
        grid=(t // tm,), in_specs=[row] + [_const_spec(c.shape) for c in consts], out_specs=row,
        scratch_shapes=[pltpu.VMEM((tm, D_MODEL), BF16), pltpu.VMEM((tm, LANES), F32)],
        compiler_params=_params(("parallel",)), name="moe")(x2d, *consts)


def _pad_cols(w, n):
    return jnp.pad(w, ((0, 0), (0, n - w.shape[1])))


def kernel(x, mem, positions, w_in, b_forget, diff_lambda, diff_subln, w_branch_a, w_branch_b, w_branch_c,
           w_gate, b_gate, w_out, w_xq, w_xk, w_xv, w_xo, w_route_group, b_route_group, w_route_expert,
           b_route_expert, w_expert_gate, w_expert_up, w_expert_down, ln_g, ln_b):
    b, s, d = x.shape
    t = b * s
    depth = w_in.shape[0]
    scale = HEAD_DIM ** -0.5

    half = HEAD_DIM // 2
    inv = ROPE_THETA ** (-jnp.arange(half, dtype=F32) / half)
    ang = positions.astype(F32).reshape(t, 1) * inv
    cos, sin = jnp.cos(ang), jnp.sin(ang)
    cos_t = jnp.concatenate([cos, cos, cos, cos], axis=-1)
    sin_t = jnp.concatenate([-sin, sin, -sin, sin], axis=-1)

    mem2d = mem.reshape(b * MEM_LEN, d)
    x2d = x.reshape(t, d)
    row = lambda v: v.reshape(1, -1).astype(F32)
    for l in range(depth):
        lam_init = 0.8 - 0.6 * math.exp(-0.3 * l)
        offs = [0, A_W, 2 * A_W, 3 * A_W, 3 * A_W + B_W, 3 * A_W + 2 * B_W, 3 * A_W + 3 * B_W]
        offs.append(offs[-1] + B_HEADS)
        offs += [offs[-1] + C_W, offs[-1] + 2 * C_W, offs[-1] + 3 * C_W]
        wl = w_in[l]
        seg = [wl[:, offs[k]:offs[k + 1]] for k in range(10)]
        qa, ka, va, qb, kb, vb, fb, qc, kc, vc = seg
        ws = [(qa * (scale * LOG2E)), ka, va.T, (qb * (scale * LOG2E)), kb, vb.T, _pad_cols(fb, LANES),
              (qc * (scale * LOG2E)), kc, vc]
        ws = [w.astype(BF16) for w in ws]
        bfp = _pad_cols(row(b_forget[l]), LANES)

        outs = _inproj(x2d, cos_t, sin_t, bfp, ws, tm=512)
        o_qa, o_ka, o_va, o_qb, o_kb, o_vb, o_lf = outs[:7]
        o_qc, o_kc, o_vc = outs[7:10], outs[10:13], outs[13:16]

        lf = diff_lambda[l].astype(F32)
        lam = jnp.exp(jnp.dot(lf[0], lf[1])) - jnp.exp(jnp.dot(lf[2], lf[3])) + lam_init
        lam_row = jnp.full((1, LANES), lam, F32)
        gain = row(diff_subln[l]) * (1.0 - lam_init)
        r3 = lambda v: v.reshape(b, s, v.shape[-1])
        oa = _flash(r3(o_qa), r3(o_ka), o_va, (lam_row, gain), forget=False, bq=2 * FLASH_BK)
        c = jnp.cumsum(o_lf.reshape(b, s, LANES)[:, :, :B_HEADS], axis=1)
        c_t = jnp.transpose(c.reshape(b, s, B_HEADS // 2, 2), (0, 2, 1, 3))
        ob = _flash(r3(o_qb), r3(o_kb), o_vb, (c_t,), forget=True, bq=2 * FLASH_BK)
        ocs, lses = [], []
        for gi, (_, dil) in enumerate(C_PATTERNS):
            nq = min(4, s // (dil * C_BLOCK))
            o_g, lse_g = _dilated(o_qc[gi], o_kc[gi], o_vc[gi], b, dil, nq, nr=min(dil, 4 // nq))
            ocs.append(o_g)
            lses.append(lse_g)
        x2d = _merge(x2d, oa.reshape(t, A_W), ob.reshape(t, B_W), ocs, lses,
                     w_gate[l].astype(BF16), row(b_gate[l]), w_branch_a[l].astype(BF16),
                     w_branch_b[l].astype(BF16), w_branch_c[l].astype(BF16), w_out[l].astype(BF16),
                     row(ln_g[l, 0]), row(ln_b[l, 0]), tm=512)
        xscale = X_HEAD_DIM ** -0.5
        mk, mv = _memkv(mem2d, w_xk[l].astype(BF16), w_xv[l].astype(BF16), tm=MEM_LEN)
        x2d = _xattn(x2d.reshape(b, s, d), mk.reshape(b, MEM_LEN, d), mv.reshape(b, MEM_LEN, d),
                     (w_xq[l] * xscale).astype(BF16), w_xo[l].astype(BF16),
                     row(ln_g[l, 1]), row(ln_b[l, 1]), tm=512).reshape(t, d)
        wr = jnp.concatenate([w_route_group[l]] + [w_route_expert[l, gi] for gi in range(N_GROUPS)], axis=1)
        wr = _pad_cols(wr.astype(F32), LANES)
        wrh = wr.astype(BF16)
        wrl = (wr - wrh.astype(F32)).astype(BF16)
        br = _pad_cols(row(jnp.concatenate([b_route_group[l], b_route_expert[l].reshape(-1)])), LANES)
        ne = (N_EXPERTS,)
        x2d = _moe(x2d, wrh, wrl, br,
                   w_expert_gate[l].reshape(ne + w_expert_gate.shape[3:]).astype(BF16),
                   w_expert_up[l].reshape(ne + w_expert_up.shape[3:]).astype(BF16),
                   w_expert_down[l].reshape(ne + w_expert_down.shape[3:]).astype(BF16),
                   row(ln_g[l, 2]), row(ln_b[l, 2]), tm=1024)
    return x2d.reshape(b, s, d)
```

```python
import functools
import math

import jax
import jax.numpy as jnp
from jax import lax
from jax.experimental import pallas as pl
from jax.experimental.pallas import tpu as pltpu

F32 = jnp.float32
BF16 = jnp.bfloat16

D_MODEL = 1024
HEAD_DIM = 64
ROPE_THETA = 10000.0
EPS = 1e-5
MEM_LEN = 256
A_HEADS = 4
B_HEADS = 8
C_PATTERNS = ((128, 1), (512, 4), (2048, 16))
C_GROUPS = 3
C_SLOTS = 4
C_BLOCK = 128
X_HEADS = 4
X_HEAD_DIM = D_MODEL // X_HEADS
N_GROUPS = 4
EXPERTS_PER_GROUP = 4
N_EXPERTS = N_GROUPS * EXPERTS_PER_GROUP
D_EXPERT = D_MODEL // 4
DEPTH = 2
DN_ALPHA = (2 * DEPTH) ** 0.25

LANES = 128
NEG = -1e30
LOG2E = math.log2(math.e)
VMEM_LIMIT = 56 * 1024 * 1024

A_W = A_HEADS * 2 * HEAD_DIM
B_W = B_HEADS * HEAD_DIM
C_GW = C_SLOTS * HEAD_DIM
C_W = C_GROUPS * C_GW
ONES_ROWS = 16
VA_ROWS = A_HEADS * (2 * HEAD_DIM + ONES_ROWS)
VB_ROWS = B_HEADS * (HEAD_DIM + ONES_ROWS)
FLASH_BK = 512


def _params(sem, flags=None):
    return pltpu.CompilerParams(dimension_semantics=sem, vmem_limit_bytes=VMEM_LIMIT, flags=flags)


def _const_spec(shape):
    nd = len(shape)
    return pl.BlockSpec(shape, lambda *_: (0,) * nd, pipeline_mode=pl.Buffered(1))


def _layer_norm(y, g, b):
    mu = jnp.mean(y, axis=-1, keepdims=True)
    yc = y - mu
    var = jnp.mean(yc * yc, axis=-1, keepdims=True)
    return yc * lax.rsqrt(var + EPS) * g + b


def _dot(a, b):
    return jnp.dot(a, b, preferred_element_type=F32)


def _dot_nt(a, b):
    return lax.dot_general(a, b, (((1,), (1,)), ((), ())), preferred_element_type=F32)


def _inproj_kernel(x_ref, cos_ref, sin_ref, bf_ref,
                   w_qa, w_ka, w_va, w_qb, w_kb, w_vb, w_fb, w_qc, w_kc, w_vc,
                   o_qa, o_ka, o_va, o_qb, o_kb, o_vb, o_lf,
                   o_qc0, o_qc1, o_qc2, o_kc0, o_kc1, o_kc2, o_vc0, o_vc1, o_vc2, perm_ref):
    xb = x_ref[...].astype(BF16)
    cos = cos_ref[...]
    sin = sin_ref[...]
    lane = lax.broadcasted_iota(jnp.int32, cos.shape, 1)
    lower = (lane & (HEAD_DIM - 1)) < HEAD_DIM // 2

    def rope(t):
        sw = jnp.where(lower, pltpu.roll(t, LANES - HEAD_DIM // 2, 1), pltpu.roll(t, HEAD_DIM // 2, 1))
        return t * cos + sw * sin

    def roped(w_ref, outs, width):
        y = _dot(xb, w_ref[...])
        per_out = width // LANES
        for c in range(y.shape[1] // LANES):
            o = outs[c // per_out]
            lo = (c % per_out) * LANES
            o[:, lo:lo + LANES] = rope(y[:, c * LANES:(c + 1) * LANES]).astype(BF16)

    def plain(w_ref, outs, width):
        y = _dot(xb, w_ref[...])
        for c, o in enumerate(outs):
            o[...] = y[:, c * width:(c + 1) * width].astype(BF16)

    def transposed(wt_ref, o, width):
        yt = _dot_nt(wt_ref[...], xb)
        ones = jnp.ones((ONES_ROWS, FLASH_BK), F32)
        for c in range(o.shape[0]):
            cols = slice(c * FLASH_BK, (c + 1) * FLASH_BK)
            parts = []
            for r0 in range(0, yt.shape[0], width):
                parts += [yt[r0:r0 + width, cols], ones]
            o[c] = jnp.concatenate(parts, axis=0).astype(BF16)

    roped(w_qa, (o_qa,), A_W)
    roped(w_ka, (o_ka,), A_W)
    transposed(w_va, o_va, LANES)
    plain(w_qb, (o_qb,), B_W)
    plain(w_kb, (o_kb,), B_W)
    transposed(w_vb, o_vb, HEAD_DIM)
    def dilated(w_ref, outs, slot, use_rope):
        y = _dot(xb, w_ref[...])
        tm = y.shape[0]
        for gi, (_, d) in enumerate(C_PATTERNS):
            for h in range(C_GW // LANES):
                c = gi * (C_GW // LANES) + h
                t = y[:, c * LANES:(c + 1) * LANES]
                t = rope(t) if use_rope else t
                if d == 1:
                    outs[gi][:, h * LANES:(h + 1) * LANES] = t.astype(BF16)
                    continue
                sl = slot + (gi - 1) * 2 + h
                perm_ref[sl] = t
                for r in range(d):
                    lo = r * C_GW + h * LANES
                    outs[gi][:, lo:lo + LANES] = perm_ref.at[sl][pl.ds(r, tm // d, stride=d), :].astype(BF16)

    dilated(w_qc, (o_qc0, o_qc1, o_qc2), 0, True)
    dilated(w_kc, (o_kc0, o_kc1, o_kc2), 4, True)
    dilated(w_vc, (o_vc0, o_vc1, o_vc2), 8, False)
    z =_dot(xb, w_fb[...]) + bf_ref[...]
    o_lf[...] = jnp.minimum(z, 0.0) - jnp.log(1.0 + jnp.exp(-jnp.abs(z)))


def _inproj(x2d, cos, sin, bfp, ws, tm):
    t = x2d.shape[0]
    row = lambda n: pl.BlockSpec((tm, n), lambda i: (i, 0))
    tshape = lambda rows: jax.ShapeDtypeStruct((t // FLASH_BK, rows, FLASH_BK), BF16)
    tspec = lambda rows: pl.BlockSpec((tm // FLASH_BK, rows, FLASH_BK), lambda i: (i, 0, 0))
    rshape = lambda w: jax.ShapeDtypeStruct((t, w), BF16)
    out_shape = [rshape(A_W), rshape(A_W), tshape(VA_ROWS), rshape(B_W), rshape(B_W), tshape(VB_ROWS)]
    out_shape.append(jax.ShapeDtypeStruct((t, LANES), F32))
    out_specs = [row(A_W), row(A_W), tspec(VA_ROWS), row(B_W), row(B_W), tspec(VB_ROWS), row(LANES)]
    for _ in range(3):
        for _, d in C_PATTERNS:
            out_shape.append(jax.ShapeDtypeStruct((t // d, d * C_GW), BF16))
            out_specs.append(pl.BlockSpec((tm // d, d * C_GW), lambda i: (i, 0)))
    in_specs = [row(D_MODEL), row(LANES), row(LANES), _const_spec((1, LANES))]
    in_specs += [_const_spec(w.shape) for w in ws]
    n_perm = 3 * (C_GROUPS - 1) * (C_GW // LANES)
    return pl.pallas_call(
        _inproj_kernel, out_shape=out_shape, grid=(t // tm,), in_specs=in_specs, out_specs=out_specs,
        scratch_shapes=[pltpu.VMEM((n_perm, tm, LANES), F32)],
        compiler_params=_params(("parallel",)), name="inproj")(x2d, cos, sin, bfp, *ws)


BIAS_PARTS = 3


def _split3(v):
    p0 = v.astype(BF16)
    r1 = v - p0.astype(F32)
    p1 = r1.astype(BF16)
    p2 = (r1 - p1.astype(F32)).astype(BF16)
    return p0, p1, p2


def _fox_keys_kernel(k_ref, lf_ref, tri_ref, sel_ref, o_ref, carry_ref):
    @pl.when(pl.program_id(1) == 0)
    def _reset():
        carry_ref[...] = jnp.zeros_like(carry_ref)

    tk = lf_ref.shape[0]
    tri = tri_ref[...]
    c = carry_ref[...] + sum(_dot(tri, part) for part in _split3(lf_ref[...]))
    carry_ref[...] = c[tk - 1:tk, :]
    extra = _dot(jnp.concatenate(_split3(-LOG2E * c), axis=1), sel_ref[...])
    low = lax.broadcasted_iota(jnp.int32, (tk, LANES), 1) < HEAD_DIM
    for h in range(B_HEADS):
        pair = k_ref[:, (h // 2) * LANES:(h // 2 + 1) * LANES].astype(F32)
        own = low if h % 2 == 0 else jnp.logical_not(low)
        sl = slice(h * LANES, (h + 1) * LANES)
        o_ref[:, sl] = jnp.where(own, pair, extra[:, sl]).astype(BF16)


def _fox_keys(k3d, lf3d, tk):
    b, s, w = k3d.shape
    tri = jnp.tril(jnp.ones((tk, tk), BF16))
    rows = jnp.arange(BIAS_PARTS * LANES)
    part, head = rows // LANES, rows % LANES
    col = head * LANES + jnp.where(head % 2 == 0, HEAD_DIM, 0) + part
    sel = jnp.zeros((BIAS_PARTS * LANES, B_HEADS * LANES), BF16)
    sel = sel.at[rows, jnp.where(head < B_HEADS, col, 0)].add(jnp.where(head < B_HEADS, 1, 0).astype(BF16))
    blk = lambda n: pl.BlockSpec((None, tk, n), lambda bi, j: (bi, j, 0))
    return pl.pallas_call(
        _fox_keys_kernel, out_shape=jax.ShapeDtypeStruct((b, s, B_HEADS * LANES), BF16),
        grid=(b, s // tk), in_specs=[blk(w), blk(LANES), _const_spec(tri.shape), _const_spec(sel.shape)],
        out_specs=blk(B_HEADS * LANES), scratch_shapes=[pltpu.VMEM((1, LANES), F32)],
        compiler_params=_params(("parallel", "arbitrary")), name="fox_keys")(k3d, lf3d, tri, sel)


def _flash_kernel(forget, bq, *refs):
    bk = FLASH_BK
    if forget:
        q_ref, k_ref, vt_ref, o_ref, m_ref, acc_ref, st_a, st_b = refs
    else:
        q_ref, k_ref, vt_ref, lam_ref, gain_ref, o_ref, m_ref, acc_ref, st_a, st_b = refs
    dvp = acc_ref.shape[2]
    dv = dvp - ONES_ROWS
    i = pl.program_id(2)
    q = q_ref[...].astype(F32)
    lane = lax.broadcasted_iota(jnp.int32, (bq, LANES), 1)
    low = lane < HEAD_DIM
    if forget:
        fill = (jnp.where(lane < HEAD_DIM + BIAS_PARTS, 1.0, 0.0), jnp.where(lane < BIAS_PARTS, 1.0, 0.0))
    else:
        fill = (0.0, 0.0)
    qts = (jnp.where(low, q, fill[0]).T.astype(BF16), jnp.where(low, fill[1], q).T.astype(BF16))
    key = lax.broadcasted_iota(jnp.int32, (bk, bk), 0)
    qry = lax.broadcasted_iota(jnp.int32, (bk, bk), 1)
    halves = bq // bk

    m_ref[...] = jnp.full(m_ref.shape, NEG, F32)
    acc_ref[...] = jnp.zeros(acc_ref.shape, F32)

    def scores(j, st_ref, first_half=0):
        off = pl.multiple_of(j * bk, bk)
        for s in range(2):
            kc = k_ref[pl.ds(off, bk), s * LANES:(s + 1) * LANES] if forget else k_ref[pl.ds(off, bk), :]
            for h in range(first_half, halves):
                st_ref[s, h] = _dot(kc, qts[s][:, h * bk:(h + 1) * bk])

    def absorb(j, st_ref, first_half=0, diagonal=False):
        for s in range(2):
            for h in range(first_half, halves):
                st = st_ref[s, h]
                if diagonal and h == first_half:
                    st = jnp.where(key <= qry, st, NEG)
                m_prev = m_ref[s, h]
                m_new = jnp.maximum(m_prev, jnp.max(st, axis=0, keepdims=True))
                a = jnp.exp2(m_prev - m_new)
                p = jnp.exp2(st - m_new).astype(BF16)
                m_ref[s, h] = m_new
                vt = vt_ref[j, s * dvp:(s + 1) * dvp, :] if forget else vt_ref[j]
                acc_ref[s, h] = a * acc_ref[s, h] + _dot(vt, p)

    scores(0, st_a)

    def pair(t, c):
        scores(2 * t + 1, st_b)
        absorb(2 * t, st_a)
        scores(2 * t + 2, st_a)
        absorb(2 * t + 1, st_b)
        return c

    lax.fori_loop(0, i, pair, 0)
    scores(2 * i + 1, st_b, 1)
    absorb(2 * i, st_a, 0, True)
    absorb(2 * i + 1, st_b, 1, True)

    for h in range(halves):
        o0 = acc_ref[0, h, :dv, :] / acc_ref[0, h, dv:dv + 1, :]
        o1 = acc_ref[1, h, :dv, :] / acc_ref[1, h, dv:dv + 1, :]
        rows = slice(h * bk, (h + 1) * bk)
        if forget:
            o_ref[rows, :] = jnp.concatenate([o0, o1], axis=0).T.astype(o_ref.dtype)
        else:
            o = (o0 - lam_ref[:, 0:1] * o1).T
            ms = jnp.mean(o * o, axis=-1, keepdims=True)
            o_ref[rows, :] = (o * lax.rsqrt(ms + EPS) * gain_ref[...]).astype(o_ref.dtype)


def _flash(q, k, vt, extra, forget, bq):
    b, s, w = q.shape
    g = w // LANES
    nk = s // FLASH_BK
    dvp = (HEAD_DIM if forget else LANES) + ONES_ROWS
    rows = vt.shape[1] // g
    kw = k.shape[2] // g
    qspec = pl.BlockSpec((None, bq, LANES), lambda bi, gi, i: (bi, i, gi))
    kspec = pl.BlockSpec((None, s, kw), lambda bi, gi, i: (bi, 0, gi))
    vspec = pl.BlockSpec((nk, rows, FLASH_BK), lambda bi, gi, i: (bi, gi, 0))
    assert bq == 2 * FLASH_BK
    bk = FLASH_BK
    scratch = [pltpu.VMEM((2, 2, 1, bk), F32), pltpu.VMEM((2, 2, dvp, bk), F32),
               pltpu.VMEM((2, 2, bk, bk), F32), pltpu.VMEM((2, 2, bk, bk), F32)]
    especs = [] if forget else [_const_spec((1, LANES)), _const_spec((1, LANES))]
    return pl.pallas_call(
        functools.partial(_flash_kernel, forget, bq),
        out_shape=jax.ShapeDtypeStruct((b, s, w), BF16),
        grid=(b, g, s // bq), in_specs=[qspec, kspec, vspec] + especs, out_specs=qspec,
        scratch_shapes=scratch,
        compiler_params=_params(("parallel", "parallel", "arbitrary")),
        name="fox_attn" if forget else "diff_attn")(q, k, vt, *extra)


def _dilated_kernel(nq, q_ref, kp_ref, kc_ref, vp_ref, vc_ref, o_ref, lse_ref):
    n = C_BLOCK
    pairs = q_ref.shape[1] // LANES
    j = pl.program_id(2)
    kall = jnp.concatenate([kp_ref[...], kc_ref[...]], axis=0)
    vall = jnp.concatenate([vp_ref[...], vc_ref[...]], axis=0)
    qi = lax.broadcasted_iota(jnp.int32, (n, 2 * n), 0)
    kj = lax.broadcasted_iota(jnp.int32, (n, 2 * n), 1)
    dist = n + qi - kj
    window = (dist >= 0) & (dist <= n)
    first_key = jnp.where(j > 0, 0, n)
    lane = lax.broadcasted_iota(jnp.int32, (n, LANES), 1)
    low = lane < HEAD_DIM
    units = [(t, p, s) for t in range(nq) for p in range(pairs) for s in range(2)]
    scores = {}
    for t, p, s in units:
        sl = slice(p * LANES, (p + 1) * LANES)
        q128 = q_ref[t * n:(t + 1) * n, sl]
        zero = jnp.zeros_like(q128)
        qm = jnp.where(low, q128, zero) if s == 0 else jnp.where(low, zero, q128)
        mask = (window & (kj >= first_key)) if t == 0 else window
        scores[t, p, s] = jnp.where(mask, _dot_nt(qm, kall[t * n:(t + 2) * n, sl]), NEG)
    probs = {}
    for u in units:
        m = jnp.max(scores[u], axis=-1, keepdims=True)
        probs[u] = (jnp.exp2(scores[u] - m).astype(BF16), m)
    low_keys = lax.broadcasted_iota(jnp.int32, (2 * n, LANES), 1) < HEAD_DIM
    for t in range(nq):
        for p in range(pairs):
            sl = slice(p * LANES, (p + 1) * LANES)
            v128 = vall[t * n:(t + 2) * n, sl]
            one = jnp.ones_like(v128)
            os, ls = [], []
            for s in range(2):
                vs = jnp.where(low_keys, v128, one) if s == 0 else jnp.where(low_keys, one, v128)
                num = _dot(probs[t, p, s][0], vs)
                den = pltpu.roll(num, HEAD_DIM, 1)
                os.append(num / den)
                ls.append(probs[t, p, s][1] + jnp.log2(den))
            o_ref[t * n:(t + 1) * n, sl] = jnp.where(low, os[0], os[1])
            lse_ref[t * n:(t + 1) * n, sl] = jnp.where(low, ls[0], ls[1])


def _dilated(q, k, v, b, dilation, nq, nr):
    w = C_GW * nr
    n = C_BLOCK
    m_len = q.shape[0] // b
    nb = m_len // (n * nq)
    view = lambda t: t.reshape(b, m_len, dilation * C_GW)
    cur = pl.BlockSpec((None, n * nq, w), lambda bi, r, j: (bi, j, r))
    prev = pl.BlockSpec((None, n, w), lambda bi, r, j: (bi, jnp.maximum(j * nq - 1, 0), r))
    o, lse = pl.pallas_call(
        functools.partial(_dilated_kernel, nq),
        out_shape=[jax.ShapeDtypeStruct((b, m_len, dilation * C_GW), F32)] * 2,
        grid=(b, dilation // nr, nb), in_specs=[cur, prev, cur, prev, cur], out_specs=[cur, cur],
        compiler_params=_params(("parallel", "parallel", "arbitrary")),
        name=f"dilated_d{dilation}")(view(q), view(k), view(k), view(v), view(v))
    return o.reshape(b * m_len, dilation * C_GW), lse.reshape(b * m_len, dilation * C_GW)


def _merge_kernel(x_ref, oa_ref, ob_ref, oc0, oc1, oc2, ls0, ls1, ls2,
                  wg_ref, bg_ref, wa_ref, wb_ref, wc_ref, wo_ref, g_ref, b_ref, o_ref, perm_ref):
    x = x_ref[...]
    xb = x.astype(BF16)
    tm = x.shape[0]

    def token_major(blk_ref, gi, slot):
        d = C_PATTERNS[gi][1]
        if d == 1:
            return blk_ref[...]
        for h in range(C_GW // LANES):
            for r in range(d):
                lo = r * C_GW + h * LANES
                perm_ref.at[slot + h][pl.ds(r, tm // d, stride=d), :] = blk_ref[:, lo:lo + LANES]
        return jnp.concatenate([perm_ref[slot + h] for h in range(C_GW // LANES)], axis=1)

    o0, o1, o2 = (token_major(r, gi, 2 * gi) for gi, r in enumerate((oc0, oc1, oc2)))
    l0, l1, l2 = (token_major(r, gi, 2 * C_GROUPS + 2 * gi) for gi, r in enumerate((ls0, ls1, ls2)))
    m = jnp.maximum(jnp.maximum(l0, l1), l2)
    e0, e1, e2 = jnp.exp2(l0 - m), jnp.exp2(l1 - m), jnp.exp2(l2 - m)
    oc = (e0 * o0 + e1 * o1 + e2 * o2) / (e0 + e1 + e2)
    branches = ((oa_ref[...], wa_ref), (ob_ref[...], wb_ref), (oc.astype(BF16), wc_ref))
    merged = None
    for k, (ob, w_ref) in enumerate(branches):
        sl = slice(k * D_MODEL, (k + 1) * D_MODEL)
        z = _dot(xb, wg_ref[:, sl]) + bg_ref[:, sl]
        gate = 1.0 / (1.0 + jnp.exp(-z))
        term = gate * _dot(ob, w_ref[...])
        merged = term if merged is None else merged + term
    h = _dot(merged.astype(BF16), wo_ref[...])
    o_ref[...] = _layer_norm(DN_ALPHA * x + h, g_ref[...], b_ref[...])


def _merge(x2d, oa, ob, ocs, lses, wg, bg, wa, wb, wc, wo, g, bta, tm):
    t = x2d.shape[0]
    row = lambda n: pl.BlockSpec((tm, n), lambda i: (i, 0))
    consts = (wg, bg, wa, wb, wc, wo, g, bta)
    cspecs = [pl.BlockSpec((tm // d, d * C_GW), lambda i: (i, 0)) for _, d in C_PATTERNS]
    in_specs = [row(D_MODEL), row(A_W), row(B_W)] + cspecs * 2 + [_const_spec(c.shape) for c in consts]
    return pl.pallas_call(
        _merge_kernel, out_shape=jax.ShapeDtypeStruct((t, D_MODEL), F32), grid=(t // tm,),
        in_specs=in_specs, out_specs=row(D_MODEL),
        scratch_shapes=[pltpu.VMEM((4 * C_GROUPS, tm, LANES), F32)],
        compiler_params=_params(("parallel",)), name="merge")(x2d, oa, ob, *ocs, *lses, *consts)


def _memkv_kernel(m_ref, wk_ref, wv_ref, k_ref, v_ref):
    mb = m_ref[...].astype(BF16)
    k_ref[...] = _dot(mb, wk_ref[...]).astype(BF16)
    v_ref[...] = _dot(mb, wv_ref[...]).astype(BF16)


def _memkv(mem2d, wk, wv, tm):
    t = mem2d.shape[0]
    row = pl.BlockSpec((tm, D_MODEL), lambda i: (i, 0))
    return pl.pallas_call(
        _memkv_kernel, out_shape=[jax.ShapeDtypeStruct((t, D_MODEL), BF16)] * 2, grid=(t // tm,),
        in_specs=[row, _const_spec(wk.shape), _const_spec(wv.shape)], out_specs=[row, row],
        compiler_params=_params(("parallel",)), name="memkv")(mem2d, wk, wv)


def _xattn_kernel(x_ref, k_ref, v_ref, wq_ref, wo_ref, g_ref, b_ref, o_ref):
    x = x_ref[...]
    q = _dot(x.astype(BF16), wq_ref[...]).astype(BF16)
    heads = []
    for h in range(X_HEADS):
        sl = slice(h * X_HEAD_DIM, (h + 1) * X_HEAD_DIM)
        st = _dot_nt(q[:, sl], k_ref[:, sl])
        m = jnp.max(st, axis=-1, keepdims=True)
        e = jnp.exp(st - m)
        l = jnp.sum(e, axis=-1, keepdims=True)
        heads.append((_dot(e.astype(BF16), v_ref[:, sl]) / l).astype(BF16))
    o = jnp.concatenate(heads, axis=-1)
    h_out = _dot(o, wo_ref[...])
    o_ref[...] = _layer_norm(DN_ALPHA * x + h_out, g_ref[...], b_ref[...])


def _xattn(x3d, k3d, v3d, wq, wo, g, bta, tm):
    b, s, d = x3d.shape
    xspec = pl.BlockSpec((None, tm, d), lambda bi, i: (bi, i, 0))
    mspec = pl.BlockSpec((None, MEM_LEN, d), lambda bi, i: (bi, 0, 0))
    consts = (wq, wo, g, bta)
    return pl.pallas_call(
        _xattn_kernel, out_shape=jax.ShapeDtypeStruct((b, s, d), F32), grid=(b, s // tm),
        in_specs=[xspec, mspec, mspec] + [_const_spec(c.shape) for c in consts], out_specs=xspec,
        compiler_params=_params(("parallel", "parallel")), name="xattn")(x3d, k3d, v3d, *consts)


MOE_EPB = 2
ROUTE_LANE0 = N_GROUPS


def _moe_kernel(x_ref, wrh_ref, wrl_ref, br_ref, wg_ref, wu_ref, wd_ref, g_ref, b_ref, o_ref,
                xb_ref, gate_ref):
    tm = x_ref.shape[0]
    lane = lax.broadcasted_iota(jnp.int32, (tm, LANES), 1)
    x = x_ref[...]
    lanef = lane.astype(F32)
    xh = x.astype(BF16)
    xl = (x - xh.astype(F32)).astype(BF16)
    xb_ref[...] = xh
    hh_hl = _dot(xh, jnp.concatenate([wrh_ref[...], wrl_ref[...]], axis=1))
    logits = (hh_hl[:, :LANES] + _dot(xl, wrh_ref[...]) + hh_hl[:, LANES:]) + br_ref[...]
    gl = jnp.where(lane < N_GROUPS, logits, NEG)
    gmax = jnp.max(gl, axis=-1, keepdims=True)
    g_gate = 1.0 / jnp.sum(jnp.exp(gl - gmax), axis=-1, keepdims=True)
    g_idx = jnp.min(jnp.where(gl == gmax, lanef, float(LANES)), axis=-1, keepdims=True)
    e_group = ((lane - ROUTE_LANE0) >> 2).astype(F32)
    sel = (lane >= ROUTE_LANE0) & (e_group == g_idx)
    el = jnp.where(sel, logits, NEG)
    top1 = jnp.max(el, axis=-1, keepdims=True)
    i1 = jnp.min(jnp.where(el == top1, lanef, float(LANES)), axis=-1, keepdims=True)
    el2 = jnp.where(lanef == i1, NEG, el)
    top2 = jnp.max(el2, axis=-1, keepdims=True)
    i2 = jnp.min(jnp.where(el2 == top2, lanef, float(LANES)), axis=-1, keepdims=True)
    r = jnp.exp(top2 - top1)
    w1 = 1.0 / (1.0 + r)
    w2 = r * w1
    gate_ref[...] = jnp.where(lanef == i1, w1, jnp.where(lanef == i2, w2, 0.0)) * g_gate
    o_ref[...] = jnp.zeros_like(o_ref)

    def experts(eb, carry):
        xb = xb_ref[...]
        gates = gate_ref[...]
        for k in range(MOE_EPB):
            e = eb * MOE_EPB + k
            gcol = jnp.sum(jnp.where(lane == ROUTE_LANE0 + e, gates, 0.0), axis=-1, keepdims=True)
            a = _dot(xb, wg_ref[e])
            u = _dot(xb, wu_ref[e])
            h = (a / (1.0 + jnp.exp(-a))) * u * gcol
            o_ref[...] += _dot(h.astype(BF16), wd_ref[e])
        return carry

    lax.fori_loop(0, N_EXPERTS // MOE_EPB, experts, 0)
    o_ref[...] = _layer_norm(DN_ALPHA * x_ref[...] + o_ref[...], g_ref[...], b_ref[...])


def _moe(x2d, wrh, wrl, br, wg, wu, wd, g, bta, tm):
    t = x2d.shape[0]
    row = pl.BlockSpec((tm, D_MODEL), lambda i: (i, 0))
    consts = (wrh, wrl, br, wg, wu, wd, g, bta)
    return pl.pallas_call(
        _moe_kernel, out_shape=jax.ShapeDtypeStruct((t, D_MODEL), F32),
        grid=(t // tm,), in_specs=[row] + [_const_spec(c.shape) for c in consts], out_specs=row,
        scratch_shapes=[pltpu.VMEM((tm, D_MODEL), BF16), pltpu.VMEM((tm, LANES), F32)],
        compiler_params=_params(("parallel",)), name="moe")(x2d, *consts)


def _pad_cols(w, n):
    return jnp.pad(w, ((0, 0), (0, n - w.shape[1])))


def kernel(x, mem, positions, w_in, b_forget, diff_lambda, diff_subln, w_branch_a, w_branch_b, w_branch_c,
           w_gate, b_gate, w_out, w_xq, w_xk, w_xv, w_xo, w_route_group, b_route_group, w_route_expert,
           b_route_expert, w_expert_gate, w_expert_up, w_expert_down, ln_g, ln_b):
    b, s, d = x.shape
    t = b * s
    depth = w_in.shape[0]
    scale = HEAD_DIM ** -0.5

    half = HEAD_DIM // 2
    inv = ROPE_THETA ** (-jnp.arange(half, dtype=F32) / half)
    ang = positions.astype(F32).reshape(t, 1) * inv
    cos, sin = jnp.cos(ang), jnp.sin(ang)
    cos_t = jnp.concatenate([cos, cos, cos, cos], axis=-1)
    sin_t = jnp.concatenate([-sin, sin, -sin, sin], axis=-1)

    mem2d = mem.reshape(b * MEM_LEN, d)
    x2d = x.reshape(t, d)
    row = lambda v: v.reshape(1, -1).astype(F32)
    for l in range(depth):
        lam_init = 0.8 - 0.6 * math.exp(-0.3 * l)
        offs = [0, A_W, 2 * A_W, 3 * A_W, 3 * A_W + B_W, 3 * A_W + 2 * B_W, 3 * A_W + 3 * B_W]
        offs.append(offs[-1] + B_HEADS)
        offs += [offs[-1] + C_W, offs[-1] + 2 * C_W, offs[-1] + 3 * C_W]
        wl = w_in[l]
        seg = [wl[:, offs[k]:offs[k + 1]] for k in range(10)]
        qa, ka, va, qb, kb, vb, fb, qc, kc, vc = seg
        ws = [(qa * (scale * LOG2E)), ka, va.T, (qb * (scale * LOG2E)), kb, vb.T, _pad_cols(fb, LANES),
              (qc * (scale * LOG2E)), kc, vc]
        ws = [w.astype(BF16) for w in ws]
        bfp = _pad_cols(row(b_forget[l]), LANES)

        outs = _inproj(x2d, cos_t, sin_t, bfp, ws, tm=512)
        o_qa, o_ka, o_va, o_qb, o_kb, o_vb, o_lf = outs[:7]
        o_qc, o_kc, o_vc = outs[7:10], outs[10:13], outs[13:16]

        lf = diff_lambda[l].astype(F32)
        lam = jnp.exp(jnp.dot(lf[0], lf[1])) - jnp.exp(jnp.dot(lf[2], lf[3])) + lam_init
        lam_row = jnp.full((1, LANES), lam, F32)
        gain = row(diff_subln[l]) * (1.0 - lam_init)
        r3 = lambda v: v.reshape(b, s, v.shape[-1])
        oa = _flash(r3(o_qa), r3(o_ka), o_va, (lam_row, gain), forget=False, bq=2 * FLASH_BK)
        kb_aug = _fox_keys(r3(o_kb), r3(o_lf), tk=512)
        ob = _flash(r3(o_qb), kb_aug, o_vb, (), forget=True, bq=2 * FLASH_BK)
        ocs, lses = [], []
        for gi, (_, dil) in enumerate(C_PATTERNS):
            nq = min(4, s // (dil * C_BLOCK))
            o_g, lse_g = _dilated(o_qc[gi], o_kc[gi], o_vc[gi], b, dil, nq, nr=min(dil, 4 // nq))
            ocs.append(o_g)
            lses.append(lse_g)
        x2d = _merge(x2d, oa.reshape(t, A_W), ob.reshape(t, B_W), ocs, lses,
                     w_gate[l].astype(BF16), row(b_gate[l]), w_branch_a[l].astype(BF16),
                     w_branch_b[l].astype(BF16), w_branch_c[l].astype(BF16), w_out[l].astype(BF16),
                     row(ln_g[l, 0]), row(ln_b[l, 0]), tm=512)
        xscale = X_HEAD_DIM ** -0.5
        mk, mv = _memkv(mem2d, w_xk[l].astype(BF16), w_xv[l].astype(BF16), tm=MEM_LEN)
        x2d = _xattn(x2d.reshape(b, s, d), mk.reshape(b, MEM_LEN, d), mv.reshape(b, MEM_LEN, d),
                     (w_xq[l] * xscale).astype(BF16), w_xo[l].astype(BF16),
                     row(ln_g[l, 1]), row(ln_b[l, 1]), tm=512).reshape(t, d)
        wr = jnp.concatenate([w_route_group[l]] + [w_route_expert[l, gi] for gi in range(N_GROUPS)], axis=1)
        wr = _pad_cols(wr.astype(F32), LANES)
        wrh = wr.astype(BF16)
        wrl = (wr - wrh.astype(F32)).astype(BF16)
        br = _pad_cols(row(jnp.concatenate([b_route_group[l], b_route_expert[l].reshape(-1)])), LANES)
        ne = (N_EXPERTS,)
        x2d = _moe(x2d, wrh, wrl, br,
                   w_expert_gate[l].reshape(ne + w_expert_gate.shape[3:]).astype(BF16),
                   w_expert_up[l].reshape(ne + w_expert_up.shape[3:]).astype(BF16),
                   w_expert_down[l].reshape(ne + w_expert_down.shape[3:]).astype(BF16),
                   row(ln_g[l, 2]), row(ln_b[l, 2]), tm=1024)
    return x2d.reshape(b, s, d)
```

```python
import functools
import math

import jax
import jax.numpy as jnp
from jax import lax
from jax.experimental import pallas as pl
from jax.experimental.pallas import tpu as pltpu

F32 = jnp.float32
BF16 = jnp.bfloat16

D_MODEL = 1024
HEAD_DIM = 64
ROPE_THETA = 10000.0
EPS = 1e-5
MEM_LEN = 256
A_HEADS = 4
B_HEADS = 8
C_PATTERNS = ((128, 1), (512, 4), (2048, 16))
C_GROUPS = 3
C_SLOTS = 4
C_BLOCK = 128
X_HEADS = 4
X_HEAD_DIM = D_MODEL // X_HEADS
N_GROUPS = 4
EXPERTS_PER_GROUP = 4
N_EXPERTS = N_GROUPS * EXPERTS_PER_GROUP
D_EXPERT = D_MODEL // 4
DEPTH = 2
DN_ALPHA = (2 * DEPTH) ** 0.25

LANES = 128
NEG = -1e30
LOG2E = math.log2(math.e)
VMEM_LIMIT = 56 * 1024 * 1024

A_W = A_HEADS * 2 * HEAD_DIM
B_W = B_HEADS * HEAD_DIM
C_GW = C_SLOTS * HEAD_DIM
C_W = C_GROUPS * C_GW
ONES_ROWS = 16
VA_ROWS = A_HEADS * (2 * HEAD_DIM + ONES_ROWS)
VB_ROWS = B_HEADS * (HEAD_DIM + ONES_ROWS)
SUB_ROWS = 256
FLASH_BK = 512


def _params(sem, flags=None):
    return pltpu.CompilerParams(dimension_semantics=sem, vmem_limit_bytes=VMEM_LIMIT, flags=flags)


def _const_spec(shape):
    nd = len(shape)
    return pl.BlockSpec(shape, lambda *_: (0,) * nd, pipeline_mode=pl.Buffered(1))


def _layer_norm(y, g, b):
    mu = jnp.mean(y, axis=-1, keepdims=True)
    yc = y - mu
    var = jnp.mean(yc * yc, axis=-1, keepdims=True)
    return yc * lax.rsqrt(var + EPS) * g + b


def _dot(a, b):
    return jnp.dot(a, b, preferred_element_type=F32)


def _dot_nt(a, b):
    return lax.dot_general(a, b, (((1,), (1,)), ((), ())), preferred_element_type=F32)


def _inproj_kernel(x_ref, pos_ref, inv_ref, bf_ref,
                   w_qa, w_ka, w_va, w_qb, w_kb, w_vb, w_fb, w_qc, w_kc, w_vc,
                   o_qa, o_ka, o_va, o_qb, o_kb, o_vb, o_lf,
                   o_qc0, o_qc1, o_qc2, o_kc0, o_kc1, o_kc2, o_vc0, o_vc1, o_vc2, perm_ref):
    xb = x_ref[...].astype(BF16)
    ang = pos_ref[...] * inv_ref[...]
    lane = lax.broadcasted_iota(jnp.int32, ang.shape, 1)
    lower = (lane & (HEAD_DIM - 1)) < HEAD_DIM // 2
    cos = jnp.cos(ang)
    sin = jnp.sin(ang)
    sin = jnp.where(lower, -sin, sin)

    def rope(t):
        sw = jnp.where(lower, pltpu.roll(t, LANES - HEAD_DIM // 2, 1), pltpu.roll(t, HEAD_DIM // 2, 1))
        return t * cos + sw * sin

    def roped(w_ref, outs, width):
        y = _dot(xb, w_ref[...])
        per_out = width // LANES
        for c in range(y.shape[1] // LANES):
            o = outs[c // per_out]
            lo = (c % per_out) * LANES
            o[:, lo:lo + LANES] = rope(y[:, c * LANES:(c + 1) * LANES]).astype(BF16)

    def plain(w_ref, outs, width):
        y = _dot(xb, w_ref[...])
        for c, o in enumerate(outs):
            o[...] = y[:, c * width:(c + 1) * width].astype(BF16)

    def transposed(wt_ref, o, width):
        yt = _dot_nt(wt_ref[...], xb)
        ones = jnp.ones((ONES_ROWS, FLASH_BK), F32)
        for c in range(o.shape[0]):
            cols = slice(c * FLASH_BK, (c + 1) * FLASH_BK)
            parts = []
            for r0 in range(0, yt.shape[0], width):
                parts += [yt[r0:r0 + width, cols], ones]
            o[c] = jnp.concatenate(parts, axis=0).astype(BF16)

    roped(w_qa, (o_qa,), A_W)
    roped(w_ka, (o_ka,), A_W)
    transposed(w_va, o_va, LANES)
    plain(w_qb, (o_qb,), B_W)
    plain(w_kb, (o_kb,), B_W)
    transposed(w_vb, o_vb, HEAD_DIM)
    def dilated(w_ref, outs, slot, use_rope):
        y = _dot(xb, w_ref[...])
        tm = y.shape[0]
        for gi, (_, d) in enumerate(C_PATTERNS):
            for h in range(C_GW // LANES):
                c = gi * (C_GW // LANES) + h
                t = y[:, c * LANES:(c + 1) * LANES]
                t = rope(t) if use_rope else t
                if d == 1:
                    outs[gi][:, h * LANES:(h + 1) * LANES] = t.astype(BF16)
                    continue
                sl = slot + (gi - 1) * 2 + h
                perm_ref[sl] = t
                for r in range(d):
                    lo = r * C_GW + h * LANES
                    outs[gi][:, lo:lo + LANES] = perm_ref.at[sl][pl.ds(r, tm // d, stride=d), :].astype(BF16)

    dilated(w_qc, (o_qc0, o_qc1, o_qc2), 0, True)
    dilated(w_kc, (o_kc0, o_kc1, o_kc2), 4, True)
    dilated(w_vc, (o_vc0, o_vc1, o_vc2), 8, False)
    z =_dot(xb, w_fb[...]) + bf_ref[...]
    o_lf[...] = jnp.minimum(z, 0.0) - jnp.log(1.0 + jnp.exp(-jnp.abs(z)))


def _inproj(x2d, pos, inv, bfp, ws, tm):
    t = x2d.shape[0]
    row = lambda n: pl.BlockSpec((tm, n), lambda i: (i, 0))
    tshape = lambda rows: jax.ShapeDtypeStruct((t // FLASH_BK, rows, FLASH_BK), BF16)
    tspec = lambda rows: pl.BlockSpec((tm // FLASH_BK, rows, FLASH_BK), lambda i: (i, 0, 0))
    rshape = lambda w: jax.ShapeDtypeStruct((t, w), BF16)
    out_shape = [rshape(A_W), rshape(A_W), tshape(VA_ROWS), rshape(B_W), rshape(B_W), tshape(VB_ROWS)]
    out_shape.append(jax.ShapeDtypeStruct((t, LANES), F32))
    out_specs = [row(A_W), row(A_W), tspec(VA_ROWS), row(B_W), row(B_W), tspec(VB_ROWS), row(LANES)]
    for _ in range(3):
        for _, d in C_PATTERNS:
            out_shape.append(jax.ShapeDtypeStruct((t // d, d * C_GW), BF16))
            out_specs.append(pl.BlockSpec((tm // d, d * C_GW), lambda i: (i, 0)))
    in_specs = [row(D_MODEL), row(1), _const_spec((1, LANES)), _const_spec((1, LANES))]
    in_specs += [_const_spec(w.shape) for w in ws]
    n_perm = 3 * (C_GROUPS - 1) * (C_GW // LANES)
    return pl.pallas_call(
        _inproj_kernel, out_shape=out_shape, grid=(t // tm,), in_specs=in_specs, out_specs=out_specs,
        scratch_shapes=[pltpu.VMEM((n_perm, tm, LANES), F32)],
        compiler_params=_params(("parallel",)), name="inproj")(x2d, pos, inv, bfp, *ws)


BIAS_PARTS = 3


def _split3(v):
    p0 = v.astype(BF16)
    r1 = v - p0.astype(F32)
    p1 = r1.astype(BF16)
    p2 = (r1 - p1.astype(F32)).astype(BF16)
    return p0, p1, p2


def _fox_keys_kernel(k_ref, lf_ref, tri_ref, sel_ref, o_ref, carry_ref):
    @pl.when(pl.program_id(1) == 0)
    def _reset():
        carry_ref[...] = jnp.zeros_like(carry_ref)

    tk = lf_ref.shape[0]
    tri = tri_ref[...]
    c = carry_ref[...] + sum(_dot(tri, part) for part in _split3(lf_ref[...]))
    carry_ref[...] = c[tk - 1:tk, :]
    extra = _dot(jnp.concatenate(_split3(-LOG2E * c), axis=1), sel_ref[...])
    low = lax.broadcasted_iota(jnp.int32, (tk, LANES), 1) < HEAD_DIM
    for h in range(B_HEADS):
        pair = k_ref[:, (h // 2) * LANES:(h // 2 + 1) * LANES].astype(F32)
        own = low if h % 2 == 0 else jnp.logical_not(low)
        sl = slice(h * LANES, (h + 1) * LANES)
        o_ref[:, sl] = jnp.where(own, pair, extra[:, sl]).astype(BF16)


def _fox_keys(k3d, lf3d, tk):
    b, s, w = k3d.shape
    tri = jnp.tril(jnp.ones((tk, tk), BF16))
    rows = jnp.arange(BIAS_PARTS * LANES)
    part, head = rows // LANES, rows % LANES
    col = head * LANES + jnp.where(head % 2 == 0, HEAD_DIM, 0) + part
    sel = jnp.zeros((BIAS_PARTS * LANES, B_HEADS * LANES), BF16)
    sel = sel.at[rows, jnp.where(head < B_HEADS, col, 0)].add(jnp.where(head < B_HEADS, 1, 0).astype(BF16))
    blk = lambda n: pl.BlockSpec((None, tk, n), lambda bi, j: (bi, j, 0))
    return pl.pallas_call(
        _fox_keys_kernel, out_shape=jax.ShapeDtypeStruct((b, s, B_HEADS * LANES), BF16),
        grid=(b, s // tk), in_specs=[blk(w), blk(LANES), _const_spec(tri.shape), _const_spec(sel.shape)],
        out_specs=blk(B_HEADS * LANES), scratch_shapes=[pltpu.VMEM((1, LANES), F32)],
        compiler_params=_params(("parallel", "arbitrary")), name="fox_keys")(k3d, lf3d, tri, sel)


def _flash_kernel(forget, bq, *refs):
    bk = FLASH_BK
    if forget:
        q_ref, k_ref, vt_ref, o_ref, m_ref, acc_ref, st_a, st_b = refs
    else:
        q_ref, k_ref, vt_ref, lam_ref, gain_ref, o_ref, m_ref, acc_ref, st_a, st_b = refs
    dvp = acc_ref.shape[2]
    dv = dvp - ONES_ROWS
    i = pl.program_id(2)
    q = q_ref[...].astype(F32)
    lane = lax.broadcasted_iota(jnp.int32, (bq, LANES), 1)
    low = lane < HEAD_DIM
    if forget:
        fill = (jnp.where(lane < HEAD_DIM + BIAS_PARTS, 1.0, 0.0), jnp.where(lane < BIAS_PARTS, 1.0, 0.0))
    else:
        fill = (0.0, 0.0)
    qts = (jnp.where(low, q, fill[0]).T.astype(BF16), jnp.where(low, fill[1], q).T.astype(BF16))
    key = lax.broadcasted_iota(jnp.int32, (bk, bk), 0)
    qry = lax.broadcasted_iota(jnp.int32, (bk, bk), 1)
    halves = bq // bk

    m_ref[...] = jnp.full(m_ref.shape, NEG, F32)
    acc_ref[...] = jnp.zeros(acc_ref.shape, F32)

    def scores(j, st_ref, first_half=0):
        off = pl.multiple_of(j * bk, bk)
        for s in range(2):
            kc = k_ref[pl.ds(off, bk), s * LANES:(s + 1) * LANES] if forget else k_ref[pl.ds(off, bk), :]
            for h in range(first_half, halves):
                st_ref[s, h] = _dot(kc, qts[s][:, h * bk:(h + 1) * bk])

    def absorb(j, st_ref, first_half=0, diagonal=False):
        for s in range(2):
            for h in range(first_half, halves):
                st = st_ref[s, h]
                if diagonal and h == first_half:
                    st = jnp.where(key <= qry, st, NEG)
                m_prev = m_ref[s, h]
                m_new = jnp.maximum(m_prev, jnp.max(st, axis=0, keepdims=True))
                a = jnp.exp2(m_prev - m_new)
                p = jnp.exp2(st - m_new).astype(BF16)
                m_ref[s, h] = m_new
                vt = vt_ref[j, s * dvp:(s + 1) * dvp, :] if forget else vt_ref[j]
                acc_ref[s, h] = a * acc_ref[s, h] + _dot(vt, p)

    scores(0, st_a)

    def pair(t, c):
        scores(2 * t + 1, st_b)
        absorb(2 * t, st_a)
        scores(2 * t + 2, st_a)
        absorb(2 * t + 1, st_b)
        return c

    lax.fori_loop(0, i, pair, 0)
    scores(2 * i + 1, st_b, 1)
    absorb(2 * i, st_a, 0, True)
    absorb(2 * i + 1, st_b, 1, True)

    for h in range(halves):
        o0 = acc_ref[0, h, :dv, :] / acc_ref[0, h, dv:dv + 1, :]
        o1 = acc_ref[1, h, :dv, :] / acc_ref[1, h, dv:dv + 1, :]
        rows = slice(h * bk, (h + 1) * bk)
        if forget:
            o_ref[rows, :] = jnp.concatenate([o0, o1], axis=0).T.astype(o_ref.dtype)
        else:
            o = (o0 - lam_ref[:, 0:1] * o1).T
            ms = jnp.mean(o * o, axis=-1, keepdims=True)
            o_ref[rows, :] = (o * lax.rsqrt(ms + EPS) * gain_ref[...]).astype(o_ref.dtype)


def _flash(q, k, vt, extra, forget, bq):
    b, s, w = q.shape
    g = w // LANES
    nk = s // FLASH_BK
    dvp = (HEAD_DIM if forget else LANES) + ONES_ROWS
    rows = vt.shape[1] // g
    kw = k.shape[2] // g
    qspec = pl.BlockSpec((None, bq, LANES), lambda bi, gi, i: (bi, i, gi))
    kspec = pl.BlockSpec((None, s, kw), lambda bi, gi, i: (bi, 0, gi))
    vspec = pl.BlockSpec((nk, rows, FLASH_BK), lambda bi, gi, i: (bi, gi, 0))
    assert bq == 2 * FLASH_BK
    bk = FLASH_BK
    scratch = [pltpu.VMEM((2, 2, 1, bk), F32), pltpu.VMEM((2, 2, dvp, bk), F32),
               pltpu.VMEM((2, 2, bk, bk), F32), pltpu.VMEM((2, 2, bk, bk), F32)]
    especs = [] if forget else [_const_spec((1, LANES)), _const_spec((1, LANES))]
    return pl.pallas_call(
        functools.partial(_flash_kernel, forget, bq),
        out_shape=jax.ShapeDtypeStruct((b, s, w), BF16),
        grid=(b, g, s // bq), in_specs=[qspec, kspec, vspec] + especs, out_specs=qspec,
        scratch_shapes=scratch,
        compiler_params=_params(("parallel", "parallel", "arbitrary")),
        name="fox_attn" if forget else "diff_attn")(q, k, vt, *extra)


def _dilated_kernel(nq, q_ref, kp_ref, kc_ref, vp_ref, vc_ref, o_ref, lse_ref):
    n = C_BLOCK
    pairs = q_ref.shape[1] // LANES
    j = pl.program_id(2)
    kall = jnp.concatenate([kp_ref[...], kc_ref[...]], axis=0)
    vall = jnp.concatenate([vp_ref[...], vc_ref[...]], axis=0)
    qi = lax.broadcasted_iota(jnp.int32, (n, 2 * n), 0)
    kj = lax.broadcasted_iota(jnp.int32, (n, 2 * n), 1)
    dist = n + qi - kj
    window = (dist >= 0) & (dist <= n)
    first_key = jnp.where(j > 0, 0, n)
    lane = lax.broadcasted_iota(jnp.int32, (n, LANES), 1)
    low = lane < HEAD_DIM
    units = [(t, p, s) for t in range(nq) for p in range(pairs) for s in range(2)]
    scores = {}
    for t, p, s in units:
        sl = slice(p * LANES, (p + 1) * LANES)
        q128 = q_ref[t * n:(t + 1) * n, sl]
        zero = jnp.zeros_like(q128)
        qm = jnp.where(low, q128, zero) if s == 0 else jnp.where(low, zero, q128)
        mask = (window & (kj >= first_key)) if t == 0 else window
        scores[t, p, s] = jnp.where(mask, _dot_nt(qm, kall[t * n:(t + 2) * n, sl]), NEG)
    probs = {}
    for u in units:
        m = jnp.max(scores[u], axis=-1, keepdims=True)
        probs[u] = (jnp.exp2(scores[u] - m).astype(BF16), m)
    low_keys = lax.broadcasted_iota(jnp.int32, (2 * n, LANES), 1) < HEAD_DIM
    for t in range(nq):
        for p in range(pairs):
            sl = slice(p * LANES, (p + 1) * LANES)
            v128 = vall[t * n:(t + 2) * n, sl]
            one = jnp.ones_like(v128)
            os, ls = [], []
            for s in range(2):
                vs = jnp.where(low_keys, v128, one) if s == 0 else jnp.where(low_keys, one, v128)
                num = _dot(probs[t, p, s][0], vs)
                den = pltpu.roll(num, HEAD_DIM, 1)
                os.append(num / den)
                ls.append(probs[t, p, s][1] + jnp.log2(den))
            o_ref[t * n:(t + 1) * n, sl] = jnp.where(low, os[0], os[1])
            lse_ref[t * n:(t + 1) * n, sl] = jnp.where(low, ls[0], ls[1])


def _dilated(q, k, v, b, dilation, nq, nr):
    w = C_GW * nr
    n = C_BLOCK
    m_len = q.shape[0] // b
    nb = m_len // (n * nq)
    view = lambda t: t.reshape(b, m_len, dilation * C_GW)
    cur = pl.BlockSpec((None, n * nq, w), lambda bi, r, j: (bi, j, r))
    prev = pl.BlockSpec((None, n, w), lambda bi, r, j: (bi, jnp.maximum(j * nq - 1, 0), r))
    o, lse = pl.pallas_call(
        functools.partial(_dilated_kernel, nq),
        out_shape=[jax.ShapeDtypeStruct((b, m_len, dilation * C_GW), F32)] * 2,
        grid=(b, dilation // nr, nb), in_specs=[cur, prev, cur, prev, cur], out_specs=[cur, cur],
        compiler_params=_params(("parallel", "parallel", "arbitrary")),
        name=f"dilated_d{dilation}")(view(q), view(k), view(k), view(v), view(v))
    return o.reshape(b * m_len, dilation * C_GW), lse.reshape(b * m_len, dilation * C_GW)


def _merge_kernel(x_ref, oa_ref, ob_ref, oc0, oc1, oc2, ls0, ls1, ls2,
                  wg_ref, bg_ref, wa_ref, wb_ref, wc_ref, wo_ref, g_ref, b_ref, o_ref, perm_ref):
    x = x_ref[...]
    xb = x.astype(BF16)
    tm = x.shape[0]

    def token_major(blk_ref, gi, slot):
        d = C_PATTERNS[gi][1]
        if d == 1:
            return blk_ref[...]
        for h in range(C_GW // LANES):
            for r in range(d):
                lo = r * C_GW + h * LANES
                perm_ref.at[slot + h][pl.ds(r, tm // d, stride=d), :] = blk_ref[:, lo:lo + LANES]
        return jnp.concatenate([perm_ref[slot + h] for h in range(C_GW // LANES)], axis=1)

    os = [token_major(r, gi, 2 * gi) for gi, r in enumerate((oc0, oc1, oc2))]
    ls = [token_major(r, gi, 2 * C_GROUPS + 2 * gi) for gi, r in enumerate((ls0, ls1, ls2))]
    for r0 in range(0, tm, SUB_ROWS):
        rows = slice(r0, r0 + SUB_ROWS)
        l0, l1, l2 = (l[rows] for l in ls)
        m = jnp.maximum(jnp.maximum(l0, l1), l2)
        e0, e1, e2 = jnp.exp2(l0 - m), jnp.exp2(l1 - m), jnp.exp2(l2 - m)
        oc = (e0 * os[0][rows] + e1 * os[1][rows] + e2 * os[2][rows]) / (e0 + e1 + e2)
        branches = ((oa_ref[rows, :], wa_ref), (ob_ref[rows, :], wb_ref), (oc.astype(BF16), wc_ref))
        merged = None
        for k, (ob, w_ref) in enumerate(branches):
            sl = slice(k * D_MODEL, (k + 1) * D_MODEL)
            z = _dot(xb[rows], wg_ref[:, sl]) + bg_ref[:, sl]
            gate = 1.0 / (1.0 + jnp.exp(-z))
            term = gate * _dot(ob, w_ref[...])
            merged = term if merged is None else merged + term
        h = _dot(merged.astype(BF16), wo_ref[...])
        o_ref[rows, :] = _layer_norm(DN_ALPHA * x[rows] + h, g_ref[...], b_ref[...])


def _merge(x2d, oa, ob, ocs, lses, wg, bg, wa, wb, wc, wo, g, bta, tm):
    t = x2d.shape[0]
    row = lambda n: pl.BlockSpec((tm, n), lambda i: (i, 0))
    consts = (wg, bg, wa, wb, wc, wo, g, bta)
    cspecs = [pl.BlockSpec((tm // d, d * C_GW), lambda i: (i, 0)) for _, d in C_PATTERNS]
    in_specs = [row(D_MODEL), row(A_W), row(B_W)] + cspecs * 2 + [_const_spec(c.shape) for c in consts]
    return pl.pallas_call(
        _merge_kernel, out_shape=jax.ShapeDtypeStruct((t, D_MODEL), F32), grid=(t // tm,),
        in_specs=in_specs, out_specs=row(D_MODEL),
        scratch_shapes=[pltpu.VMEM((4 * C_GROUPS, tm, LANES), F32)],
        compiler_params=_params(("parallel",)), name="merge")(x2d, oa, ob, *ocs, *lses, *consts)


def _memkv_kernel(m_ref, wk_ref, wv_ref, k_ref, v_ref):
    mb = m_ref[...].astype(BF16)
    k_ref[...] = _dot(mb, wk_ref[...]).astype(BF16)
    v_ref[...] = _dot(mb, wv_ref[...]).astype(BF16)


def _memkv(mem2d, wk, wv, tm):
    t = mem2d.shape[0]
    row = pl.BlockSpec((tm, D_MODEL), lambda i: (i, 0))
    return pl.pallas_call(
        _memkv_kernel, out_shape=[jax.ShapeDtypeStruct((t, D_MODEL), BF16)] * 2, grid=(t // tm,),
        in_specs=[row, _const_spec(wk.shape), _const_spec(wv.shape)], out_specs=[row, row],
        compiler_params=_params(("parallel",)), name="memkv")(mem2d, wk, wv)


def _xattn_kernel(x_ref, k_ref, v_ref, wq_ref, wo_ref, g_ref, b_ref, o_ref):
    for r0 in range(0, x_ref.shape[0], SUB_ROWS):
        rows = slice(r0, r0 + SUB_ROWS)
        x = x_ref[rows, :]
        q = _dot(x.astype(BF16), wq_ref[...]).astype(BF16)
        heads = []
        for h in range(X_HEADS):
            sl = slice(h * X_HEAD_DIM, (h + 1) * X_HEAD_DIM)
            st = _dot_nt(q[:, sl], k_ref[:, sl])
            m = jnp.max(st, axis=-1, keepdims=True)
            e = jnp.exp(st - m)
            l = jnp.sum(e, axis=-1, keepdims=True)
            heads.append((_dot(e.astype(BF16), v_ref[:, sl]) / l).astype(BF16))
        o = jnp.concatenate(heads, axis=-1)
        h_out = _dot(o, wo_ref[...])
        o_ref[rows, :] = _layer_norm(DN_ALPHA * x + h_out, g_ref[...], b_ref[...])


def _xattn(x3d, k3d, v3d, wq, wo, g, bta, tm):
    b, s, d = x3d.shape
    xspec = pl.BlockSpec((None, tm, d), lambda bi, i: (bi, i, 0))
    mspec = pl.BlockSpec((None, MEM_LEN, d), lambda bi, i: (bi, 0, 0))
    consts = (wq, wo, g, bta)
    return pl.pallas_call(
        _xattn_kernel, out_shape=jax.ShapeDtypeStruct((b, s, d), F32), grid=(b, s // tm),
        in_specs=[xspec, mspec, mspec] + [_const_spec(c.shape) for c in consts], out_specs=xspec,
        compiler_params=_params(("parallel", "parallel")), name="xattn")(x3d, k3d, v3d, *consts)


MOE_EPB = 2
ROUTE_LANE0 = N_GROUPS


def _moe_kernel(x_ref, wrh_ref, wrl_ref, br_ref, wg_ref, wu_ref, wd_ref, g_ref, b_ref, o_ref,
                xb_ref, gate_ref):
    tm = x_ref.shape[0]
    lane = lax.broadcasted_iota(jnp.int32, (tm, LANES), 1)
    x = x_ref[...]
    lanef = lane.astype(F32)
    xh = x.astype(BF16)
    xl = (x - xh.astype(F32)).astype(BF16)
    xb_ref[...] = xh
    hh_hl = _dot(xh, jnp.concatenate([wrh_ref[...], wrl_ref[...]], axis=1))
    logits = (hh_hl[:, :LANES] + _dot(xl, wrh_ref[...]) + hh_hl[:, LANES:]) + br_ref[...]
    gl = jnp.where(lane < N_GROUPS, logits, NEG)
    gmax = jnp.max(gl, axis=-1, keepdims=True)
    g_gate = 1.0 / jnp.sum(jnp.exp(gl - gmax), axis=-1, keepdims=True)
    g_idx = jnp.min(jnp.where(gl == gmax, lanef, float(LANES)), axis=-1, keepdims=True)
    e_group = ((lane - ROUTE_LANE0) >> 2).astype(F32)
    sel = (lane >= ROUTE_LANE0) & (e_group == g_idx)
    el = jnp.where(sel, logits, NEG)
    top1 = jnp.max(el, axis=-1, keepdims=True)
    i1 = jnp.min(jnp.where(el == top1, lanef, float(LANES)), axis=-1, keepdims=True)
    el2 = jnp.where(lanef == i1, NEG, el)
    top2 = jnp.max(el2, axis=-1, keepdims=True)
    i2 = jnp.min(jnp.where(el2 == top2, lanef, float(LANES)), axis=-1, keepdims=True)
    r = jnp.exp(top2 - top1)
    w1 = 1.0 / (1.0 + r)
    w2 = r * w1
    gate_ref[...] = jnp.where(lanef == i1, w1, jnp.where(lanef == i2, w2, 0.0)) * g_gate
    o_ref[...] = jnp.zeros_like(o_ref)

    def experts(eb, carry):
        xb = xb_ref[...]
        gates = gate_ref[...]
        for k in range(MOE_EPB):
            e = eb * MOE_EPB + k
            gcol = jnp.sum(jnp.where(lane == ROUTE_LANE0 + e, gates, 0.0), axis=-1, keepdims=True)
            a = _dot(xb, wg_ref[e])
            u = _dot(xb, wu_ref[e])
            h = (a / (1.0 + jnp.exp(-a))) * u * gcol
            o_ref[...] += _dot(h.astype(BF16), wd_ref[e])
        return carry

    lax.fori_loop(0, N_EXPERTS // MOE_EPB, experts, 0)
    o_ref[...] = _layer_norm(DN_ALPHA * x_ref[...] + o_ref[...], g_ref[...], b_ref[...])


def _moe(x2d, wrh, wrl, br, wg, wu, wd, g, bta, tm):
    t = x2d.shape[0]
    row = pl.BlockSpec((tm, D_MODEL), lambda i: (i, 0))
    consts = (wrh, wrl, br, wg, wu, wd, g, bta)
    return pl.pallas_call(
        _moe_kernel, out_shape=jax.ShapeDtypeStruct((t, D_MODEL), F32),
        grid=(t // tm,), in_specs=[row] + [_const_spec(c.shape) for c in consts], out_specs=row,
        scratch_shapes=[pltpu.VMEM((tm, D_MODEL), BF16), pltpu.VMEM((tm, LANES), F32)],
        compiler_params=_params(("parallel",)), name="moe")(x2d, *consts)


def _pad_cols(w, n):
    return jnp.pad(w, ((0, 0), (0, n - w.shape[1])))


def kernel(x, mem, positions, w_in, b_forget, diff_lambda, diff_subln, w_branch_a, w_branch_b, w_branch_c,
           w_gate, b_gate, w_out, w_xq, w_xk, w_xv, w_xo, w_route_group, b_route_group, w_route_expert,
           b_route_expert, w_expert_gate, w_expert_up, w_expert_down, ln_g, ln_b):
    b, s, d = x.shape
    t = b * s
    depth = w_in.shape[0]
    scale = HEAD_DIM ** -0.5

    half = HEAD_DIM // 2
    inv = ROPE_THETA ** (-jnp.arange(half, dtype=F32) / half)
    inv_row = jnp.tile(inv, LANES // half).reshape(1, LANES)
    pos = positions.astype(F32).reshape(t, 1)

    mem2d = mem.reshape(b * MEM_LEN, d)
    x2d = x.reshape(t, d)
    row = lambda v: v.reshape(1, -1).astype(F32)
    for l in range(depth):
        lam_init = 0.8 - 0.6 * math.exp(-0.3 * l)
        offs = [0, A_W, 2 * A_W, 3 * A_W, 3 * A_W + B_W, 3 * A_W + 2 * B_W, 3 * A_W + 3 * B_W]
        offs.append(offs[-1] + B_HEADS)
        offs += [offs[-1] + C_W, offs[-1] + 2 * C_W, offs[-1] + 3 * C_W]
        wl = w_in[l]
        seg = [wl[:, offs[k]:offs[k + 1]] for k in range(10)]
        qa, ka, va, qb, kb, vb, fb, qc, kc, vc = seg
        ws = [(qa * (scale * LOG2E)), ka, va.T, (qb * (scale * LOG2E)), kb, vb.T, _pad_cols(fb, LANES),
              (qc * (scale * LOG2E)), kc, vc]
        ws = [w.astype(BF16) for w in ws]
        bfp = _pad_cols(row(b_forget[l]), LANES)

        outs = _inproj(x2d, pos, inv_row, bfp, ws, tm=512)
        o_qa, o_ka, o_va, o_qb, o_kb, o_vb, o_lf = outs[:7]
        o_qc, o_kc, o_vc = outs[7:10], outs[10:13], outs[13:16]

        lf = diff_lambda[l].astype(F32)
        lam = jnp.exp(jnp.dot(lf[0], lf[1])) - jnp.exp(jnp.dot(lf[2], lf[3])) + lam_init
        lam_row = jnp.full((1, LANES), lam, F32)
        gain = row(diff_subln[l]) * (1.0 - lam_init)
        r3 = lambda v: v.reshape(b, s, v.shape[-1])
        oa = _flash(r3(o_qa), r3(o_ka), o_va, (lam_row, gain), forget=False, bq=2 * FLASH_BK)
        kb_aug = _fox_keys(r3(o_kb), r3(o_lf), tk=256)
        ob = _flash(r3(o_qb), kb_aug, o_vb, (), forget=True, bq=2 * FLASH_BK)
        ocs, lses = [], []
        for gi, (_, dil) in enumerate(C_PATTERNS):
            nq = min(4, s // (dil * C_BLOCK))
            o_g, lse_g = _dilated(o_qc[gi], o_kc[gi], o_vc[gi], b, dil, nq, nr=min(dil, 4 // nq))
            ocs.append(o_g)
            lses.append(lse_g)
        x2d = _merge(x2d, oa.reshape(t, A_W), ob.reshape(t, B_W), ocs, lses,
                     w_gate[l].astype(BF16), row(b_gate[l]), w_branch_a[l].astype(BF16),
                     w_branch_b[l].astype(BF16), w_branch_c[l].astype(BF16), w_out[l].astype(BF16),
                     row(ln_g[l, 0]), row(ln_b[l, 0]), tm=512)
        xscale = X_HEAD_DIM ** -0.5
        mk, mv = _memkv(mem2d, w_xk[l].astype(BF16), w_xv[l].astype(BF16), tm=MEM_LEN)
        x2d = _xattn(x2d.reshape(b, s, d), mk.reshape(b, MEM_LEN, d), mv.reshape(b, MEM_LEN, d),
                     (w_xq[l] * xscale).astype(BF16), w_xo[l].astype(BF16),
                     row(ln_g[l, 1]), row(ln_b[l, 1]), tm=512).reshape(t, d)
        wr = jnp.concatenate([w_route_group[l]] + [w_route_expert[l, gi] for gi in range(N_GROUPS)], axis=1)
        wr = _pad_cols(wr.astype(F32), LANES)
        wrh = wr.astype(BF16)
        wrl = (wr - wrh.astype(F32)).astype(BF16)
        br = _pad_cols(row(jnp.concatenate([b_route_group[l], b_route_expert[l].reshape(-1)])), LANES)
        ne = (N_EXPERTS,)
        x2d = _moe(x2d, wrh, wrl, br,
                   w_expert_gate[l].reshape(ne + w_expert_gate.shape[3:]).astype(BF16),
                   w_expert_up[l].reshape(ne + w_expert_up.shape[3:]).astype(BF16),
                   w_expert_down[l].reshape(ne + w_expert_down.shape[3:]).astype(BF16),
                   row(ln_g[l, 2]), row(ln_b[l, 2]), tm=1024)
    return x2d.reshape(b, s, d)
```

```python
import functools
import math

import jax
import jax.numpy as jnp
import numpy as np
from jax import lax
from jax.experimental import pallas as pl
from jax.experimental.pallas import tpu as pltpu

F32 = jnp.float32
BF16 = jnp.bfloat16

D_MODEL = 1024
HEAD_DIM = 64
ROPE_THETA = 10000.0
EPS = 1e-5
MEM_LEN = 256
A_HEADS = 4
B_HEADS = 8
C_PATTERNS = ((128, 1), (512, 4), (2048, 16))
C_GROUPS = 3
C_SLOTS = 4
C_BLOCK = 128
X_HEADS = 4
X_HEAD_DIM = D_MODEL // X_HEADS
N_GROUPS = 4
EXPERTS_PER_GROUP = 4
N_EXPERTS = N_GROUPS * EXPERTS_PER_GROUP
D_EXPERT = D_MODEL // 4
DEPTH = 2
DN_ALPHA = (2 * DEPTH) ** 0.25

LANES = 128
NEG = -1e30
LOG2E = math.log2(math.e)
VMEM_LIMIT = 56 * 1024 * 1024

A_W = A_HEADS * 2 * HEAD_DIM
B_W = B_HEADS * HEAD_DIM
C_GW = C_SLOTS * HEAD_DIM
C_W = C_GROUPS * C_GW
ONES_ROWS = 16
VA_ROWS = A_HEADS * (2 * HEAD_DIM + ONES_ROWS)
VB_ROWS = B_HEADS * (HEAD_DIM + ONES_ROWS)
SUB_ROWS = 256
FLASH_BK = 512


def _params(sem, flags=None):
    return pltpu.CompilerParams(dimension_semantics=sem, vmem_limit_bytes=VMEM_LIMIT, flags=flags)


def _const_spec(shape):
    nd = len(shape)
    return pl.BlockSpec(shape, lambda *_: (0,) * nd, pipeline_mode=pl.Buffered(1))


def _layer_norm(y, g, b):
    mu = jnp.mean(y, axis=-1, keepdims=True)
    yc = y - mu
    var = jnp.mean(yc * yc, axis=-1, keepdims=True)
    return yc * lax.rsqrt(var + EPS) * g + b


def _dot(a, b):
    return jnp.dot(a, b, preferred_element_type=F32)


def _dot_nt(a, b):
    return lax.dot_general(a, b, (((1,), (1,)), ((), ())), preferred_element_type=F32)


def _inproj_kernel(x_ref, pos_ref, inv_ref, bf_ref,
                   w_qa, w_ka, w_va, w_qb, w_kb, w_vb, w_fb, w_qc, w_kc, w_vc,
                   o_qa, o_ka, o_va, o_qb, o_kb, o_vb, o_lf,
                   o_qc0, o_qc1, o_qc2, o_kc0, o_kc1, o_kc2, o_vc0, o_vc1, o_vc2, perm_ref):
    xb = x_ref[...].astype(BF16)
    ang = pos_ref[...] * inv_ref[...]
    lane = lax.broadcasted_iota(jnp.int32, ang.shape, 1)
    lower = (lane & (HEAD_DIM - 1)) < HEAD_DIM // 2
    cos = jnp.cos(ang)
    sin = jnp.sin(ang)
    sin = jnp.where(lower, -sin, sin)

    def rope(t):
        sw = jnp.where(lower, pltpu.roll(t, LANES - HEAD_DIM // 2, 1), pltpu.roll(t, HEAD_DIM // 2, 1))
        return t * cos + sw * sin

    def roped(w_ref, outs, width):
        y = _dot(xb, w_ref[...])
        per_out = width // LANES
        for c in range(y.shape[1] // LANES):
            o = outs[c // per_out]
            lo = (c % per_out) * LANES
            o[:, lo:lo + LANES] = rope(y[:, c * LANES:(c + 1) * LANES]).astype(BF16)

    def plain(w_ref, outs, width):
        y = _dot(xb, w_ref[...])
        for c, o in enumerate(outs):
            o[...] = y[:, c * width:(c + 1) * width].astype(BF16)

    def transposed(wt_ref, o, width):
        yt = _dot_nt(wt_ref[...], xb)
        ones = jnp.ones((ONES_ROWS, FLASH_BK), F32)
        for c in range(o.shape[0]):
            cols = slice(c * FLASH_BK, (c + 1) * FLASH_BK)
            parts = []
            for r0 in range(0, yt.shape[0], width):
                parts += [yt[r0:r0 + width, cols], ones]
            o[c] = jnp.concatenate(parts, axis=0).astype(BF16)

    roped(w_qa, (o_qa,), A_W)
    roped(w_ka, (o_ka,), A_W)
    transposed(w_va, o_va, LANES)
    plain(w_qb, (o_qb,), B_W)
    plain(w_kb, (o_kb,), B_W)
    transposed(w_vb, o_vb, HEAD_DIM)
    def dilated(w_ref, outs, slot, use_rope):
        y = _dot(xb, w_ref[...])
        tm = y.shape[0]
        for gi, (_, d) in enumerate(C_PATTERNS):
            for h in range(C_GW // LANES):
                c = gi * (C_GW // LANES) + h
                t = y[:, c * LANES:(c + 1) * LANES]
                t = rope(t) if use_rope else t
                if d == 1:
                    outs[gi][:, h * LANES:(h + 1) * LANES] = t.astype(BF16)
                    continue
                sl = slot + (gi - 1) * 2 + h
                perm_ref[sl] = t
                for r in range(d):
                    lo = r * C_GW + h * LANES
                    outs[gi][:, lo:lo + LANES] = perm_ref.at[sl][pl.ds(r, tm // d, stride=d), :].astype(BF16)

    dilated(w_qc, (o_qc0, o_qc1, o_qc2), 0, True)
    dilated(w_kc, (o_kc0, o_kc1, o_kc2), 4, True)
    dilated(w_vc, (o_vc0, o_vc1, o_vc2), 8, False)
    z =_dot(xb, w_fb[...]) + bf_ref[...]
    o_lf[...] = jnp.minimum(z, 0.0) - jnp.log(1.0 + jnp.exp(-jnp.abs(z)))


def _inproj(x2d, pos, inv, bfp, ws, tm):
    t = x2d.shape[0]
    row = lambda n: pl.BlockSpec((tm, n), lambda i: (i, 0))
    tshape = lambda rows: jax.ShapeDtypeStruct((t // FLASH_BK, rows, FLASH_BK), BF16)
    tspec = lambda rows: pl.BlockSpec((tm // FLASH_BK, rows, FLASH_BK), lambda i: (i, 0, 0))
    rshape = lambda w: jax.ShapeDtypeStruct((t, w), BF16)
    out_shape = [rshape(A_W), rshape(A_W), tshape(VA_ROWS), rshape(B_W), rshape(B_W), tshape(VB_ROWS)]
    out_shape.append(jax.ShapeDtypeStruct((t, LANES), F32))
    out_specs = [row(A_W), row(A_W), tspec(VA_ROWS), row(B_W), row(B_W), tspec(VB_ROWS), row(LANES)]
    for _ in range(3):
        for _, d in C_PATTERNS:
            out_shape.append(jax.ShapeDtypeStruct((t // d, d * C_GW), BF16))
            out_specs.append(pl.BlockSpec((tm // d, d * C_GW), lambda i: (i, 0)))
    in_specs = [row(D_MODEL), row(1), _const_spec((1, LANES)), _const_spec((1, LANES))]
    in_specs += [_const_spec(w.shape) for w in ws]
    n_perm = 3 * (C_GROUPS - 1) * (C_GW // LANES)
    return pl.pallas_call(
        _inproj_kernel, out_shape=out_shape, grid=(t // tm,), in_specs=in_specs, out_specs=out_specs,
        scratch_shapes=[pltpu.VMEM((n_perm, tm, LANES), F32)],
        compiler_params=_params(("parallel",)), name="inproj")(x2d, pos, inv, bfp, *ws)


BIAS_PARTS = 3


def _split3(v):
    p0 = v.astype(BF16)
    r1 = v - p0.astype(F32)
    p1 = r1.astype(BF16)
    p2 = (r1 - p1.astype(F32)).astype(BF16)
    return p0, p1, p2


def _fox_keys_kernel(k_ref, lf_ref, tri_ref, sel_ref, o_ref, carry_ref):
    @pl.when(pl.program_id(1) == 0)
    def _reset():
        carry_ref[...] = jnp.zeros_like(carry_ref)

    tk = lf_ref.shape[0]
    tri = tri_ref[...]
    c = carry_ref[...] + sum(_dot(tri, part) for part in _split3(lf_ref[...]))
    carry_ref[...] = c[tk - 1:tk, :]
    extra = _dot(jnp.concatenate(_split3(-LOG2E * c), axis=1), sel_ref[...])
    low = lax.broadcasted_iota(jnp.int32, (tk, LANES), 1) < HEAD_DIM
    for h in range(B_HEADS):
        pair = k_ref[:, (h // 2) * LANES:(h // 2 + 1) * LANES].astype(F32)
        own = low if h % 2 == 0 else jnp.logical_not(low)
        sl = slice(h * LANES, (h + 1) * LANES)
        o_ref[:, sl] = jnp.where(own, pair, extra[:, sl]).astype(BF16)


def _fox_keys(k3d, lf3d, tk):
    b, s, w = k3d.shape
    tri = jnp.asarray(np.tril(np.ones((tk, tk), np.float32)), BF16)
    sel = np.zeros((BIAS_PARTS * LANES, B_HEADS * LANES), np.float32)
    for part in range(BIAS_PARTS):
        for head in range(B_HEADS):
            sel[part * LANES + head, head * LANES + (HEAD_DIM if head % 2 == 0 else 0) + part] = 1.0
    sel = jnp.asarray(sel, BF16)
    blk = lambda n: pl.BlockSpec((None, tk, n), lambda bi, j: (bi, j, 0))
    return pl.pallas_call(
        _fox_keys_kernel, out_shape=jax.ShapeDtypeStruct((b, s, B_HEADS * LANES), BF16),
        grid=(b, s // tk), in_specs=[blk(w), blk(LANES), _const_spec(tri.shape), _const_spec(sel.shape)],
        out_specs=blk(B_HEADS * LANES), scratch_shapes=[pltpu.VMEM((1, LANES), F32)],
        compiler_params=_params(("parallel", "arbitrary")), name="fox_keys")(k3d, lf3d, tri, sel)


def _flash_kernel(forget, bq, *refs):
    bk = FLASH_BK
    if forget:
        q_ref, k_ref, vt_ref, o_ref, m_ref, acc_ref, st_ref = refs
    else:
        q_ref, k_ref, vt_ref, lam_ref, gain_ref, o_ref, m_ref, acc_ref, st_ref = refs
    dvp = acc_ref.shape[3]
    dv = dvp - ONES_ROWS
    nq = q_ref.shape[0] // bq
    halves = bq // bk
    lane = lax.broadcasted_iota(jnp.int32, (bq, LANES), 1)
    low = lane < HEAD_DIM
    if forget:
        fill = (jnp.where(lane < HEAD_DIM + BIAS_PARTS, 1.0, 0.0), jnp.where(lane < BIAS_PARTS, 1.0, 0.0))
    else:
        fill = (0.0, 0.0)
    key = lax.broadcasted_iota(jnp.int32, (bk, bk), 0)
    qry = lax.broadcasted_iota(jnp.int32, (bk, bk), 1)

    m_ref[...] = jnp.full(m_ref.shape, NEG, F32)
    acc_ref[...] = jnp.zeros(acc_ref.shape, F32)
    qts = {}

    def queries(i):
        if i not in qts:
            q = q_ref[i * bq:(i + 1) * bq, :].astype(F32)
            qts[i] = (jnp.where(low, q, fill[0]).T.astype(BF16), jnp.where(low, fill[1], q).T.astype(BF16))
        return qts[i]

    def first_half(i, c):
        return 1 if c == halves * i + 1 else 0

    def scores(i, c, buf):
        for s in range(2):
            kc = k_ref[c * bk:(c + 1) * bk, s * LANES:(s + 1) * LANES] if forget else k_ref[c * bk:(c + 1) * bk, :]
            for h in range(first_half(i, c), halves):
                st_ref[buf, s, h] = _dot(kc, queries(i)[s][:, h * bk:(h + 1) * bk])

    def absorb(i, c, buf):
        diagonal = c >= halves * i
        for s in range(2):
            for h in range(first_half(i, c), halves):
                st = st_ref[buf, s, h]
                if diagonal and h == first_half(i, c):
                    st = jnp.where(key <= qry, st, NEG)
                m_prev = m_ref[i, s, h]
                m_new = jnp.maximum(m_prev, jnp.max(st, axis=0, keepdims=True))
                a = jnp.exp2(m_prev - m_new)
                p = jnp.exp2(st - m_new).astype(BF16)
                m_ref[i, s, h] = m_new
                vt = vt_ref[c, s * dvp:(s + 1) * dvp, :] if forget else vt_ref[c]
                acc_ref[i, s, h] = a * acc_ref[i, s, h] + _dot(vt, p)

    def finish(i):
        for h in range(halves):
            o0 = acc_ref[i, 0, h, :dv, :] / acc_ref[i, 0, h, dv:dv + 1, :]
            o1 = acc_ref[i, 1, h, :dv, :] / acc_ref[i, 1, h, dv:dv + 1, :]
            rows = slice(i * bq + h * bk, i * bq + (h + 1) * bk)
            if forget:
                o_ref[rows, :] = jnp.concatenate([o0, o1], axis=0).T.astype(o_ref.dtype)
            else:
                o = (o0 - lam_ref[:, 0:1] * o1).T
                ms = jnp.mean(o * o, axis=-1, keepdims=True)
                o_ref[rows, :] = (o * lax.rsqrt(ms + EPS) * gain_ref[...]).astype(o_ref.dtype)

    units = [(i, c) for i in range(nq) for c in range(halves * i + 2)]
    nbuf = st_ref.shape[0]
    for n in range(min(nbuf - 1, len(units))):
        scores(*units[n], n % nbuf)
    for n, (i, c) in enumerate(units):
        if n + nbuf - 1 < len(units):
            scores(*units[n + nbuf - 1], (n + nbuf - 1) % nbuf)
        absorb(i, c, n % nbuf)
        if c == halves * i + 1:
            finish(i)


def _flash(q, k, vt, extra, forget, bq):
    b, s, w = q.shape
    g = w // LANES
    nk = s // FLASH_BK
    dvp = (HEAD_DIM if forget else LANES) + ONES_ROWS
    rows = vt.shape[1] // g
    kw = k.shape[2] // g
    qspec = pl.BlockSpec((None, s, LANES), lambda bi, gi: (bi, 0, gi))
    kspec = pl.BlockSpec((None, s, kw), lambda bi, gi: (bi, 0, gi))
    vspec = pl.BlockSpec((nk, rows, FLASH_BK), lambda bi, gi: (bi, gi, 0))
    assert bq == 2 * FLASH_BK
    bk = FLASH_BK
    nq = s // bq
    scratch = [pltpu.VMEM((nq, 2, 2, 1, bk), F32), pltpu.VMEM((nq, 2, 2, dvp, bk), F32),
               pltpu.VMEM((3, 2, 2, bk, bk), F32)]
    especs = [] if forget else [_const_spec((1, LANES)), _const_spec((1, LANES))]
    return pl.pallas_call(
        functools.partial(_flash_kernel, forget, bq),
        out_shape=jax.ShapeDtypeStruct((b, s, w), BF16),
        grid=(b, g), in_specs=[qspec, kspec, vspec] + especs, out_specs=qspec,
        scratch_shapes=scratch,
        compiler_params=_params(("parallel", "parallel")),
        name="fox_attn" if forget else "diff_attn")(q, k, vt, *extra)


def _dilated_kernel(nq, q_ref, kp_ref, kc_ref, vp_ref, vc_ref, o_ref, lse_ref):
    n = C_BLOCK
    pairs = q_ref.shape[1] // LANES
    j = pl.program_id(2)
    kall = jnp.concatenate([kp_ref[...], kc_ref[...]], axis=0)
    vall = jnp.concatenate([vp_ref[...], vc_ref[...]], axis=0)
    qi = lax.broadcasted_iota(jnp.int32, (n, 2 * n), 0)
    kj = lax.broadcasted_iota(jnp.int32, (n, 2 * n), 1)
    dist = n + qi - kj
    window = (dist >= 0) & (dist <= n)
    first_key = jnp.where(j > 0, 0, n)
    lane = lax.broadcasted_iota(jnp.int32, (n, LANES), 1)
    low = lane < HEAD_DIM
    units = [(t, p, s) for t in range(nq) for p in range(pairs) for s in range(2)]
    scores = {}
    for t, p, s in units:
        sl = slice(p * LANES, (p + 1) * LANES)
        q128 = q_ref[t * n:(t + 1) * n, sl]
        zero = jnp.zeros_like(q128)
        qm = jnp.where(low, q128, zero) if s == 0 else jnp.where(low, zero, q128)
        mask = (window & (kj >= first_key)) if t == 0 else window
        scores[t, p, s] = jnp.where(mask, _dot_nt(qm, kall[t * n:(t + 2) * n, sl]), NEG)
    probs = {}
    for u in units:
        m = jnp.max(scores[u], axis=-1, keepdims=True)
        probs[u] = (jnp.exp2(scores[u] - m).astype(BF16), m)
    low_keys = lax.broadcasted_iota(jnp.int32, (2 * n, LANES), 1) < HEAD_DIM
    for t in range(nq):
        for p in range(pairs):
            sl = slice(p * LANES, (p + 1) * LANES)
            v128 = vall[t * n:(t + 2) * n, sl]
            one = jnp.ones_like(v128)
            os, ls = [], []
            for s in range(2):
                vs = jnp.where(low_keys, v128, one) if s == 0 else jnp.where(low_keys, one, v128)
                num = _dot(probs[t, p, s][0], vs)
                den = pltpu.roll(num, HEAD_DIM, 1)
                os.append(num / den)
                ls.append(probs[t, p, s][1] + jnp.log2(den))
            o_ref[t * n:(t + 1) * n, sl] = jnp.where(low, os[0], os[1])
            lse_ref[t * n:(t + 1) * n, sl] = jnp.where(low, ls[0], ls[1])


def _dilated(q, k, v, b, dilation, nq, nr):
    w = C_GW * nr
    n = C_BLOCK
    m_len = q.shape[0] // b
    nb = m_len // (n * nq)
    view = lambda t: t.reshape(b, m_len, dilation * C_GW)
    cur = pl.BlockSpec((None, n * nq, w), lambda bi, r, j: (bi, j, r))
    prev = pl.BlockSpec((None, n, w), lambda bi, r, j: (bi, jnp.maximum(j * nq - 1, 0), r))
    o, lse = pl.pallas_call(
        functools.partial(_dilated_kernel, nq),
        out_shape=[jax.ShapeDtypeStruct((b, m_len, dilation * C_GW), F32)] * 2,
        grid=(b, dilation // nr, nb), in_specs=[cur, prev, cur, prev, cur], out_specs=[cur, cur],
        compiler_params=_params(("parallel", "parallel", "arbitrary")),
        name=f"dilated_d{dilation}")(view(q), view(k), view(k), view(v), view(v))
    return o.reshape(b * m_len, dilation * C_GW), lse.reshape(b * m_len, dilation * C_GW)


def _merge_kernel(x_ref, oa_ref, ob_ref, oc0, oc1, oc2, ls0, ls1, ls2,
                  wg_ref, bg_ref, wa_ref, wb_ref, wc_ref, wo_ref, g_ref, b_ref, o_ref, perm_ref):
    x = x_ref[...]
    xb = x.astype(BF16)
    tm = x.shape[0]

    def token_major(blk_ref, gi, slot):
        d = C_PATTERNS[gi][1]
        if d == 1:
            return blk_ref[...]
        for h in range(C_GW // LANES):
            for r in range(d):
                lo = r * C_GW + h * LANES
                perm_ref.at[slot + h][pl.ds(r, tm // d, stride=d), :] = blk_ref[:, lo:lo + LANES]
        return jnp.concatenate([perm_ref[slot + h] for h in range(C_GW // LANES)], axis=1)

    os = [token_major(r, gi, 2 * gi) for gi, r in enumerate((oc0, oc1, oc2))]
    ls = [token_major(r, gi, 2 * C_GROUPS + 2 * gi) for gi, r in enumerate((ls0, ls1, ls2))]
    for r0 in range(0, tm, SUB_ROWS):
        rows = slice(r0, r0 + SUB_ROWS)
        l0, l1, l2 = (l[rows] for l in ls)
        m = jnp.maximum(jnp.maximum(l0, l1), l2)
        e0, e1, e2 = jnp.exp2(l0 - m), jnp.exp2(l1 - m), jnp.exp2(l2 - m)
        oc = (e0 * os[0][rows] + e1 * os[1][rows] + e2 * os[2][rows]) / (e0 + e1 + e2)
        branches = ((oa_ref[rows, :], wa_ref), (ob_ref[rows, :], wb_ref), (oc.astype(BF16), wc_ref))
        merged = None
        for k, (ob, w_ref) in enumerate(branches):
            sl = slice(k * D_MODEL, (k + 1) * D_MODEL)
            z = _dot(xb[rows], wg_ref[:, sl]) + bg_ref[:, sl]
            gate = 1.0 / (1.0 + jnp.exp(-z))
            term = gate * _dot(ob, w_ref[...])
            merged = term if merged is None else merged + term
        h = _dot(merged.astype(BF16), wo_ref[...])
        o_ref[rows, :] = _layer_norm(DN_ALPHA * x[rows] + h, g_ref[...], b_ref[...])


def _merge(x2d, oa, ob, ocs, lses, wg, bg, wa, wb, wc, wo, g, bta, tm):
    t = x2d.shape[0]
    row = lambda n: pl.BlockSpec((tm, n), lambda i: (i, 0))
    consts = (wg, bg, wa, wb, wc, wo, g, bta)
    cspecs = [pl.BlockSpec((tm // d, d * C_GW), lambda i: (i, 0)) for _, d in C_PATTERNS]
    in_specs = [row(D_MODEL), row(A_W), row(B_W)] + cspecs * 2 + [_const_spec(c.shape) for c in consts]
    return pl.pallas_call(
        _merge_kernel, out_shape=jax.ShapeDtypeStruct((t, D_MODEL), F32), grid=(t // tm,),
        in_specs=in_specs, out_specs=row(D_MODEL),
        scratch_shapes=[pltpu.VMEM((4 * C_GROUPS, tm, LANES), F32)],
        compiler_params=_params(("parallel",)), name="merge")(x2d, oa, ob, *ocs, *lses, *consts)


def _memkv_kernel(m_ref, wk_ref, wv_ref, k_ref, v_ref):
    mb = m_ref[...].astype(BF16)
    k_ref[...] = _dot(mb, wk_ref[...]).astype(BF16)
    v_ref[...] = _dot(mb, wv_ref[...]).astype(BF16)


def _memkv(mem2d, wk, wv, tm):
    t = mem2d.shape[0]
    row = pl.BlockSpec((tm, D_MODEL), lambda i: (i, 0))
    return pl.pallas_call(
        _memkv_kernel, out_shape=[jax.ShapeDtypeStruct((t, D_MODEL), BF16)] * 2, grid=(t // tm,),
        in_specs=[row, _const_spec(wk.shape), _const_spec(wv.shape)], out_specs=[row, row],
        compiler_params=_params(("parallel",)), name="memkv")(mem2d, wk, wv)


def _xattn_kernel(x_ref, k_ref, v_ref, wq_ref, wo_ref, g_ref, b_ref, o_ref):
    for r0 in range(0, x_ref.shape[0], SUB_ROWS):
        rows = slice(r0, r0 + SUB_ROWS)
        x = x_ref[rows, :]
        q = _dot(x.astype(BF16), wq_ref[...]).astype(BF16)
        heads = []
        for h in range(X_HEADS):
            sl = slice(h * X_HEAD_DIM, (h + 1) * X_HEAD_DIM)
            st = _dot_nt(q[:, sl], k_ref[:, sl])
            m = jnp.max(st, axis=-1, keepdims=True)
            e = jnp.exp(st - m)
            l = jnp.sum(e, axis=-1, keepdims=True)
            heads.append((_dot(e.astype(BF16), v_ref[:, sl]) / l).astype(BF16))
        o = jnp.concatenate(heads, axis=-1)
        h_out = _dot(o, wo_ref[...])
        o_ref[rows, :] = _layer_norm(DN_ALPHA * x + h_out, g_ref[...], b_ref[...])


def _xattn(x3d, k3d, v3d, wq, wo, g, bta, tm):
    b, s, d = x3d.shape
    xspec = pl.BlockSpec((None, tm, d), lambda bi, i: (bi, i, 0))
    mspec = pl.BlockSpec((None, MEM_LEN, d), lambda bi, i: (bi, 0, 0))
    consts = (wq, wo, g, bta)
    return pl.pallas_call(
        _xattn_kernel, out_shape=jax.ShapeDtypeStruct((b, s, d), F32), grid=(b, s // tm),
        in_specs=[xspec, mspec, mspec] + [_const_spec(c.shape) for c in consts], out_specs=xspec,
        compiler_params=_params(("parallel", "parallel")), name="xattn")(x3d, k3d, v3d, *consts)


MOE_EPB = 2
ROUTE_LANE0 = N_GROUPS


def _moe_kernel(x_ref, wrh_ref, wrl_ref, br_ref, wg_ref, wu_ref, wd_ref, g_ref, b_ref, o_ref,
                xb_ref, gate_ref):
    tm = x_ref.shape[0]
    lane = lax.broadcasted_iota(jnp.int32, (tm, LANES), 1)
    x = x_ref[...]
    lanef = lane.astype(F32)
    xh = x.astype(BF16)
    xl = (x - xh.astype(F32)).astype(BF16)
    xb_ref[...] = xh
    hh_hl = _dot(xh, jnp.concatenate([wrh_ref[...], wrl_ref[...]], axis=1))
    logits = (hh_hl[:, :LANES] + _dot(xl, wrh_ref[...]) + hh_hl[:, LANES:]) + br_ref[...]
    gl = jnp.where(lane < N_GROUPS, logits, NEG)
    gmax = jnp.max(gl, axis=-1, keepdims=True)
    g_gate = 1.0 / jnp.sum(jnp.exp(gl - gmax), axis=-1, keepdims=True)
    g_idx = jnp.min(jnp.where(gl == gmax, lanef, float(LANES)), axis=-1, keepdims=True)
    e_group = ((lane - ROUTE_LANE0) >> 2).astype(F32)
    sel = (lane >= ROUTE_LANE0) & (e_group == g_idx)
    el = jnp.where(sel, logits, NEG)
    top1 = jnp.max(el, axis=-1, keepdims=True)
    i1 = jnp.min(jnp.where(el == top1, lanef, float(LANES)), axis=-1, keepdims=True)
    el2 = jnp.where(lanef == i1, NEG, el)
    top2 = jnp.max(el2, axis=-1, keepdims=True)
    i2 = jnp.min(jnp.where(el2 == top2, lanef, float(LANES)), axis=-1, keepdims=True)
    r = jnp.exp(top2 - top1)
    w1 = 1.0 / (1.0 + r)
    w2 = r * w1
    gate_ref[...] = jnp.where(lanef == i1, w1, jnp.where(lanef == i2, w2, 0.0)) * g_gate
    o_ref[...] = jnp.zeros_like(o_ref)

    def experts(eb, carry):
        xb = xb_ref[...]
        gates = gate_ref[...]
        for k in range(MOE_EPB):
            e = eb * MOE_EPB + k
            gcol = jnp.sum(jnp.where(lane == ROUTE_LANE0 + e, gates, 0.0), axis=-1, keepdims=True)
            a = _dot(xb, wg_ref[e])
            u = _dot(xb, wu_ref[e])
            h = (a / (1.0 + jnp.exp(-a))) * u * gcol
            o_ref[...] += _dot(h.astype(BF16), wd_ref[e])
        return carry

    lax.fori_loop(0, N_EXPERTS // MOE_EPB, experts, 0)
    o_ref[...] = _layer_norm(DN_ALPHA * x_ref[...] + o_ref[...], g_ref[...], b_ref[...])


def _moe(x2d, wrh, wrl, br, wg, wu, wd, g, bta, tm):
    t = x2d.shape[0]
    row = pl.BlockSpec((tm, D_MODEL), lambda i: (i, 0))
    consts = (wrh, wrl, br, wg, wu, wd, g, bta)
    return pl.pallas_call(
        _moe_kernel, out_shape=jax.ShapeDtypeStruct((t, D_MODEL), F32),
        grid=(t // tm,), in_specs=[row] + [_const_spec(c.shape) for c in consts], out_specs=row,
        scratch_shapes=[pltpu.VMEM((tm, D_MODEL), BF16), pltpu.VMEM((tm, LANES), F32)],
        compiler_params=_params(("parallel",)), name="moe")(x2d, *consts)


def _pad_cols(w, n):
    return jnp.pad(w, ((0, 0), (0, n - w.shape[1])))


def kernel(x, mem, positions, w_in, b_forget, diff_lambda, diff_subln, w_branch_a, w_branch_b, w_branch_c,
           w_gate, b_gate, w_out, w_xq, w_xk, w_xv, w_xo, w_route_group, b_route_group, w_route_expert,
           b_route_expert, w_expert_gate, w_expert_up, w_expert_down, ln_g, ln_b):
    b, s, d = x.shape
    t = b * s
    depth = w_in.shape[0]
    scale = HEAD_DIM ** -0.5

    half = HEAD_DIM // 2
    inv = ROPE_THETA ** (-jnp.arange(half, dtype=F32) / half)
    inv_row = jnp.tile(inv, LANES // half).reshape(1, LANES)
    pos = positions.astype(F32).reshape(t, 1)

    mem2d = mem.reshape(b * MEM_LEN, d)
    x2d = x.reshape(t, d)
    row = lambda v: v.reshape(1, -1).astype(F32)
    for l in range(depth):
        lam_init = 0.8 - 0.6 * math.exp(-0.3 * l)
        offs = [0, A_W, 2 * A_W, 3 * A_W, 3 * A_W + B_W, 3 * A_W + 2 * B_W, 3 * A_W + 3 * B_W]
        offs.append(offs[-1] + B_HEADS)
        offs += [offs[-1] + C_W, offs[-1] + 2 * C_W, offs[-1] + 3 * C_W]
        wl = w_in[l]
        seg = [wl[:, offs[k]:offs[k + 1]] for k in range(10)]
        qa, ka, va, qb, kb, vb, fb, qc, kc, vc = seg
        ws = [(qa * (scale * LOG2E)), ka, va.T, (qb * (scale * LOG2E)), kb, vb.T, _pad_cols(fb, LANES),
              (qc * (scale * LOG2E)), kc, vc]
        ws = [w.astype(BF16) for w in ws]
        bfp = _pad_cols(row(b_forget[l]), LANES)

        outs = _inproj(x2d, pos, inv_row, bfp, ws, tm=512)
        o_qa, o_ka, o_va, o_qb, o_kb, o_vb, o_lf = outs[:7]
        o_qc, o_kc, o_vc = outs[7:10], outs[10:13], outs[13:16]

        lf = diff_lambda[l].astype(F32)
        lam = jnp.exp(jnp.dot(lf[0], lf[1])) - jnp.exp(jnp.dot(lf[2], lf[3])) + lam_init
        lam_row = jnp.full((1, LANES), lam, F32)
        gain = row(diff_subln[l]) * (1.0 - lam_init)
        r3 = lambda v: v.reshape(b, s, v.shape[-1])
        oa = _flash(r3(o_qa), r3(o_ka), o_va, (lam_row, gain), forget=False, bq=2 * FLASH_BK)
        kb_aug = _fox_keys(r3(o_kb), r3(o_lf), tk=512)
        ob = _flash(r3(o_qb), kb_aug, o_vb, (), forget=True, bq=2 * FLASH_BK)
        ocs, lses = [], []
        for gi, (_, dil) in enumerate(C_PATTERNS):
            nq = min(4, s // (dil * C_BLOCK))
            o_g, lse_g = _dilated(o_qc[gi], o_kc[gi], o_vc[gi], b, dil, nq, nr=min(dil, 4 // nq))
            ocs.append(o_g)
            lses.append(lse_g)
        x2d = _merge(x2d, oa.reshape(t, A_W), ob.reshape(t, B_W), ocs, lses,
                     w_gate[l].astype(BF16), row(b_gate[l]), w_branch_a[l].astype(BF16),
                     w_branch_b[l].astype(BF16), w_branch_c[l].astype(BF16), w_out[l].astype(BF16),
                     row(ln_g[l, 0]), row(ln_b[l, 0]), tm=512)
        xscale = X_HEAD_DIM ** -0.5
        mk, mv = _memkv(mem2d, w_xk[l].astype(BF16), w_xv[l].astype(BF16), tm=MEM_LEN)
        x2d = _xattn(x2d.reshape(b, s, d), mk.reshape(b, MEM_LEN, d), mv.reshape(b, MEM_LEN, d),
                     (w_xq[l] * xscale).astype(BF16), w_xo[l].astype(BF16),
                     row(ln_g[l, 1]), row(ln_b[l, 1]), tm=512).reshape(t, d)
        wr = jnp.concatenate([w_route_group[l]] + [w_route_expert[l, gi] for gi in range(N_GROUPS)], axis=1)
        wr = _pad_cols(wr.astype(F32), LANES)
        wrh = wr.astype(BF16)
        wrl = (wr - wrh.astype(F32)).astype(BF16)
        br = _pad_cols(row(jnp.concatenate([b_route_group[l], b_route_expert[l].reshape(-1)])), LANES)
        ne = (N_EXPERTS,)
        x2d = _moe(x2d, wrh, wrl, br,
                   w_expert_gate[l].reshape(ne + w_expert_gate.shape[3:]).astype(BF16),
                   w_expert_up[l].reshape(ne + w_expert_up.shape[3:]).astype(BF16),
                   w_expert_down[l].reshape(ne + w_expert_down.shape[3:]).astype(BF16),
                   row(ln_g[l, 2]), row(ln_b[l, 2]), tm=1024)
    return x2d.reshape(b, s, d)
```

```python
import functools
import math

import jax
import jax.numpy as jnp
import numpy as np
from jax import lax
from jax.experimental import pallas as pl
from jax.experimental.pallas import tpu as pltpu

F32 = jnp.float32
BF16 = jnp.bfloat16

D_MODEL = 1024
HEAD_DIM = 64
ROPE_THETA = 10000.0
EPS = 1e-5
MEM_LEN = 256
A_HEADS = 4
B_HEADS = 8
C_PATTERNS = ((128, 1), (512, 4), (2048, 16))
C_GROUPS = 3
C_SLOTS = 4
C_BLOCK = 128
X_HEADS = 4
X_HEAD_DIM = D_MODEL // X_HEADS
N_GROUPS = 4
EXPERTS_PER_GROUP = 4
N_EXPERTS = N_GROUPS * EXPERTS_PER_GROUP
D_EXPERT = D_MODEL // 4
DEPTH = 2
DN_ALPHA = (2 * DEPTH) ** 0.25

LANES = 128
NEG = -1e30
LOG2E = math.log2(math.e)
VMEM_LIMIT = 56 * 1024 * 1024

A_W = A_HEADS * 2 * HEAD_DIM
B_W = B_HEADS * HEAD_DIM
C_GW = C_SLOTS * HEAD_DIM
C_W = C_GROUPS * C_GW
ONES_ROWS = 16
VA_ROWS = A_HEADS * (2 * HEAD_DIM + ONES_ROWS)
VB_ROWS = B_HEADS * (HEAD_DIM + ONES_ROWS)
SUB_ROWS = 256
FLASH_BK = 512


def _params(sem, flags=None):
    return pltpu.CompilerParams(dimension_semantics=sem, vmem_limit_bytes=VMEM_LIMIT, flags=flags)


def _const_spec(shape):
    nd = len(shape)
    return pl.BlockSpec(shape, lambda *_: (0,) * nd, pipeline_mode=pl.Buffered(1))


def _layer_norm(y, g, b):
    mu = jnp.mean(y, axis=-1, keepdims=True)
    yc = y - mu
    var = jnp.mean(yc * yc, axis=-1, keepdims=True)
    return yc * lax.rsqrt(var + EPS) * g + b


def _dot(a, b):
    return jnp.dot(a, b, preferred_element_type=F32)


def _dot_nt(a, b):
    return lax.dot_general(a, b, (((1,), (1,)), ((), ())), preferred_element_type=F32)


def _inproj_kernel(x_ref, pos_ref, inv_ref, bf_ref,
                   w_qa, w_ka, w_va, w_qb, w_kb, w_vb, w_fb, w_qc, w_kc, w_vc,
                   o_qa, o_ka, o_va, o_qb, o_kb, o_vb, o_lf,
                   o_qc0, o_qc1, o_qc2, o_kc0, o_kc1, o_kc2, o_vc0, o_vc1, o_vc2, perm_ref):
    xb = x_ref[...].astype(BF16)
    ang = pos_ref[...] * inv_ref[...]
    lane = lax.broadcasted_iota(jnp.int32, ang.shape, 1)
    lower = (lane & (HEAD_DIM - 1)) < HEAD_DIM // 2
    cos = jnp.cos(ang)
    sin = jnp.sin(ang)
    sin = jnp.where(lower, -sin, sin)

    def rope(t):
        sw = jnp.where(lower, pltpu.roll(t, LANES - HEAD_DIM // 2, 1), pltpu.roll(t, HEAD_DIM // 2, 1))
        return t * cos + sw * sin

    def roped(w_ref, outs, width):
        y = _dot(xb, w_ref[...])
        per_out = width // LANES
        for c in range(y.shape[1] // LANES):
            o = outs[c // per_out]
            lo = (c % per_out) * LANES
            o[:, lo:lo + LANES] = rope(y[:, c * LANES:(c + 1) * LANES]).astype(BF16)

    def plain(w_ref, outs, width):
        y = _dot(xb, w_ref[...])
        for c, o in enumerate(outs):
            o[...] = y[:, c * width:(c + 1) * width].astype(BF16)

    def transposed(wt_ref, o, width):
        yt = _dot_nt(wt_ref[...], xb)
        ones = jnp.ones((ONES_ROWS, FLASH_BK), F32)
        for c in range(o.shape[0]):
            cols = slice(c * FLASH_BK, (c + 1) * FLASH_BK)
            parts = []
            for r0 in range(0, yt.shape[0], width):
                parts += [yt[r0:r0 + width, cols], ones]
            o[c] = jnp.concatenate(parts, axis=0).astype(BF16)

    roped(w_qa, (o_qa,), A_W)
    roped(w_ka, (o_ka,), A_W)
    transposed(w_va, o_va, LANES)
    plain(w_qb, (o_qb,), B_W)
    plain(w_kb, (o_kb,), B_W)
    transposed(w_vb, o_vb, HEAD_DIM)
    def dilated(w_ref, outs, slot, use_rope):
        y = _dot(xb, w_ref[...])
        tm = y.shape[0]
        for gi, (_, d) in enumerate(C_PATTERNS):
            for h in range(C_GW // LANES):
                c = gi * (C_GW // LANES) + h
                t = y[:, c * LANES:(c + 1) * LANES]
                t = rope(t) if use_rope else t
                if d == 1:
                    outs[gi][:, h * LANES:(h + 1) * LANES] = t.astype(BF16)
                    continue
                sl = slot + (gi - 1) * 2 + h
                perm_ref[sl] = t
                for r in range(d):
                    lo = r * C_GW + h * LANES
                    outs[gi][:, lo:lo + LANES] = perm_ref.at[sl][pl.ds(r, tm // d, stride=d), :].astype(BF16)

    dilated(w_qc, (o_qc0, o_qc1, o_qc2), 0, True)
    dilated(w_kc, (o_kc0, o_kc1, o_kc2), 4, True)
    dilated(w_vc, (o_vc0, o_vc1, o_vc2), 8, False)
    z =_dot(xb, w_fb[...]) + bf_ref[...]
    o_lf[...] = jnp.minimum(z, 0.0) - jnp.log(1.0 + jnp.exp(-jnp.abs(z)))


def _inproj(x2d, pos, inv, bfp, ws, tm):
    t = x2d.shape[0]
    row = lambda n: pl.BlockSpec((tm, n), lambda i: (i, 0))
    tshape = lambda rows: jax.ShapeDtypeStruct((t // FLASH_BK, rows, FLASH_BK), BF16)
    tspec = lambda rows: pl.BlockSpec((tm // FLASH_BK, rows, FLASH_BK), lambda i: (i, 0, 0))
    rshape = lambda w: jax.ShapeDtypeStruct((t, w), BF16)
    out_shape = [rshape(A_W), rshape(A_W), tshape(VA_ROWS), rshape(B_W), rshape(B_W), tshape(VB_ROWS)]
    out_shape.append(jax.ShapeDtypeStruct((t, LANES), F32))
    out_specs = [row(A_W), row(A_W), tspec(VA_ROWS), row(B_W), row(B_W), tspec(VB_ROWS), row(LANES)]
    for _ in range(3):
        for _, d in C_PATTERNS:
            out_shape.append(jax.ShapeDtypeStruct((t // d, d * C_GW), BF16))
            out_specs.append(pl.BlockSpec((tm // d, d * C_GW), lambda i: (i, 0)))
    in_specs = [row(D_MODEL), row(1), _const_spec((1, LANES)), _const_spec((1, LANES))]
    in_specs += [_const_spec(w.shape) for w in ws]
    n_perm = 3 * (C_GROUPS - 1) * (C_GW // LANES)
    return pl.pallas_call(
        _inproj_kernel, out_shape=out_shape, grid=(t // tm,), in_specs=in_specs, out_specs=out_specs,
        scratch_shapes=[pltpu.VMEM((n_perm, tm, LANES), F32)],
        compiler_params=_params(("parallel",)), name="inproj")(x2d, pos, inv, bfp, *ws)


BIAS_PARTS = 3


def _split3(v):
    p0 = v.astype(BF16)
    r1 = v - p0.astype(F32)
    p1 = r1.astype(BF16)
    p2 = (r1 - p1.astype(F32)).astype(BF16)
    return p0, p1, p2


def _fox_keys_kernel(k_ref, lf_ref, tri_ref, sel_ref, o_ref, carry_ref):
    @pl.when(pl.program_id(1) == 0)
    def _reset():
        carry_ref[...] = jnp.zeros_like(carry_ref)

    tk = lf_ref.shape[0]
    tri = tri_ref[...]
    c = carry_ref[...] + sum(_dot(tri, part) for part in _split3(lf_ref[...]))
    carry_ref[...] = c[tk - 1:tk, :]
    extra = _dot(jnp.concatenate(_split3(-LOG2E * c), axis=1), sel_ref[...])
    low = lax.broadcasted_iota(jnp.int32, (tk, LANES), 1) < HEAD_DIM
    for h in range(B_HEADS):
        pair = k_ref[:, (h // 2) * LANES:(h // 2 + 1) * LANES].astype(F32)
        own = low if h % 2 == 0 else jnp.logical_not(low)
        sl = slice(h * LANES, (h + 1) * LANES)
        o_ref[:, sl] = jnp.where(own, pair, extra[:, sl]).astype(BF16)


def _fox_keys(k3d, lf3d, tk):
    b, s, w = k3d.shape
    tri = jnp.asarray(np.tril(np.ones((tk, tk), np.float32)), BF16)
    sel = np.zeros((BIAS_PARTS * LANES, B_HEADS * LANES), np.float32)
    for part in range(BIAS_PARTS):
        for head in range(B_HEADS):
            sel[part * LANES + head, head * LANES + (HEAD_DIM if head % 2 == 0 else 0) + part] = 1.0
    sel = jnp.asarray(sel, BF16)
    blk = lambda n: pl.BlockSpec((None, tk, n), lambda bi, j: (bi, j, 0))
    return pl.pallas_call(
        _fox_keys_kernel, out_shape=jax.ShapeDtypeStruct((b, s, B_HEADS * LANES), BF16),
        grid=(b, s // tk), in_specs=[blk(w), blk(LANES), _const_spec(tri.shape), _const_spec(sel.shape)],
        out_specs=blk(B_HEADS * LANES), scratch_shapes=[pltpu.VMEM((1, LANES), F32)],
        compiler_params=_params(("parallel", "arbitrary")), name="fox_keys")(k3d, lf3d, tri, sel)


def _flash_kernel(forget, bq, *refs):
    bk = FLASH_BK
    qw = bk // 2
    if forget:
        q_ref, k_ref, vt_ref, o_ref, m_ref, acc_ref, st_ref = refs
    else:
        q_ref, k_ref, vt_ref, lam_ref, gain_ref, o_ref, m_ref, acc_ref, st_ref = refs
    dvp = acc_ref.shape[3]
    dv = dvp - ONES_ROWS
    nq = q_ref.shape[0] // bq
    parts = bq // qw
    lane = lax.broadcasted_iota(jnp.int32, (bq, LANES), 1)
    low = lane < HEAD_DIM
    if forget:
        fill = (jnp.where(lane < HEAD_DIM + BIAS_PARTS, 1.0, 0.0), jnp.where(lane < BIAS_PARTS, 1.0, 0.0))
    else:
        fill = (0.0, 0.0)
    triangle = (lax.broadcasted_iota(jnp.int32, (qw, qw), 0) <= lax.broadcasted_iota(jnp.int32, (qw, qw), 1))

    m_ref[...] = jnp.full(m_ref.shape, NEG, F32)
    acc_ref[...] = jnp.zeros(acc_ref.shape, F32)
    qts = {}

    def queries(i):
        if i not in qts:
            q = q_ref[i * bq:(i + 1) * bq, :].astype(F32)
            qts[i] = (jnp.where(low, q, fill[0]).T.astype(BF16), jnp.where(low, fill[1], q).T.astype(BF16))
        return qts[i]

    def live_rows(i, c, t):
        qb = i * parts + t
        blocks = max(0, min(bk // qw, qb - c * (bk // qw) + 1))
        return blocks * qw, blocks > 0 and c * (bk // qw) + blocks - 1 == qb

    def scores(i, c, buf):
        for s in range(2):
            for t in range(parts):
                rows, _ = live_rows(i, c, t)
                if rows:
                    kc = k_ref[c * bk:c * bk + rows, s * LANES:(s + 1) * LANES] if forget \
                        else k_ref[c * bk:c * bk + rows, :]
                    st_ref[buf, s, t, :rows, :] = _dot(kc, queries(i)[s][:, t * qw:(t + 1) * qw])

    def absorb(i, c, buf):
        for s in range(2):
            for t in range(parts):
                rows, diagonal = live_rows(i, c, t)
                if not rows:
                    continue
                st = st_ref[buf, s, t, :rows, :]
                if diagonal:
                    tri = jnp.where(triangle, st[rows - qw:], NEG)
                    st = tri if rows == qw else jnp.concatenate([st[:rows - qw], tri], axis=0)
                m_prev = m_ref[i, s, t]
                m_new = jnp.maximum(m_prev, jnp.max(st, axis=0, keepdims=True))
                a = jnp.exp2(m_prev - m_new)
                p = jnp.exp2(st - m_new).astype(BF16)
                m_ref[i, s, t] = m_new
                vt = vt_ref[c, s * dvp:(s + 1) * dvp, :rows] if forget else vt_ref[c, :, :rows]
                acc_ref[i, s, t] = a * acc_ref[i, s, t] + _dot(vt, p)

    def finish(i):
        for t in range(parts):
            o0 = acc_ref[i, 0, t, :dv, :] / acc_ref[i, 0, t, dv:dv + 1, :]
            o1 = acc_ref[i, 1, t, :dv, :] / acc_ref[i, 1, t, dv:dv + 1, :]
            rows = slice(i * bq + t * qw, i * bq + (t + 1) * qw)
            if forget:
                o_ref[rows, :] = jnp.concatenate([o0, o1], axis=0).T.astype(o_ref.dtype)
            else:
                o = (o0 - lam_ref[:, 0:1] * o1).T
                ms = jnp.mean(o * o, axis=-1, keepdims=True)
                o_ref[rows, :] = (o * lax.rsqrt(ms + EPS) * gain_ref[...]).astype(o_ref.dtype)

    units = [(i, c) for i in range(nq) for c in range((i + 1) * bq // bk)]
    nbuf = st_ref.shape[0]
    for n in range(min(nbuf - 1, len(units))):
        scores(*units[n], n % nbuf)
    for n, (i, c) in enumerate(units):
        if n + nbuf - 1 < len(units):
            scores(*units[n + nbuf - 1], (n + nbuf - 1) % nbuf)
        absorb(i, c, n % nbuf)
        if c == (i + 1) * bq // bk - 1:
            finish(i)


def _flash(q, k, vt, extra, forget, bq):
    b, s, w = q.shape
    g = w // LANES
    nk = s // FLASH_BK
    dvp = (HEAD_DIM if forget else LANES) + ONES_ROWS
    rows = vt.shape[1] // g
    kw = k.shape[2] // g
    qspec = pl.BlockSpec((None, s, LANES), lambda bi, gi: (bi, 0, gi))
    kspec = pl.BlockSpec((None, s, kw), lambda bi, gi: (bi, 0, gi))
    vspec = pl.BlockSpec((nk, rows, FLASH_BK), lambda bi, gi: (bi, gi, 0))
    assert bq == 2 * FLASH_BK
    bk = FLASH_BK
    nq = s // bq
    qw, parts = bk // 2, 2 * bq // bk
    scratch = [pltpu.VMEM((nq, 2, parts, 1, qw), F32), pltpu.VMEM((nq, 2, parts, dvp, qw), F32),
               pltpu.VMEM((3, 2, parts, bk, qw), F32)]
    especs = [] if forget else [_const_spec((1, LANES)), _const_spec((1, LANES))]
    return pl.pallas_call(
        functools.partial(_flash_kernel, forget, bq),
        out_shape=jax.ShapeDtypeStruct((b, s, w), BF16),
        grid=(b, g), in_specs=[qspec, kspec, vspec] + especs, out_specs=qspec,
        scratch_shapes=scratch,
        compiler_params=_params(("parallel", "parallel")),
        name="fox_attn" if forget else "diff_attn")(q, k, vt, *extra)


def _dilated_kernel(nq, q_ref, kp_ref, kc_ref, vp_ref, vc_ref, o_ref, lse_ref):
    n = C_BLOCK
    pairs = q_ref.shape[1] // LANES
    j = pl.program_id(2)
    kall = jnp.concatenate([kp_ref[...], kc_ref[...]], axis=0)
    vall = jnp.concatenate([vp_ref[...], vc_ref[...]], axis=0)
    qi = lax.broadcasted_iota(jnp.int32, (n, 2 * n), 0)
    kj = lax.broadcasted_iota(jnp.int32, (n, 2 * n), 1)
    dist = n + qi - kj
    window = (dist >= 0) & (dist <= n)
    first_key = jnp.where(j > 0, 0, n)
    lane = lax.broadcasted_iota(jnp.int32, (n, LANES), 1)
    low = lane < HEAD_DIM
    units = [(t, p, s) for t in range(nq) for p in range(pairs) for s in range(2)]
    scores = {}
    for t, p, s in units:
        sl = slice(p * LANES, (p + 1) * LANES)
        q128 = q_ref[t * n:(t + 1) * n, sl]
        zero = jnp.zeros_like(q128)
        qm = jnp.where(low, q128, zero) if s == 0 else jnp.where(low, zero, q128)
        mask = (window & (kj >= first_key)) if t == 0 else window
        scores[t, p, s] = jnp.where(mask, _dot_nt(qm, kall[t * n:(t + 2) * n, sl]), NEG)
    probs = {}
    for u in units:
        m = jnp.max(scores[u], axis=-1, keepdims=True)
        probs[u] = (jnp.exp2(scores[u] - m).astype(BF16), m)
    low_keys = lax.broadcasted_iota(jnp.int32, (2 * n, LANES), 1) < HEAD_DIM
    for t in range(nq):
        for p in range(pairs):
            sl = slice(p * LANES, (p + 1) * LANES)
            v128 = vall[t * n:(t + 2) * n, sl]
            one = jnp.ones_like(v128)
            os, ls = [], []
            for s in range(2):
                vs = jnp.where(low_keys, v128, one) if s == 0 else jnp.where(low_keys, one, v128)
                num = _dot(probs[t, p, s][0], vs)
                den = pltpu.roll(num, HEAD_DIM, 1)
                os.append(num / den)
                ls.append(probs[t, p, s][1] + jnp.log2(den))
            o_ref[t * n:(t + 1) * n, sl] = jnp.where(low, os[0], os[1])
            lse_ref[t * n:(t + 1) * n, sl] = jnp.where(low, ls[0], ls[1])


def _dilated(q, k, v, b, dilation, nq, nr):
    w = C_GW * nr
    n = C_BLOCK
    m_len = q.shape[0] // b
    nb = m_len // (n * nq)
    view = lambda t: t.reshape(b, m_len, dilation * C_GW)
    cur = pl.BlockSpec((None, n * nq, w), lambda bi, r, j: (bi, j, r))
    prev = pl.BlockSpec((None, n, w), lambda bi, r, j: (bi, jnp.maximum(j * nq - 1, 0), r))
    o, lse = pl.pallas_call(
        functools.partial(_dilated_kernel, nq),
        out_shape=[jax.ShapeDtypeStruct((b, m_len, dilation * C_GW), F32)] * 2,
        grid=(b, dilation // nr, nb), in_specs=[cur, prev, cur, prev, cur], out_specs=[cur, cur],
        compiler_params=_params(("parallel", "parallel", "arbitrary")),
        name=f"dilated_d{dilation}")(view(q), view(k), view(k), view(v), view(v))
    return o.reshape(b * m_len, dilation * C_GW), lse.reshape(b * m_len, dilation * C_GW)


def _merge_kernel(x_ref, oa_ref, ob_ref, oc0, oc1, oc2, ls0, ls1, ls2,
                  wg_ref, bg_ref, wa_ref, wb_ref, wc_ref, wo_ref, g_ref, b_ref, o_ref, perm_ref):
    x = x_ref[...]
    xb = x.astype(BF16)
    tm = x.shape[0]

    def token_major(blk_ref, gi, slot):
        d = C_PATTERNS[gi][1]
        if d == 1:
            return blk_ref[...]
        for h in range(C_GW // LANES):
            for r in range(d):
                lo = r * C_GW + h * LANES
                perm_ref.at[slot + h][pl.ds(r, tm // d, stride=d), :] = blk_ref[:, lo:lo + LANES]
        return jnp.concatenate([perm_ref[slot + h] for h in range(C_GW // LANES)], axis=1)

    os = [token_major(r, gi, 2 * gi) for gi, r in enumerate((oc0, oc1, oc2))]
    ls = [token_major(r, gi, 2 * C_GROUPS + 2 * gi) for gi, r in enumerate((ls0, ls1, ls2))]
    for r0 in range(0, tm, SUB_ROWS):
        rows = slice(r0, r0 + SUB_ROWS)
        l0, l1, l2 = (l[rows] for l in ls)
        m = jnp.maximum(jnp.maximum(l0, l1), l2)
        e0, e1, e2 = jnp.exp2(l0 - m), jnp.exp2(l1 - m), jnp.exp2(l2 - m)
        oc = (e0 * os[0][rows] + e1 * os[1][rows] + e2 * os[2][rows]) / (e0 + e1 + e2)
        branches = ((oa_ref[rows, :], wa_ref), (ob_ref[rows, :], wb_ref), (oc.astype(BF16), wc_ref))
        merged = None
        for k, (ob, w_ref) in enumerate(branches):
            sl = slice(k * D_MODEL, (k + 1) * D_MODEL)
            z = _dot(xb[rows], wg_ref[:, sl]) + bg_ref[:, sl]
            gate = 1.0 / (1.0 + jnp.exp(-z))
            term = gate * _dot(ob, w_ref[...])
            merged = term if merged is None else merged + term
        h = _dot(merged.astype(BF16), wo_ref[...])
        o_ref[rows, :] = _layer_norm(DN_ALPHA * x[rows] + h, g_ref[...], b_ref[...])


def _merge(x2d, oa, ob, ocs, lses, wg, bg, wa, wb, wc, wo, g, bta, tm):
    t = x2d.shape[0]
    row = lambda n: pl.BlockSpec((tm, n), lambda i: (i, 0))
    consts = (wg, bg, wa, wb, wc, wo, g, bta)
    cspecs = [pl.BlockSpec((tm // d, d * C_GW), lambda i: (i, 0)) for _, d in C_PATTERNS]
    in_specs = [row(D_MODEL), row(A_W), row(B_W)] + cspecs * 2 + [_const_spec(c.shape) for c in consts]
    return pl.pallas_call(
        _merge_kernel, out_shape=jax.ShapeDtypeStruct((t, D_MODEL), F32), grid=(t // tm,),
        in_specs=in_specs, out_specs=row(D_MODEL),
        scratch_shapes=[pltpu.VMEM((4 * C_GROUPS, tm, LANES), F32)],
        compiler_params=_params(("parallel",)), name="merge")(x2d, oa, ob, *ocs, *lses, *consts)


def _memkv_kernel(m_ref, wk_ref, wv_ref, k_ref, v_ref):
    mb = m_ref[...].astype(BF16)
    k_ref[...] = _dot(mb, wk_ref[...]).astype(BF16)
    v_ref[...] = _dot(mb, wv_ref[...]).astype(BF16)


def _memkv(mem2d, wk, wv, tm):
    t = mem2d.shape[0]
    row = pl.BlockSpec((tm, D_MODEL), lambda i: (i, 0))
    return pl.pallas_call(
        _memkv_kernel, out_shape=[jax.ShapeDtypeStruct((t, D_MODEL), BF16)] * 2, grid=(t // tm,),
        in_specs=[row, _const_spec(wk.shape), _const_spec(wv.shape)], out_specs=[row, row],
        compiler_params=_params(("parallel",)), name="memkv")(mem2d, wk, wv)


def _xattn_kernel(x_ref, k_ref, v_ref, wq_ref, wo_ref, g_ref, b_ref, o_ref):
    for r0 in range(0, x_ref.shape[0], SUB_ROWS):
        rows = slice(r0, r0 + SUB_ROWS)
        x = x_ref[rows, :]
        q = _dot(x.astype(BF16), wq_ref[...]).astype(BF16)
        heads = []
        for h in range(X_HEADS):
            sl = slice(h * X_HEAD_DIM, (h + 1) * X_HEAD_DIM)
            st = _dot_nt(q[:, sl], k_ref[:, sl])
            m = jnp.max(st, axis=-1, keepdims=True)
            e = jnp.exp(st - m)
            l = jnp.sum(e, axis=-1, keepdims=True)
            heads.append((_dot(e.astype(BF16), v_ref[:, sl]) / l).astype(BF16))
        o = jnp.concatenate(heads, axis=-1)
        h_out = _dot(o, wo_ref[...])
        o_ref[rows, :] = _layer_norm(DN_ALPHA * x + h_out, g_ref[...], b_ref[...])


def _xattn(x3d, k3d, v3d, wq, wo, g, bta, tm):
    b, s, d = x3d.shape
    xspec = pl.BlockSpec((None, tm, d), lambda bi, i: (bi, i, 0))
    mspec = pl.BlockSpec((None, MEM_LEN, d), lambda bi, i: (bi, 0, 0))
    consts = (wq, wo, g, bta)
    return pl.pallas_call(
        _xattn_kernel, out_shape=jax.ShapeDtypeStruct((b, s, d), F32), grid=(b, s // tm),
        in_specs=[xspec, mspec, mspec] + [_const_spec(c.shape) for c in consts], out_specs=xspec,
        compiler_params=_params(("parallel", "parallel")), name="xattn")(x3d, k3d, v3d, *consts)


MOE_EPB = 2
ROUTE_LANE0 = N_GROUPS


def _moe_kernel(x_ref, wrh_ref, wrl_ref, br_ref, wg_ref, wu_ref, wd_ref, g_ref, b_ref, o_ref,
                xb_ref, gate_ref):
    tm = x_ref.shape[0]
    lane = lax.broadcasted_iota(jnp.int32, (tm, LANES), 1)
    x = x_ref[...]
    lanef = lane.astype(F32)
    xh = x.astype(BF16)
    xl = (x - xh.astype(F32)).astype(BF16)
    xb_ref[...] = xh
    hh_hl = _dot(xh, jnp.concatenate([wrh_ref[...], wrl_ref[...]], axis=1))
    logits = (hh_hl[:, :LANES] + _dot(xl, wrh_ref[...]) + hh_hl[:, LANES:]) + br_ref[...]
    gl = jnp.where(lane < N_GROUPS, logits, NEG)
    gmax = jnp.max(gl, axis=-1, keepdims=True)
    g_gate = 1.0 / jnp.sum(jnp.exp(gl - gmax), axis=-1, keepdims=True)
    g_idx = jnp.min(jnp.where(gl == gmax, lanef, float(LANES)), axis=-1, keepdims=True)
    e_group = ((lane - ROUTE_LANE0) >> 2).astype(F32)
    sel = (lane >= ROUTE_LANE0) & (e_group == g_idx)
    el = jnp.where(sel, logits, NEG)
    top1 = jnp.max(el, axis=-1, keepdims=True)
    i1 = jnp.min(jnp.where(el == top1, lanef, float(LANES)), axis=-1, keepdims=True)
    el2 = jnp.where(lanef == i1, NEG, el)
    top2 = jnp.max(el2, axis=-1, keepdims=True)
    i2 = jnp.min(jnp.where(el2 == top2, lanef, float(LANES)), axis=-1, keepdims=True)
    r = jnp.exp(top2 - top1)
    w1 = 1.0 / (1.0 + r)
    w2 = r * w1
    gate_ref[...] = jnp.where(lanef == i1, w1, jnp.where(lanef == i2, w2, 0.0)) * g_gate
    o_ref[...] = jnp.zeros_like(o_ref)

    def experts(eb, carry):
        xb = xb_ref[...]
        gates = gate_ref[...]
        for k in range(MOE_EPB):
            e = eb * MOE_EPB + k
            gcol = jnp.sum(jnp.where(lane == ROUTE_LANE0 + e, gates, 0.0), axis=-1, keepdims=True)
            a = _dot(xb, wg_ref[e])
            u = _dot(xb, wu_ref[e])
            h = (a / (1.0 + jnp.exp(-a))) * u * gcol
            o_ref[...] += _dot(h.astype(BF16), wd_ref[e])
        return carry

    lax.fori_loop(0, N_EXPERTS // MOE_EPB, experts, 0)
    o_ref[...] = _layer_norm(DN_ALPHA * x_ref[...] + o_ref[...], g_ref[...], b_ref[...])


def _moe(x2d, wrh, wrl, br, wg, wu, wd, g, bta, tm):
    t = x2d.shape[0]
    row = pl.BlockSpec((tm, D_MODEL), lambda i: (i, 0))
    consts = (wrh, wrl, br, wg, wu, wd, g, bta)
    return pl.pallas_call(
        _moe_kernel, out_shape=jax.ShapeDtypeStruct((t, D_MODEL), F32),
        grid=(t // tm,), in_specs=[row] + [_const_spec(c.shape) for c in consts], out_specs=row,
        scratch_shapes=[pltpu.VMEM((tm, D_MODEL), BF16), pltpu.VMEM((tm, LANES), F32)],
        compiler_params=_params(("parallel",)), name="moe")(x2d, *consts)


def _pad_cols(w, n):
    return jnp.pad(w, ((0, 0), (0, n - w.shape[1])))


def kernel(x, mem, positions, w_in, b_forget, diff_lambda, diff_subln, w_branch_a, w_branch_b, w_branch_c,
           w_gate, b_gate, w_out, w_xq, w_xk, w_xv, w_xo, w_route_group, b_route_group, w_route_expert,
           b_route_expert, w_expert_gate, w_expert_up, w_expert_down, ln_g, ln_b):
    b, s, d = x.shape
    t = b * s
    depth = w_in.shape[0]
    scale = HEAD_DIM ** -0.5

    half = HEAD_DIM // 2
    inv = ROPE_THETA ** (-jnp.arange(half, dtype=F32) / half)
    inv_row = jnp.tile(inv, LANES // half).reshape(1, LANES)
    pos = positions.astype(F32).reshape(t, 1)

    mem2d = mem.reshape(b * MEM_LEN, d)
    x2d = x.reshape(t, d)
    row = lambda v: v.reshape(1, -1).astype(F32)
    for l in range(depth):
        lam_init = 0.8 - 0.6 * math.exp(-0.3 * l)
        offs = [0, A_W, 2 * A_W, 3 * A_W, 3 * A_W + B_W, 3 * A_W + 2 * B_W, 3 * A_W + 3 * B_W]
        offs.append(offs[-1] + B_HEADS)
        offs += [offs[-1] + C_W, offs[-1] + 2 * C_W, offs[-1] + 3 * C_W]
        wl = w_in[l]
        seg = [wl[:, offs[k]:offs[k + 1]] for k in range(10)]
        qa, ka, va, qb, kb, vb, fb, qc, kc, vc = seg
        ws = [(qa * (scale * LOG2E)), ka, va.T, (qb * (scale * LOG2E)), kb, vb.T, _pad_cols(fb, LANES),
              (qc * (scale * LOG2E)), kc, vc]
        ws = [w.astype(BF16) for w in ws]
        bfp = _pad_cols(row(b_forget[l]), LANES)

        outs = _inproj(x2d, pos, inv_row, bfp, ws, tm=512)
        o_qa, o_ka, o_va, o_qb, o_kb, o_vb, o_lf = outs[:7]
        o_qc, o_kc, o_vc = outs[7:10], outs[10:13], outs[13:16]

        lf = diff_lambda[l].astype(F32)
        lam = jnp.exp(jnp.dot(lf[0], lf[1])) - jnp.exp(jnp.dot(lf[2], lf[3])) + lam_init
        lam_row = jnp.full((1, LANES), lam, F32)
        gain = row(diff_subln[l]) * (1.0 - lam_init)
        r3 = lambda v: v.reshape(b, s, v.shape[-1])
        oa = _flash(r3(o_qa), r3(o_ka), o_va, (lam_row, gain), forget=False, bq=2 * FLASH_BK)
        kb_aug = _fox_keys(r3(o_kb), r3(o_lf), tk=512)
        ob = _flash(r3(o_qb), kb_aug, o_vb, (), forget=True, bq=2 * FLASH_BK)
        ocs, lses = [], []
        for gi, (_, dil) in enumerate(C_PATTERNS):
            nq = min(4, s // (dil * C_BLOCK))
            o_g, lse_g = _dilated(o_qc[gi], o_kc[gi], o_vc[gi], b, dil, nq, nr=min(dil, 4 // nq))
            ocs.append(o_g)
            lses.append(lse_g)
        x2d = _merge(x2d, oa.reshape(t, A_W), ob.reshape(t, B_W), ocs, lses,
                     w_gate[l].astype(BF16), row(b_gate[l]), w_branch_a[l].astype(BF16),
                     w_branch_b[l].astype(BF16), w_branch_c[l].astype(BF16), w_out[l].astype(BF16),
                     row(ln_g[l, 0]), row(ln_b[l, 0]), tm=512)
        xscale = X_HEAD_DIM ** -0.5
        mk, mv = _memkv(mem2d, w_xk[l].astype(BF16), w_xv[l].astype(BF16), tm=MEM_LEN)
        x2d = _xattn(x2d.reshape(b, s, d), mk.reshape(b, MEM_LEN, d), mv.reshape(b, MEM_LEN, d),
                     (w_xq[l] * xscale).astype(BF16), w_xo[l].astype(BF16),
                     row(ln_g[l, 1]), row(ln_b[l, 1]), tm=512).reshape(t, d)
        wr = jnp.concatenate([w_route_group[l]] + [w_route_expert[l, gi] for gi in range(N_GROUPS)], axis=1)
        wr = _pad_cols(wr.astype(F32), LANES)
        wrh = wr.astype(BF16)
        wrl = (wr - wrh.astype(F32)).astype(BF16)
        br = _pad_cols(row(jnp.concatenate([b_route_group[l], b_route_expert[l].reshape(-1)])), LANES)
        ne = (N_EXPERTS,)
        x2d = _moe(x2d, wrh, wrl, br,
                   w_expert_gate[l].reshape(ne + w_expert_gate.shape[3:]).astype(BF16),
                   w_expert_up[l].reshape(ne + w_expert_up.shape[3:]).astype(BF16),
                   w_expert_down[l].reshape(ne + w_expert_down.shape[3:]).astype(BF16),
                   row(ln_g[l, 2]), row(ln_b[l, 2]), tm=1024)
    return x2d.reshape(b, s, d)
```

```python
import functools
import math

import jax
import jax.numpy as jnp
import numpy as np
from jax import lax
from jax.experimental import pallas as pl
from jax.experimental.pallas import tpu as pltpu

F32 = jnp.float32
BF16 = jnp.bfloat16

D_MODEL = 1024
HEAD_DIM = 64
ROPE_THETA = 10000.0
EPS = 1e-5
MEM_LEN = 256
A_HEADS = 4
B_HEADS = 8
C_PATTERNS = ((128, 1), (512, 4), (2048, 16))
C_GROUPS = 3
C_SLOTS = 4
C_BLOCK = 128
X_HEADS = 4
X_HEAD_DIM = D_MODEL // X_HEADS
N_GROUPS = 4
EXPERTS_PER_GROUP = 4
N_EXPERTS = N_GROUPS * EXPERTS_PER_GROUP
D_EXPERT = D_MODEL // 4
DEPTH = 2
DN_ALPHA = (2 * DEPTH) ** 0.25

LANES = 128
NEG = -1e30
LOG2E = math.log2(math.e)
VMEM_LIMIT = 56 * 1024 * 1024

A_W = A_HEADS * 2 * HEAD_DIM
B_W = B_HEADS * HEAD_DIM
C_GW = C_SLOTS * HEAD_DIM
C_W = C_GROUPS * C_GW
ONES_ROWS = 16
VA_ROWS = A_HEADS * (2 * HEAD_DIM + ONES_ROWS)
VB_ROWS = B_HEADS * (HEAD_DIM + ONES_ROWS)
SUB_ROWS = 256
FLASH_BK = 512


def _params(sem, flags=None):
    return pltpu.CompilerParams(dimension_semantics=sem, vmem_limit_bytes=VMEM_LIMIT, flags=flags)


def _const_spec(shape):
    nd = len(shape)
    return pl.BlockSpec(shape, lambda *_: (0,) * nd, pipeline_mode=pl.Buffered(1))


def _layer_norm(y, g, b):
    mu = jnp.mean(y, axis=-1, keepdims=True)
    yc = y - mu
    var = jnp.mean(yc * yc, axis=-1, keepdims=True)
    return yc * lax.rsqrt(var + EPS) * g + b


def _dot(a, b):
    return jnp.dot(a, b, preferred_element_type=F32)


def _dot_nt(a, b):
    return lax.dot_general(a, b, (((1,), (1,)), ((), ())), preferred_element_type=F32)


def _inproj_kernel(x_ref, pos_ref, inv_ref, bf_ref,
                   w_qa, w_ka, w_va, w_qb, w_kb, w_vb, w_fb, w_qc, w_kc, w_vc,
                   o_qa, o_ka, o_va, o_qb, o_kb, o_vb, o_lf,
                   o_qc0, o_qc1, o_qc2, o_kc0, o_kc1, o_kc2, o_vc0, o_vc1, o_vc2, perm_ref):
    xb = x_ref[...].astype(BF16)
    ang = pos_ref[...] * inv_ref[...]
    lane = lax.broadcasted_iota(jnp.int32, ang.shape, 1)
    lower = (lane & (HEAD_DIM - 1)) < HEAD_DIM // 2
    cos = jnp.cos(ang)
    sin = jnp.sin(ang)
    sin = jnp.where(lower, -sin, sin)

    def rope(t):
        sw = jnp.where(lower, pltpu.roll(t, LANES - HEAD_DIM // 2, 1), pltpu.roll(t, HEAD_DIM // 2, 1))
        return t * cos + sw * sin

    def roped(w_ref, outs, width):
        y = _dot(xb, w_ref[...])
        per_out = width // LANES
        for c in range(y.shape[1] // LANES):
            o = outs[c // per_out]
            lo = (c % per_out) * LANES
            o[:, lo:lo + LANES] = rope(y[:, c * LANES:(c + 1) * LANES]).astype(BF16)

    def plain(w_ref, outs, width):
        y = _dot(xb, w_ref[...])
        for c, o in enumerate(outs):
            o[...] = y[:, c * width:(c + 1) * width].astype(BF16)

    def transposed(wt_ref, o, width):
        yt = _dot_nt(wt_ref[...], xb)
        ones = jnp.ones((ONES_ROWS, FLASH_BK), F32)
        for c in range(o.shape[0]):
            cols = slice(c * FLASH_BK, (c + 1) * FLASH_BK)
            parts = []
            for r0 in range(0, yt.shape[0], width):
                parts += [yt[r0:r0 + width, cols], ones]
            o[c] = jnp.concatenate(parts, axis=0).astype(BF16)

    roped(w_qa, (o_qa,), A_W)
    roped(w_ka, (o_ka,), A_W)
    transposed(w_va, o_va, LANES)
    plain(w_qb, (o_qb,), B_W)
    plain(w_kb, (o_kb,), B_W)
    transposed(w_vb, o_vb, HEAD_DIM)
    def dilated(w_ref, outs, slot, use_rope):
        y = _dot(xb, w_ref[...])
        tm = y.shape[0]
        for gi, (_, d) in enumerate(C_PATTERNS):
            for h in range(C_GW // LANES):
                c = gi * (C_GW // LANES) + h
                t = y[:, c * LANES:(c + 1) * LANES]
                t = rope(t) if use_rope else t
                if d == 1:
                    outs[gi][:, h * LANES:(h + 1) * LANES] = t.astype(BF16)
                    continue
                sl = slot + (gi - 1) * 2 + h
                perm_ref[sl] = t
                for r in range(d):
                    lo = r * C_GW + h * LANES
                    outs[gi][:, lo:lo + LANES] = perm_ref.at[sl][pl.ds(r, tm // d, stride=d), :].astype(BF16)

    dilated(w_qc, (o_qc0, o_qc1, o_qc2), 0, True)
    dilated(w_kc, (o_kc0, o_kc1, o_kc2), 4, True)
    dilated(w_vc, (o_vc0, o_vc1, o_vc2), 8, False)
    z =_dot(xb, w_fb[...]) + bf_ref[...]
    o_lf[...] = jnp.minimum(z, 0.0) - jnp.log(1.0 + jnp.exp(-jnp.abs(z)))


def _inproj(x2d, pos, inv, bfp, ws, tm):
    t = x2d.shape[0]
    row = lambda n: pl.BlockSpec((tm, n), lambda i: (i, 0))
    tshape = lambda rows: jax.ShapeDtypeStruct((t // FLASH_BK, rows, FLASH_BK), BF16)
    tspec = lambda rows: pl.BlockSpec((tm // FLASH_BK, rows, FLASH_BK), lambda i: (i, 0, 0))
    rshape = lambda w: jax.ShapeDtypeStruct((t, w), BF16)
    out_shape = [rshape(A_W), rshape(A_W), tshape(VA_ROWS), rshape(B_W), rshape(B_W), tshape(VB_ROWS)]
    out_shape.append(jax.ShapeDtypeStruct((t, LANES), F32))
    out_specs = [row(A_W), row(A_W), tspec(VA_ROWS), row(B_W), row(B_W), tspec(VB_ROWS), row(LANES)]
    for _ in range(3):
        for _, d in C_PATTERNS:
            out_shape.append(jax.ShapeDtypeStruct((t // d, d * C_GW), BF16))
            out_specs.append(pl.BlockSpec((tm // d, d * C_GW), lambda i: (i, 0)))
    in_specs = [row(D_MODEL), row(1), _const_spec((1, LANES)), _const_spec((1, LANES))]
    in_specs += [_const_spec(w.shape) for w in ws]
    n_perm = 3 * (C_GROUPS - 1) * (C_GW // LANES)
    return pl.pallas_call(
        _inproj_kernel, out_shape=out_shape, grid=(t // tm,), in_specs=in_specs, out_specs=out_specs,
        scratch_shapes=[pltpu.VMEM((n_perm, tm, LANES), F32)],
        compiler_params=_params(("parallel",)), name="inproj")(x2d, pos, inv, bfp, *ws)


BIAS_PARTS = 3


def _split3(v):
    p0 = v.astype(BF16)
    r1 = v - p0.astype(F32)
    p1 = r1.astype(BF16)
    p2 = (r1 - p1.astype(F32)).astype(BF16)
    return p0, p1, p2


def _fox_keys_kernel(k_ref, lf_ref, tri_ref, sel_ref, o_ref, carry_ref):
    @pl.when(pl.program_id(1) == 0)
    def _reset():
        carry_ref[...] = jnp.zeros_like(carry_ref)

    tk = lf_ref.shape[0]
    tri = tri_ref[...]
    c = carry_ref[...] + sum(_dot(tri, part) for part in _split3(lf_ref[...]))
    carry_ref[...] = c[tk - 1:tk, :]
    extra = _dot(jnp.concatenate(_split3(-LOG2E * c), axis=1), sel_ref[...])
    low = lax.broadcasted_iota(jnp.int32, (tk, LANES), 1) < HEAD_DIM
    for h in range(B_HEADS):
        pair = k_ref[:, (h // 2) * LANES:(h // 2 + 1) * LANES].astype(F32)
        own = low if h % 2 == 0 else jnp.logical_not(low)
        sl = slice(h * LANES, (h + 1) * LANES)
        o_ref[:, sl] = jnp.where(own, pair, extra[:, sl]).astype(BF16)


def _fox_keys(k3d, lf3d, tk):
    b, s, w = k3d.shape
    tri = jnp.asarray(np.tril(np.ones((tk, tk), np.float32)), BF16)
    sel = np.zeros((BIAS_PARTS * LANES, B_HEADS * LANES), np.float32)
    for part in range(BIAS_PARTS):
        for head in range(B_HEADS):
            sel[part * LANES + head, head * LANES + (HEAD_DIM if head % 2 == 0 else 0) + part] = 1.0
    sel = jnp.asarray(sel, BF16)
    blk = lambda n: pl.BlockSpec((None, tk, n), lambda bi, j: (bi, j, 0))
    return pl.pallas_call(
        _fox_keys_kernel, out_shape=jax.ShapeDtypeStruct((b, s, B_HEADS * LANES), BF16),
        grid=(b, s // tk), in_specs=[blk(w), blk(LANES), _const_spec(tri.shape), _const_spec(sel.shape)],
        out_specs=blk(B_HEADS * LANES), scratch_shapes=[pltpu.VMEM((1, LANES), F32)],
        compiler_params=_params(("parallel", "arbitrary")), name="fox_keys")(k3d, lf3d, tri, sel)


def _flash_kernel(forget, bq, *refs):
    bk = FLASH_BK
    qw = bk // 2
    if forget:
        q_ref, k_ref, vt_ref, o_ref, m_ref, acc_ref, st_ref = refs
    else:
        q_ref, k_ref, vt_ref, lam_ref, gain_ref, o_ref, m_ref, acc_ref, st_ref = refs
    dvp = acc_ref.shape[3]
    dv = dvp - ONES_ROWS
    nq = q_ref.shape[0] // bq
    parts = bq // qw
    lane = lax.broadcasted_iota(jnp.int32, (bq, LANES), 1)
    low = lane < HEAD_DIM
    if forget:
        fill = (jnp.where(lane < HEAD_DIM + BIAS_PARTS, 1.0, 0.0), jnp.where(lane < BIAS_PARTS, 1.0, 0.0))
    else:
        fill = (0.0, 0.0)
    triangle = (lax.broadcasted_iota(jnp.int32, (qw, qw), 0) <= lax.broadcasted_iota(jnp.int32, (qw, qw), 1))

    m_ref[...] = jnp.full(m_ref.shape, NEG, F32)
    acc_ref[...] = jnp.zeros(acc_ref.shape, F32)
    qts = {}

    def queries(i):
        if i not in qts:
            q = q_ref[i * bq:(i + 1) * bq, :].astype(F32)
            qts[i] = (jnp.where(low, q, fill[0]).T.astype(BF16), jnp.where(low, fill[1], q).T.astype(BF16))
        return qts[i]

    def live_rows(i, c, t):
        qb = i * parts + t
        blocks = max(0, min(bk // qw, qb - c * (bk // qw) + 1))
        return blocks * qw, blocks > 0 and c * (bk // qw) + blocks - 1 == qb

    def scores(i, c, buf):
        for s in range(2):
            for t in range(parts):
                rows, _ = live_rows(i, c, t)
                if rows:
                    kc = k_ref[c * bk:c * bk + rows, s * LANES:(s + 1) * LANES] if forget \
                        else k_ref[c * bk:c * bk + rows, :]
                    st_ref[buf, s, t, :rows, :] = _dot(kc, queries(i)[s][:, t * qw:(t + 1) * qw])

    def absorb(i, c, buf):
        for s in range(2):
            for t in range(parts):
                rows, diagonal = live_rows(i, c, t)
                if not rows:
                    continue
                st = st_ref[buf, s, t, :rows, :]
                if diagonal:
                    tri = jnp.where(triangle, st[rows - qw:], NEG)
                    st = tri if rows == qw else jnp.concatenate([st[:rows - qw], tri], axis=0)
                m_prev = m_ref[i, s, t]
                m_new = jnp.maximum(m_prev, jnp.max(st, axis=0, keepdims=True))
                a = jnp.exp2(m_prev - m_new)
                p = jnp.exp2(st - m_new).astype(BF16)
                m_ref[i, s, t] = m_new
                vt = vt_ref[c, s * dvp:(s + 1) * dvp, :rows] if forget else vt_ref[c, :, :rows]
                acc_ref[i, s, t] = a * acc_ref[i, s, t] + _dot(vt, p)

    def finish(i):
        for t in range(parts):
            o0 = acc_ref[i, 0, t, :dv, :] / acc_ref[i, 0, t, dv:dv + 1, :]
            o1 = acc_ref[i, 1, t, :dv, :] / acc_ref[i, 1, t, dv:dv + 1, :]
            rows = slice(i * bq + t * qw, i * bq + (t + 1) * qw)
            if forget:
                o_ref[rows, :] = jnp.concatenate([o0, o1], axis=0).T.astype(o_ref.dtype)
            else:
                o = (o0 - lam_ref[:, 0:1] * o1).T
                ms = jnp.mean(o * o, axis=-1, keepdims=True)
                o_ref[rows, :] = (o * lax.rsqrt(ms + EPS) * gain_ref[...]).astype(o_ref.dtype)

    units = [(i, c) for i in range(nq) for c in range((i + 1) * bq // bk)]
    nbuf = st_ref.shape[0]
    for n in range(min(nbuf - 1, len(units))):
        scores(*units[n], n % nbuf)
    for n, (i, c) in enumerate(units):
        if n + nbuf - 1 < len(units):
            scores(*units[n + nbuf - 1], (n + nbuf - 1) % nbuf)
        absorb(i, c, n % nbuf)
        if c == (i + 1) * bq // bk - 1:
            finish(i)


def _flash(q, k, vt, extra, forget, bq):
    b, s, w = q.shape
    g = w // LANES
    nk = s // FLASH_BK
    dvp = (HEAD_DIM if forget else LANES) + ONES_ROWS
    rows = vt.shape[1] // g
    kw = k.shape[2] // g
    qspec = pl.BlockSpec((None, s, LANES), lambda bi, gi: (bi, 0, gi))
    kspec = pl.BlockSpec((None, s, kw), lambda bi, gi: (bi, 0, gi))
    vspec = pl.BlockSpec((nk, rows, FLASH_BK), lambda bi, gi: (bi, gi, 0))
    assert bq == 2 * FLASH_BK
    bk = FLASH_BK
    nq = s // bq
    qw, parts = bk // 2, 2 * bq // bk
    scratch = [pltpu.VMEM((nq, 2, parts, 1, qw), F32), pltpu.VMEM((nq, 2, parts, dvp, qw), F32),
               pltpu.VMEM((3, 2, parts, bk, qw), F32)]
    especs = [] if forget else [_const_spec((1, LANES)), _const_spec((1, LANES))]
    return pl.pallas_call(
        functools.partial(_flash_kernel, forget, bq),
        out_shape=jax.ShapeDtypeStruct((b, s, w), BF16),
        grid=(b, g), in_specs=[qspec, kspec, vspec] + especs, out_specs=qspec,
        scratch_shapes=scratch,
        compiler_params=_params(("parallel", "parallel")),
        name="fox_attn" if forget else "diff_attn")(q, k, vt, *extra)


def _dilated_kernel(nq, q_ref, kp_ref, kc_ref, vp_ref, vc_ref, o_ref, lse_ref):
    n = C_BLOCK
    pairs = q_ref.shape[1] // LANES
    j = pl.program_id(2)
    kall = jnp.concatenate([kp_ref[...], kc_ref[...]], axis=0)
    vall = jnp.concatenate([vp_ref[...], vc_ref[...]], axis=0)
    qi = lax.broadcasted_iota(jnp.int32, (n, 2 * n), 0)
    kj = lax.broadcasted_iota(jnp.int32, (n, 2 * n), 1)
    dist = n + qi - kj
    window = (dist >= 0) & (dist <= n)
    first_key = jnp.where(j > 0, 0, n)
    lane = lax.broadcasted_iota(jnp.int32, (n, LANES), 1)
    low = lane < HEAD_DIM
    units = [(t, p, s) for t in range(nq) for p in range(pairs) for s in range(2)]
    scores = {}
    for t, p, s in units:
        sl = slice(p * LANES, (p + 1) * LANES)
        q128 = q_ref[t * n:(t + 1) * n, sl]
        zero = jnp.zeros_like(q128)
        qm = jnp.where(low, q128, zero) if s == 0 else jnp.where(low, zero, q128)
        mask = (window & (kj >= first_key)) if t == 0 else window
        scores[t, p, s] = jnp.where(mask, _dot_nt(qm, kall[t * n:(t + 2) * n, sl]), NEG)
    probs = {}
    for u in units:
        m = jnp.max(scores[u], axis=-1, keepdims=True)
        probs[u] = (jnp.exp2(scores[u] - m).astype(BF16), m)
    low_keys = lax.broadcasted_iota(jnp.int32, (2 * n, LANES), 1) < HEAD_DIM
    for t in range(nq):
        for p in range(pairs):
            sl = slice(p * LANES, (p + 1) * LANES)
            v128 = vall[t * n:(t + 2) * n, sl]
            one = jnp.ones_like(v128)
            os, ls = [], []
            for s in range(2):
                vs = jnp.where(low_keys, v128, one) if s == 0 else jnp.where(low_keys, one, v128)
                num = _dot(probs[t, p, s][0], vs)
                den = pltpu.roll(num, HEAD_DIM, 1)
                os.append(num / den)
                ls.append(probs[t, p, s][1] + jnp.log2(den))
            o_ref[t * n:(t + 1) * n, sl] = jnp.where(low, os[0], os[1])
            lse_ref[t * n:(t + 1) * n, sl] = jnp.where(low, ls[0], ls[1])


def _dilated(q, k, v, b, dilation, nq, nr):
    w = C_GW * nr
    n = C_BLOCK
    m_len = q.shape[0] // b
    nb = m_len // (n * nq)
    view = lambda t: t.reshape(b, m_len, dilation * C_GW)
    cur = pl.BlockSpec((None, n * nq, w), lambda bi, r, j: (bi, j, r))
    prev = pl.BlockSpec((None, n, w), lambda bi, r, j: (bi, jnp.maximum(j * nq - 1, 0), r))
    o, lse = pl.pallas_call(
        functools.partial(_dilated_kernel, nq),
        out_shape=[jax.ShapeDtypeStruct((b, m_len, dilation * C_GW), F32)] * 2,
        grid=(b, dilation // nr, nb), in_specs=[cur, prev, cur, prev, cur], out_specs=[cur, cur],
        compiler_params=_params(("parallel", "parallel", "arbitrary")),
        name=f"dilated_d{dilation}")(view(q), view(k), view(k), view(v), view(v))
    return o.reshape(b * m_len, dilation * C_GW), lse.reshape(b * m_len, dilation * C_GW)


def _merge_kernel(x_ref, oa_ref, ob_ref, oc0, oc1, oc2, ls0, ls1, ls2,
                  wg_ref, bg_ref, wa_ref, wb_ref, wc_ref, wo_ref, g_ref, b_ref, o_ref, perm_ref):
    x = x_ref[...]
    xb = x.astype(BF16)
    tm = x.shape[0]

    def token_major(blk_ref, gi, slot):
        d = C_PATTERNS[gi][1]
        if d == 1:
            return blk_ref[...]
        for h in range(C_GW // LANES):
            for r in range(d):
                lo = r * C_GW + h * LANES
                perm_ref.at[slot + h][pl.ds(r, tm // d, stride=d), :] = blk_ref[:, lo:lo + LANES]
        return jnp.concatenate([perm_ref[slot + h] for h in range(C_GW // LANES)], axis=1)

    os = [token_major(r, gi, 2 * gi) for gi, r in enumerate((oc0, oc1, oc2))]
    ls = [token_major(r, gi, 2 * C_GROUPS + 2 * gi) for gi, r in enumerate((ls0, ls1, ls2))]
    for r0 in range(0, tm, SUB_ROWS):
        rows = slice(r0, r0 + SUB_ROWS)
        l0, l1, l2 = (l[rows] for l in ls)
        m = jnp.maximum(jnp.maximum(l0, l1), l2)
        e0, e1, e2 = jnp.exp2(l0 - m), jnp.exp2(l1 - m), jnp.exp2(l2 - m)
        oc = (e0 * os[0][rows] + e1 * os[1][rows] + e2 * os[2][rows]) / (e0 + e1 + e2)
        branches = ((oa_ref[rows, :], wa_ref), (ob_ref[rows, :], wb_ref), (oc.astype(BF16), wc_ref))
        merged = None
        for k, (ob, w_ref) in enumerate(branches):
            sl = slice(k * D_MODEL, (k + 1) * D_MODEL)
            z = _dot(xb[rows], wg_ref[:, sl]) + bg_ref[:, sl]
            gate = 1.0 / (1.0 + jnp.exp(-z))
            term = gate * _dot(ob, w_ref[...])
            merged = term if merged is None else merged + term
        h = _dot(merged.astype(BF16), wo_ref[...])
        o_ref[rows, :] = _layer_norm(DN_ALPHA * x[rows] + h, g_ref[...], b_ref[...])


def _merge(x2d, oa, ob, ocs, lses, wg, bg, wa, wb, wc, wo, g, bta, tm):
    t = x2d.shape[0]
    row = lambda n: pl.BlockSpec((tm, n), lambda i: (i, 0))
    consts = (wg, bg, wa, wb, wc, wo, g, bta)
    cspecs = [pl.BlockSpec((tm // d, d * C_GW), lambda i: (i, 0)) for _, d in C_PATTERNS]
    in_specs = [row(D_MODEL), row(A_W), row(B_W)] + cspecs * 2 + [_const_spec(c.shape) for c in consts]
    return pl.pallas_call(
        _merge_kernel, out_shape=jax.ShapeDtypeStruct((t, D_MODEL), F32), grid=(t // tm,),
        in_specs=in_specs, out_specs=row(D_MODEL),
        scratch_shapes=[pltpu.VMEM((4 * C_GROUPS, tm, LANES), F32)],
        compiler_params=_params(("parallel",)), name="merge")(x2d, oa, ob, *ocs, *lses, *consts)


def _memkv_kernel(m_ref, wk_ref, wv_ref, k_ref, v_ref):
    mb = m_ref[...].astype(BF16)
    k_ref[...] = _dot(mb, wk_ref[...]).astype(BF16)
    v_ref[...] = _dot(mb, wv_ref[...]).astype(BF16)


def _memkv(mem2d, wk, wv, tm):
    t = mem2d.shape[0]
    row = pl.BlockSpec((tm, D_MODEL), lambda i: (i, 0))
    return pl.pallas_call(
        _memkv_kernel, out_shape=[jax.ShapeDtypeStruct((t, D_MODEL), BF16)] * 2, grid=(t // tm,),
        in_specs=[row, _const_spec(wk.shape), _const_spec(wv.shape)], out_specs=[row, row],
        compiler_params=_params(("parallel",)), name="memkv")(mem2d, wk, wv)


def _xattn_kernel(x_ref, k_ref, v_ref, wq_ref, wo_ref, g_ref, b_ref, o_ref):
    for r0 in range(0, x_ref.shape[0], SUB_ROWS):
        rows = slice(r0, r0 + SUB_ROWS)
        x = x_ref[rows, :]
        q = _dot(x.astype(BF16), wq_ref[...]).astype(BF16)
        heads = []
        for h in range(X_HEADS):
            sl = slice(h * X_HEAD_DIM, (h + 1) * X_HEAD_DIM)
            st = _dot_nt(q[:, sl], k_ref[:, sl])
            m = jnp.max(st, axis=-1, keepdims=True)
            e = jnp.exp(st - m)
            l = jnp.sum(e, axis=-1, keepdims=True)
            heads.append((_dot(e.astype(BF16), v_ref[:, sl]) / l).astype(BF16))
        o = jnp.concatenate(heads, axis=-1)
        h_out = _dot(o, wo_ref[...])
        o_ref[rows, :] = _layer_norm(DN_ALPHA * x + h_out, g_ref[...], b_ref[...])


def _xattn(x3d, k3d, v3d, wq, wo, g, bta, tm):
    b, s, d = x3d.shape
    xspec = pl.BlockSpec((None, tm, d), lambda bi, i: (bi, i, 0))
    mspec = pl.BlockSpec((None, MEM_LEN, d), lambda bi, i: (bi, 0, 0))
    consts = (wq, wo, g, bta)
    return pl.pallas_call(
        _xattn_kernel, out_shape=jax.ShapeDtypeStruct((b, s, d), F32), grid=(b, s // tm),
        in_specs=[xspec, mspec, mspec] + [_const_spec(c.shape) for c in consts], out_specs=xspec,
        compiler_params=_params(("parallel", "parallel")), name="xattn")(x3d, k3d, v3d, *consts)


MOE_EPB = 2
ROUTE_LANE0 = N_GROUPS


def _moe_kernel(x_ref, wrh_ref, wrl_ref, br_ref, wg_ref, wu_ref, wd_ref, g_ref, b_ref, o_ref,
                xb_ref, gate_ref):
    tm = x_ref.shape[0]
    lane = lax.broadcasted_iota(jnp.int32, (tm, LANES), 1)
    x = x_ref[...]
    lanef = lane.astype(F32)
    xh = x.astype(BF16)
    xl = (x - xh.astype(F32)).astype(BF16)
    xb_ref[...] = xh
    hh_hl = _dot(xh, jnp.concatenate([wrh_ref[...], wrl_ref[...]], axis=1))
    logits = (hh_hl[:, :LANES] + _dot(xl, wrh_ref[...]) + hh_hl[:, LANES:]) + br_ref[...]
    gl = jnp.where(lane < N_GROUPS, logits, NEG)
    gmax = jnp.max(gl, axis=-1, keepdims=True)
    g_gate = 1.0 / jnp.sum(jnp.exp(gl - gmax), axis=-1, keepdims=True)
    g_idx = jnp.min(jnp.where(gl == gmax, lanef, float(LANES)), axis=-1, keepdims=True)
    e_group = ((lane - ROUTE_LANE0) >> 2).astype(F32)
    sel = (lane >= ROUTE_LANE0) & (e_group == g_idx)
    el = jnp.where(sel, logits, NEG)
    top1 = jnp.max(el, axis=-1, keepdims=True)
    i1 = jnp.min(jnp.where(el == top1, lanef, float(LANES)), axis=-1, keepdims=True)
    el2 = jnp.where(lanef == i1, NEG, el)
    top2 = jnp.max(el2, axis=-1, keepdims=True)
    i2 = jnp.min(jnp.where(el2 == top2, lanef, float(LANES)), axis=-1, keepdims=True)
    r = jnp.exp(top2 - top1)
    w1 = 1.0 / (1.0 + r)
    w2 = r * w1
    gate_ref[...] = jnp.where(lanef == i1, w1, jnp.where(lanef == i2, w2, 0.0)) * g_gate
    o_ref[...] = jnp.zeros_like(o_ref)

    for e in range(N_EXPERTS):
        xb = xb_ref[...]
        gcol = jnp.sum(jnp.where(lane == ROUTE_LANE0 + e, gate_ref[...], 0.0), axis=-1, keepdims=True)
        a = _dot(xb, wg_ref[e])
        u = _dot(xb, wu_ref[e])
        h = (a / (1.0 + jnp.exp(-a))) * u * gcol
        o_ref[...] += _dot(h.astype(BF16), wd_ref[e])
    o_ref[...] = _layer_norm(DN_ALPHA * x_ref[...] + o_ref[...], g_ref[...], b_ref[...])


def _moe(x2d, wrh, wrl, br, wg, wu, wd, g, bta, tm):
    t = x2d.shape[0]
    row = pl.BlockSpec((tm, D_MODEL), lambda i: (i, 0))
    consts = (wrh, wrl, br, wg, wu, wd, g, bta)
    return pl.pallas_call(
        _moe_kernel, out_shape=jax.ShapeDtypeStruct((t, D_MODEL), F32),
        grid=(t // tm,), in_specs=[row] + [_const_spec(c.shape) for c in consts], out_specs=row,
        scratch_shapes=[pltpu.VMEM((tm, D_MODEL), BF16), pltpu.VMEM((tm, LANES), F32)],
        compiler_params=_params(("parallel",)), name="moe")(x2d, *consts)


def _pad_cols(w, n):
    return jnp.pad(w, ((0, 0), (0, n - w.shape[1])))


def kernel(x, mem, positions, w_in, b_forget, diff_lambda, diff_subln, w_branch_a, w_branch_b, w_branch_c,
           w_gate, b_gate, w_out, w_xq, w_xk, w_xv, w_xo, w_route_group, b_route_group, w_route_expert,
           b_route_expert, w_expert_gate, w_expert_up, w_expert_down, ln_g, ln_b):
    b, s, d = x.shape
    t = b * s
    depth = w_in.shape[0]
    scale = HEAD_DIM ** -0.5

    half = HEAD_DIM // 2
    inv = ROPE_THETA ** (-jnp.arange(half, dtype=F32) / half)
    inv_row = jnp.tile(inv, LANES // half).reshape(1, LANES)
    pos = positions.astype(F32).reshape(t, 1)

    mem2d = mem.reshape(b * MEM_LEN, d)
    x2d = x.reshape(t, d)
    row = lambda v: v.reshape(1, -1).astype(F32)
    for l in range(depth):
        lam_init = 0.8 - 0.6 * math.exp(-0.3 * l)
        offs = [0, A_W, 2 * A_W, 3 * A_W, 3 * A_W + B_W, 3 * A_W + 2 * B_W, 3 * A_W + 3 * B_W]
        offs.append(offs[-1] + B_HEADS)
        offs += [offs[-1] + C_W, offs[-1] + 2 * C_W, offs[-1] + 3 * C_W]
        wl = w_in[l]
        seg = [wl[:, offs[k]:offs[k + 1]] for k in range(10)]
        qa, ka, va, qb, kb, vb, fb, qc, kc, vc = seg
        ws = [(qa * (scale * LOG2E)), ka, va.T, (qb * (scale * LOG2E)), kb, vb.T, _pad_cols(fb, LANES),
              (qc * (scale * LOG2E)), kc, vc]
        ws = [w.astype(BF16) for w in ws]
        bfp = _pad_cols(row(b_forget[l]), LANES)

        outs = _inproj(x2d, pos, inv_row, bfp, ws, tm=512)
        o_qa, o_ka, o_va, o_qb, o_kb, o_vb, o_lf = outs[:7]
        o_qc, o_kc, o_vc = outs[7:10], outs[10:13], outs[13:16]

        lf = diff_lambda[l].astype(F32)
        lam = jnp.exp(jnp.dot(lf[0], lf[1])) - jnp.exp(jnp.dot(lf[2], lf[3])) + lam_init
        lam_row = jnp.full((1, LANES), lam, F32)
        gain = row(diff_subln[l]) * (1.0 - lam_init)
        r3 = lambda v: v.reshape(b, s, v.shape[-1])
        oa = _flash(r3(o_qa), r3(o_ka), o_va, (lam_row, gain), forget=False, bq=2 * FLASH_BK)
        kb_aug = _fox_keys(r3(o_kb), r3(o_lf), tk=512)
        ob = _flash(r3(o_qb), kb_aug, o_vb, (), forget=True, bq=2 * FLASH_BK)
        ocs, lses = [], []
        for gi, (_, dil) in enumerate(C_PATTERNS):
            nq = min(4, s // (dil * C_BLOCK))
            o_g, lse_g = _dilated(o_qc[gi], o_kc[gi], o_vc[gi], b, dil, nq, nr=min(dil, 4 // nq))
            ocs.append(o_g)
            lses.append(lse_g)
        x2d = _merge(x2d, oa.reshape(t, A_W), ob.reshape(t, B_W), ocs, lses,
                     w_gate[l].astype(BF16), row(b_gate[l]), w_branch_a[l].astype(BF16),
                     w_branch_b[l].astype(BF16), w_branch_c[l].astype(BF16), w_out[l].astype(BF16),
                     row(ln_g[l, 0]), row(ln_b[l, 0]), tm=512)
        xscale = X_HEAD_DIM ** -0.5
        mk, mv = _memkv(mem2d, w_xk[l].astype(BF16), w_xv[l].astype(BF16), tm=MEM_LEN)
        x2d = _xattn(x2d.reshape(b, s, d), mk.reshape(b, MEM_LEN, d), mv.reshape(b, MEM_LEN, d),
                     (w_xq[l] * xscale).astype(BF16), w_xo[l].astype(BF16),
                     row(ln_g[l, 1]), row(ln_b[l, 1]), tm=512).reshape(t, d)
        wr = jnp.concatenate([w_route_group[l]] + [w_route_expert[l, gi] for gi in range(N_GROUPS)], axis=1)
        wr = _pad_cols(wr.astype(F32), LANES)
        wrh = wr.astype(BF16)
        wrl = (wr - wrh.astype(F32)).astype(BF16)
        br = _pad_cols(row(jnp.concatenate([b_route_group[l], b_route_expert[l].reshape(-1)])), LANES)
        ne = (N_EXPERTS,)
        x2d = _moe(x2d, wrh, wrl, br,
                   w_expert_gate[l].reshape(ne + w_expert_gate.shape[3:]).astype(BF16),
                   w_expert_up[l].reshape(ne + w_expert_up.shape[3:]).astype(BF16),
                   w_expert_down[l].reshape(ne + w_expert_down.shape[3:]).astype(BF16),
                   row(ln_g[l, 2]), row(ln_b[l, 2]), tm=1024)
    return x2d.reshape(b, s, d)
```

```python
import functools
import math

import jax
import jax.numpy as jnp
import numpy as np
from jax import lax
from jax.experimental import pallas as pl
from jax.experimental.pallas import tpu as pltpu

F32 = jnp.float32
BF16 = jnp.bfloat16

D_MODEL = 1024
HEAD_DIM = 64
ROPE_THETA = 10000.0
EPS = 1e-5
MEM_LEN = 256
A_HEADS = 4
B_HEADS = 8
C_PATTERNS = ((128, 1), (512, 4), (2048, 16))
C_GROUPS = 3
C_SLOTS = 4
C_BLOCK = 128
X_HEADS = 4
X_HEAD_DIM = D_MODEL // X_HEADS
N_GROUPS = 4
EXPERTS_PER_GROUP = 4
N_EXPERTS = N_GROUPS * EXPERTS_PER_GROUP
D_EXPERT = D_MODEL // 4
DEPTH = 2
DN_ALPHA = (2 * DEPTH) ** 0.25

LANES = 128
NEG = -1e30
LOG2E = math.log2(math.e)
VMEM_LIMIT = 56 * 1024 * 1024

A_W = A_HEADS * 2 * HEAD_DIM
B_W = B_HEADS * HEAD_DIM
C_GW = C_SLOTS * HEAD_DIM
C_W = C_GROUPS * C_GW
ONES_ROWS = 16
VA_ROWS = A_HEADS * (2 * HEAD_DIM + ONES_ROWS)
VB_ROWS = B_HEADS * (HEAD_DIM + ONES_ROWS)
SUB_ROWS = 256
FLASH_BK = 512


def _params(sem, flags=None):
    return pltpu.CompilerParams(dimension_semantics=sem, vmem_limit_bytes=VMEM_LIMIT, flags=flags)


def _const_spec(shape):
    nd = len(shape)
    return pl.BlockSpec(shape, lambda *_: (0,) * nd, pipeline_mode=pl.Buffered(1))


def _layer_norm(y, g, b):
    mu = jnp.mean(y, axis=-1, keepdims=True)
    yc = y - mu
    var = jnp.mean(yc * yc, axis=-1, keepdims=True)
    return yc * lax.rsqrt(var + EPS) * g + b


def _dot(a, b):
    return jnp.dot(a, b, preferred_element_type=F32)


def _dot_nt(a, b):
    return lax.dot_general(a, b, (((1,), (1,)), ((), ())), preferred_element_type=F32)


def _inproj_kernel(x_ref, pos_ref, inv_ref, bf_ref,
                   w_qa, w_ka, w_va, w_qb, w_kb, w_vb, w_fb, w_qc, w_kc, w_vc,
                   o_qa, o_ka, o_va, o_qb, o_kb, o_vb, o_lf,
                   o_qc0, o_qc1, o_qc2, o_kc0, o_kc1, o_kc2, o_vc0, o_vc1, o_vc2, perm_ref):
    xb = x_ref[...].astype(BF16)
    ang = pos_ref[...] * inv_ref[...]
    lane = lax.broadcasted_iota(jnp.int32, ang.shape, 1)
    lower = (lane & (HEAD_DIM - 1)) < HEAD_DIM // 2
    cos = jnp.cos(ang)
    sin = jnp.sin(ang)
    sin = jnp.where(lower, -sin, sin)

    def rope(t):
        sw = jnp.where(lower, pltpu.roll(t, LANES - HEAD_DIM // 2, 1), pltpu.roll(t, HEAD_DIM // 2, 1))
        return t * cos + sw * sin

    def roped(w_ref, outs, width):
        y = _dot(xb, w_ref[...])
        per_out = width // LANES
        for c in range(y.shape[1] // LANES):
            o = outs[c // per_out]
            lo = (c % per_out) * LANES
            o[:, lo:lo + LANES] = rope(y[:, c * LANES:(c + 1) * LANES]).astype(BF16)

    def plain(w_ref, outs, width):
        y = _dot(xb, w_ref[...])
        for c, o in enumerate(outs):
            o[...] = y[:, c * width:(c + 1) * width].astype(BF16)

    def transposed(wt_ref, o, width):
        yt = _dot_nt(wt_ref[...], xb)
        ones = jnp.ones((ONES_ROWS, FLASH_BK), F32)
        for c in range(o.shape[0]):
            cols = slice(c * FLASH_BK, (c + 1) * FLASH_BK)
            parts = []
            for r0 in range(0, yt.shape[0], width):
                parts += [yt[r0:r0 + width, cols], ones]
            o[c] = jnp.concatenate(parts, axis=0).astype(BF16)

    roped(w_qa, (o_qa,), A_W)
    roped(w_ka, (o_ka,), A_W)
    transposed(w_va, o_va, LANES)
    plain(w_qb, (o_qb,), B_W)
    plain(w_kb, (o_kb,), B_W)
    transposed(w_vb, o_vb, HEAD_DIM)
    def dilated(w_ref, outs, slot, use_rope):
        y = _dot(xb, w_ref[...])
        tm = y.shape[0]
        for gi, (_, d) in enumerate(C_PATTERNS):
            for h in range(C_GW // LANES):
                c = gi * (C_GW // LANES) + h
                t = y[:, c * LANES:(c + 1) * LANES]
                t = rope(t) if use_rope else t
                if d == 1:
                    outs[gi][:, h * LANES:(h + 1) * LANES] = t.astype(BF16)
                    continue
                sl = slot + (gi - 1) * 2 + h
                perm_ref[sl] = t
                for r in range(d):
                    lo = r * C_GW + h * LANES
                    outs[gi][:, lo:lo + LANES] = perm_ref.at[sl][pl.ds(r, tm // d, stride=d), :].astype(BF16)

    dilated(w_qc, (o_qc0, o_qc1, o_qc2), 0, True)
    dilated(w_kc, (o_kc0, o_kc1, o_kc2), 4, True)
    dilated(w_vc, (o_vc0, o_vc1, o_vc2), 8, False)
    z =_dot(xb, w_fb[...]) + bf_ref[...]
    o_lf[...] = jnp.minimum(z, 0.0) - jnp.log(1.0 + jnp.exp(-jnp.abs(z)))


def _inproj(x2d, pos, inv, bfp, ws, tm):
    t = x2d.shape[0]
    row = lambda n: pl.BlockSpec((tm, n), lambda i: (i, 0))
    tshape = lambda rows: jax.ShapeDtypeStruct((t // FLASH_BK, rows, FLASH_BK), BF16)
    tspec = lambda rows: pl.BlockSpec((tm // FLASH_BK, rows, FLASH_BK), lambda i: (i, 0, 0))
    rshape = lambda w: jax.ShapeDtypeStruct((t, w), BF16)
    out_shape = [rshape(A_W), rshape(A_W), tshape(VA_ROWS), rshape(B_W), rshape(B_W), tshape(VB_ROWS)]
    out_shape.append(jax.ShapeDtypeStruct((t, LANES), F32))
    out_specs = [row(A_W), row(A_W), tspec(VA_ROWS), row(B_W), row(B_W), tspec(VB_ROWS), row(LANES)]
    for _ in range(3):
        for _, d in C_PATTERNS:
            out_shape.append(jax.ShapeDtypeStruct((t // d, d * C_GW), BF16))
            out_specs.append(pl.BlockSpec((tm // d, d * C_GW), lambda i: (i, 0)))
    in_specs = [row(D_MODEL), row(1), _const_spec((1, LANES)), _const_spec((1, LANES))]
    in_specs += [_const_spec(w.shape) for w in ws]
    n_perm = 3 * (C_GROUPS - 1) * (C_GW // LANES)
    return pl.pallas_call(
        _inproj_kernel, out_shape=out_shape, grid=(t // tm,), in_specs=in_specs, out_specs=out_specs,
        scratch_shapes=[pltpu.VMEM((n_perm, tm, LANES), F32)],
        compiler_params=_params(("parallel",)), name="inproj")(x2d, pos, inv, bfp, *ws)


BIAS_PARTS = 3


def _split3(v):
    p0 = v.astype(BF16)
    r1 = v - p0.astype(F32)
    p1 = r1.astype(BF16)
    p2 = (r1 - p1.astype(F32)).astype(BF16)
    return p0, p1, p2


def _fox_keys_kernel(k_ref, lf_ref, tri_ref, sel_ref, o_ref, carry_ref):
    @pl.when(pl.program_id(1) == 0)
    def _reset():
        carry_ref[...] = jnp.zeros_like(carry_ref)

    tk = lf_ref.shape[0]
    tri = tri_ref[...]
    c = carry_ref[...] + sum(_dot(tri, part) for part in _split3(lf_ref[...]))
    carry_ref[...] = c[tk - 1:tk, :]
    extra = _dot(jnp.concatenate(_split3(-LOG2E * c), axis=1), sel_ref[...])
    low = lax.broadcasted_iota(jnp.int32, (tk, LANES), 1) < HEAD_DIM
    for h in range(B_HEADS):
        pair = k_ref[:, (h // 2) * LANES:(h // 2 + 1) * LANES].astype(F32)
        own = low if h % 2 == 0 else jnp.logical_not(low)
        sl = slice(h * LANES, (h + 1) * LANES)
        o_ref[:, sl] = jnp.where(own, pair, extra[:, sl]).astype(BF16)


def _fox_keys(k3d, lf3d, tk):
    b, s, w = k3d.shape
    tri = jnp.asarray(np.tril(np.ones((tk, tk), np.float32)), BF16)
    sel = np.zeros((BIAS_PARTS * LANES, B_HEADS * LANES), np.float32)
    for part in range(BIAS_PARTS):
        for head in range(B_HEADS):
            sel[part * LANES + head, head * LANES + (HEAD_DIM if head % 2 == 0 else 0) + part] = 1.0
    sel = jnp.asarray(sel, BF16)
    blk = lambda n: pl.BlockSpec((None, tk, n), lambda bi, j: (bi, j, 0))
    return pl.pallas_call(
        _fox_keys_kernel, out_shape=jax.ShapeDtypeStruct((b, s, B_HEADS * LANES), BF16),
        grid=(b, s // tk), in_specs=[blk(w), blk(LANES), _const_spec(tri.shape), _const_spec(sel.shape)],
        out_specs=blk(B_HEADS * LANES), scratch_shapes=[pltpu.VMEM((1, LANES), F32)],
        compiler_params=_params(("parallel", "arbitrary")), name="fox_keys")(k3d, lf3d, tri, sel)


def _flash_kernel(forget, bq, *refs):
    bk = FLASH_BK
    qw = bk // 2
    if forget:
        q_ref, k_ref, vt_ref, o_ref, m_ref, acc_ref, st_ref = refs
    else:
        q_ref, k_ref, vt_ref, lam_ref, gain_ref, o_ref, m_ref, acc_ref, st_ref = refs
    dvp = acc_ref.shape[3]
    dv = dvp - ONES_ROWS
    nq = q_ref.shape[0] // bq
    parts = bq // qw
    lane = lax.broadcasted_iota(jnp.int32, (bq, LANES), 1)
    low = lane < HEAD_DIM
    if forget:
        fill = (jnp.where(lane < HEAD_DIM + BIAS_PARTS, 1.0, 0.0), jnp.where(lane < BIAS_PARTS, 1.0, 0.0))
    else:
        fill = (0.0, 0.0)
    triangle = (lax.broadcasted_iota(jnp.int32, (qw, qw), 0) <= lax.broadcasted_iota(jnp.int32, (qw, qw), 1))

    m_ref[...] = jnp.full(m_ref.shape, NEG, F32)
    acc_ref[...] = jnp.zeros(acc_ref.shape, F32)
    qts = {}

    def queries(i):
        if i not in qts:
            q = q_ref[i * bq:(i + 1) * bq, :].astype(F32)
            qts[i] = (jnp.where(low, q, fill[0]).T.astype(BF16), jnp.where(low, fill[1], q).T.astype(BF16))
        return qts[i]

    def live_rows(i, c, t):
        qb = i * parts + t
        blocks = max(0, min(bk // qw, qb - c * (bk // qw) + 1))
        return blocks * qw, blocks > 0 and c * (bk // qw) + blocks - 1 == qb

    def scores(i, c, buf):
        for s in range(2):
            for t in range(parts):
                rows, _ = live_rows(i, c, t)
                if rows:
                    kc = k_ref[c * bk:c * bk + rows, s * LANES:(s + 1) * LANES] if forget \
                        else k_ref[c * bk:c * bk + rows, :]
                    st_ref[buf, s, t, :rows, :] = _dot(kc, queries(i)[s][:, t * qw:(t + 1) * qw])

    def absorb(i, c, buf):
        for s in range(2):
            for t in range(parts):
                rows, diagonal = live_rows(i, c, t)
                if not rows:
                    continue
                st = st_ref[buf, s, t, :rows, :]
                if diagonal:
                    tri = jnp.where(triangle, st[rows - qw:], NEG)
                    st = tri if rows == qw else jnp.concatenate([st[:rows - qw], tri], axis=0)
                m_prev = m_ref[i, s, t]
                m_new = jnp.maximum(m_prev, jnp.max(st, axis=0, keepdims=True))
                a = jnp.exp2(m_prev - m_new)
                p = jnp.exp2(st - m_new).astype(BF16)
                m_ref[i, s, t] = m_new
                vt = vt_ref[c, s * dvp:(s + 1) * dvp, :rows] if forget else vt_ref[c, :, :rows]
                acc_ref[i, s, t] = a * acc_ref[i, s, t] + _dot(vt, p)

    def finish(i):
        for t in range(parts):
            o0 = acc_ref[i, 0, t, :dv, :] / acc_ref[i, 0, t, dv:dv + 1, :]
            o1 = acc_ref[i, 1, t, :dv, :] / acc_ref[i, 1, t, dv:dv + 1, :]
            rows = slice(i * bq + t * qw, i * bq + (t + 1) * qw)
            if forget:
                o_ref[rows, :] = jnp.concatenate([o0, o1], axis=0).T.astype(o_ref.dtype)
            else:
                o = (o0 - lam_ref[:, 0:1] * o1).T
                ms = jnp.mean(o * o, axis=-1, keepdims=True)
                o_ref[rows, :] = (o * lax.rsqrt(ms + EPS) * gain_ref[...]).astype(o_ref.dtype)

    units = [(i, c) for i in range(nq) for c in range((i + 1) * bq // bk)]
    nbuf = st_ref.shape[0]
    for n in range(min(nbuf - 1, len(units))):
        scores(*units[n], n % nbuf)
    for n, (i, c) in enumerate(units):
        if n + nbuf - 1 < len(units):
            scores(*units[n + nbuf - 1], (n + nbuf - 1) % nbuf)
        absorb(i, c, n % nbuf)
        if c == (i + 1) * bq // bk - 1:
            finish(i)


def _flash(q, k, vt, extra, forget, bq):
    b, s, w = q.shape
    g = w // LANES
    nk = s // FLASH_BK
    dvp = (HEAD_DIM if forget else LANES) + ONES_ROWS
    rows = vt.shape[1] // g
    kw = k.shape[2] // g
    qspec = pl.BlockSpec((None, s, LANES), lambda bi, gi: (bi, 0, gi))
    kspec = pl.BlockSpec((None, s, kw), lambda bi, gi: (bi, 0, gi))
    vspec = pl.BlockSpec((nk, rows, FLASH_BK), lambda bi, gi: (bi, gi, 0))
    assert bq == 2 * FLASH_BK
    bk = FLASH_BK
    nq = s // bq
    qw, parts = bk // 2, 2 * bq // bk
    scratch = [pltpu.VMEM((nq, 2, parts, 1, qw), F32), pltpu.VMEM((nq, 2, parts, dvp, qw), F32),
               pltpu.VMEM((3, 2, parts, bk, qw), F32)]
    especs = [] if forget else [_const_spec((1, LANES)), _const_spec((1, LANES))]
    return pl.pallas_call(
        functools.partial(_flash_kernel, forget, bq),
        out_shape=jax.ShapeDtypeStruct((b, s, w), BF16),
        grid=(b, g), in_specs=[qspec, kspec, vspec] + especs, out_specs=qspec,
        scratch_shapes=scratch,
        compiler_params=_params(("parallel", "parallel")),
        name="fox_attn" if forget else "diff_attn")(q, k, vt, *extra)


def _dilated_kernel(nq, q_ref, kp_ref, kc_ref, vp_ref, vc_ref, o_ref, lse_ref):
    n = C_BLOCK
    pairs = q_ref.shape[1] // LANES
    j = pl.program_id(2)
    kall = jnp.concatenate([kp_ref[...], kc_ref[...]], axis=0)
    vall = jnp.concatenate([vp_ref[...], vc_ref[...]], axis=0)
    qi = lax.broadcasted_iota(jnp.int32, (n, 2 * n), 0)
    kj = lax.broadcasted_iota(jnp.int32, (n, 2 * n), 1)
    dist = n + qi - kj
    window = (dist >= 0) & (dist <= n)
    first_key = jnp.where(j > 0, 0, n)
    lane = lax.broadcasted_iota(jnp.int32, (n, LANES), 1)
    low = lane < HEAD_DIM
    units = [(t, p, s) for t in range(nq) for p in range(pairs) for s in range(2)]
    scores = {}
    for t, p, s in units:
        sl = slice(p * LANES, (p + 1) * LANES)
        q128 = q_ref[t * n:(t + 1) * n, sl]
        zero = jnp.zeros_like(q128)
        qm = jnp.where(low, q128, zero) if s == 0 else jnp.where(low, zero, q128)
        mask = (window & (kj >= first_key)) if t == 0 else window
        scores[t, p, s] = jnp.where(mask, _dot_nt(qm, kall[t * n:(t + 2) * n, sl]), NEG)
    probs = {}
    for u in units:
        m = jnp.max(scores[u], axis=-1, keepdims=True)
        probs[u] = (jnp.exp2(scores[u] - m).astype(BF16), m)
    low_keys = lax.broadcasted_iota(jnp.int32, (2 * n, LANES), 1) < HEAD_DIM
    for t in range(nq):
        for p in range(pairs):
            sl = slice(p * LANES, (p + 1) * LANES)
            v128 = vall[t * n:(t + 2) * n, sl]
            one = jnp.ones_like(v128)
            os, ls = [], []
            for s in range(2):
                vs = jnp.where(low_keys, v128, one) if s == 0 else jnp.where(low_keys, one, v128)
                num = _dot(probs[t, p, s][0], vs)
                den = pltpu.roll(num, HEAD_DIM, 1)
                os.append(num / den)
                ls.append(probs[t, p, s][1] + jnp.log2(den))
            o_ref[t * n:(t + 1) * n, sl] = jnp.where(low, os[0], os[1])
            lse_ref[t * n:(t + 1) * n, sl] = jnp.where(low, ls[0], ls[1])


def _dilated(q, k, v, b, dilation, nq, nr):
    w = C_GW * nr
    n = C_BLOCK
    m_len = q.shape[0] // b
    nb = m_len // (n * nq)
    view = lambda t: t.reshape(b, m_len, dilation * C_GW)
    cur = pl.BlockSpec((None, n * nq, w), lambda bi, r, j: (bi, j, r))
    prev = pl.BlockSpec((None, n, w), lambda bi, r, j: (bi, jnp.maximum(j * nq - 1, 0), r))
    o, lse = pl.pallas_call(
        functools.partial(_dilated_kernel, nq),
        out_shape=[jax.ShapeDtypeStruct((b, m_len, dilation * C_GW), F32)] * 2,
        grid=(b, dilation // nr, nb), in_specs=[cur, prev, cur, prev, cur], out_specs=[cur, cur],
        compiler_params=_params(("parallel", "parallel", "arbitrary")),
        name=f"dilated_d{dilation}")(view(q), view(k), view(k), view(v), view(v))
    return o.reshape(b * m_len, dilation * C_GW), lse.reshape(b * m_len, dilation * C_GW)


def _merge_kernel(x_ref, oa_ref, ob_ref, oc0, oc1, oc2, ls0, ls1, ls2,
                  wg_ref, bg_ref, wa_ref, wb_ref, wc_ref, wo_ref, g_ref, b_ref, o_ref, perm_ref):
    x = x_ref[...]
    xb = x.astype(BF16)
    tm = x.shape[0]

    def token_major(blk_ref, gi, slot):
        d = C_PATTERNS[gi][1]
        if d == 1:
            return blk_ref[...]
        for h in range(C_GW // LANES):
            for r in range(d):
                lo = r * C_GW + h * LANES
                perm_ref.at[slot + h][pl.ds(r, tm // d, stride=d), :] = blk_ref[:, lo:lo + LANES]
        return jnp.concatenate([perm_ref[slot + h] for h in range(C_GW // LANES)], axis=1)

    os = [token_major(r, gi, 2 * gi) for gi, r in enumerate((oc0, oc1, oc2))]
    ls = [token_major(r, gi, 2 * C_GROUPS + 2 * gi) for gi, r in enumerate((ls0, ls1, ls2))]
    for r0 in range(0, tm, SUB_ROWS):
        rows = slice(r0, r0 + SUB_ROWS)
        l0, l1, l2 = (l[rows] for l in ls)
        m = jnp.maximum(jnp.maximum(l0, l1), l2)
        e0, e1, e2 = jnp.exp2(l0 - m), jnp.exp2(l1 - m), jnp.exp2(l2 - m)
        oc = (e0 * os[0][rows] + e1 * os[1][rows] + e2 * os[2][rows]) / (e0 + e1 + e2)
        branches = ((oa_ref[rows, :], wa_ref), (ob_ref[rows, :], wb_ref), (oc.astype(BF16), wc_ref))
        merged = None
        for k, (ob, w_ref) in enumerate(branches):
            sl = slice(k * D_MODEL, (k + 1) * D_MODEL)
            z = _dot(xb[rows], wg_ref[:, sl]) + bg_ref[:, sl]
            gate = 1.0 / (1.0 + jnp.exp(-z))
            term = gate * _dot(ob, w_ref[...])
            merged = term if merged is None else merged + term
        h = _dot(merged.astype(BF16), wo_ref[...])
        o_ref[rows, :] = _layer_norm(DN_ALPHA * x[rows] + h, g_ref[...], b_ref[...])


def _merge(x2d, oa, ob, ocs, lses, wg, bg, wa, wb, wc, wo, g, bta, tm):
    t = x2d.shape[0]
    row = lambda n: pl.BlockSpec((tm, n), lambda i: (i, 0))
    consts = (wg, bg, wa, wb, wc, wo, g, bta)
    cspecs = [pl.BlockSpec((tm // d, d * C_GW), lambda i: (i, 0)) for _, d in C_PATTERNS]
    in_specs = [row(D_MODEL), row(A_W), row(B_W)] + cspecs * 2 + [_const_spec(c.shape) for c in consts]
    return pl.pallas_call(
        _merge_kernel, out_shape=jax.ShapeDtypeStruct((t, D_MODEL), F32), grid=(t // tm,),
        in_specs=in_specs, out_specs=row(D_MODEL),
        scratch_shapes=[pltpu.VMEM((4 * C_GROUPS, tm, LANES), F32)],
        compiler_params=_params(("parallel",)), name="merge")(x2d, oa, ob, *ocs, *lses, *consts)


def _memkv_kernel(m_ref, wk_ref, wv_ref, k_ref, v_ref):
    mb = m_ref[...].astype(BF16)
    k_ref[...] = _dot(mb, wk_ref[...]).astype(BF16)
    v_ref[...] = _dot(mb, wv_ref[...]).astype(BF16)


def _memkv(mem2d, wk, wv, tm):
    t = mem2d.shape[0]
    row = pl.BlockSpec((tm, D_MODEL), lambda i: (i, 0))
    return pl.pallas_call(
        _memkv_kernel, out_shape=[jax.ShapeDtypeStruct((t, D_MODEL), BF16)] * 2, grid=(t // tm,),
        in_specs=[row, _const_spec(wk.shape), _const_spec(wv.shape)], out_specs=[row, row],
        compiler_params=_params(("parallel",)), name="memkv")(mem2d, wk, wv)


def _xattn_kernel(x_ref, k_ref, v_ref, wq_ref, wo_ref, g_ref, b_ref, o_ref):
    for r0 in range(0, x_ref.shape[0], SUB_ROWS):
        rows = slice(r0, r0 + SUB_ROWS)
        x = x_ref[rows, :]
        q = _dot(x.astype(BF16), wq_ref[...]).astype(BF16)
        heads = []
        for h in range(X_HEADS):
            sl = slice(h * X_HEAD_DIM, (h + 1) * X_HEAD_DIM)
            st = _dot_nt(q[:, sl], k_ref[:, sl])
            m = jnp.max(st, axis=-1, keepdims=True)
            e = jnp.exp(st - m)
            l = jnp.sum(e, axis=-1, keepdims=True)
            heads.append((_dot(e.astype(BF16), v_ref[:, sl]) / l).astype(BF16))
        o = jnp.concatenate(heads, axis=-1)
        h_out = _dot(o, wo_ref[...])
        o_ref[rows, :] = _layer_norm(DN_ALPHA * x + h_out, g_ref[...], b_ref[...])


def _xattn(x3d, k3d, v3d, wq, wo, g, bta, tm):
    b, s, d = x3d.shape
    xspec = pl.BlockSpec((None, tm, d), lambda bi, i: (bi, i, 0))
    mspec = pl.BlockSpec((None, MEM_LEN, d), lambda bi, i: (bi, 0, 0))
    consts = (wq, wo, g, bta)
    return pl.pallas_call(
        _xattn_kernel, out_shape=jax.ShapeDtypeStruct((b, s, d), F32), grid=(b, s // tm),
        in_specs=[xspec, mspec, mspec] + [_const_spec(c.shape) for c in consts], out_specs=xspec,
        compiler_params=_params(("parallel", "parallel")), name="xattn")(x3d, k3d, v3d, *consts)


MOE_EPB = 2
ROUTE_LANE0 = N_GROUPS


def _moe_kernel(x_ref, wrh_ref, wrl_ref, br_ref, wg_ref, wu_ref, wd_ref, g_ref, b_ref, o_ref,
                xb_ref, gate_ref):
    half = x_ref.shape[0] // 2
    lane = lax.broadcasted_iota(jnp.int32, (half, LANES), 1)
    lanef = lane.astype(F32)
    for r0 in (0, half):
        rows = slice(r0, r0 + half)
        x = x_ref[rows, :]
        xh = x.astype(BF16)
        xl = (x - xh.astype(F32)).astype(BF16)
        xb_ref[rows, :] = xh
        hh_hl = _dot(xh, jnp.concatenate([wrh_ref[...], wrl_ref[...]], axis=1))
        logits = (hh_hl[:, :LANES] + _dot(xl, wrh_ref[...]) + hh_hl[:, LANES:]) + br_ref[...]
        gl = jnp.where(lane < N_GROUPS, logits, NEG)
        gmax = jnp.max(gl, axis=-1, keepdims=True)
        g_gate = 1.0 / jnp.sum(jnp.exp(gl - gmax), axis=-1, keepdims=True)
        g_idx = jnp.min(jnp.where(gl == gmax, lanef, float(LANES)), axis=-1, keepdims=True)
        e_group = ((lane - ROUTE_LANE0) >> 2).astype(F32)
        sel = (lane >= ROUTE_LANE0) & (e_group == g_idx)
        el = jnp.where(sel, logits, NEG)
        top1 = jnp.max(el, axis=-1, keepdims=True)
        i1 = jnp.min(jnp.where(el == top1, lanef, float(LANES)), axis=-1, keepdims=True)
        el2 = jnp.where(lanef == i1, NEG, el)
        top2 = jnp.max(el2, axis=-1, keepdims=True)
        i2 = jnp.min(jnp.where(el2 == top2, lanef, float(LANES)), axis=-1, keepdims=True)
        r = jnp.exp(top2 - top1)
        w1 = 1.0 / (1.0 + r)
        w2 = r * w1
        gate_ref[rows, :] = jnp.where(lanef == i1, w1, jnp.where(lanef == i2, w2, 0.0)) * g_gate
        o_ref[rows, :] = jnp.zeros((half, D_MODEL), F32)

    for r0 in (0, half):
        rows = slice(r0, r0 + half)
        for e in range(N_EXPERTS):
            xb = xb_ref[rows, :]
            gcol = jnp.sum(jnp.where(lane == ROUTE_LANE0 + e, gate_ref[rows, :], 0.0), axis=-1, keepdims=True)
            a = _dot(xb, wg_ref[e])
            u = _dot(xb, wu_ref[e])
            h = (a / (1.0 + jnp.exp(-a))) * u * gcol
            o_ref[rows, :] += _dot(h.astype(BF16), wd_ref[e])
        o_ref[rows, :] = _layer_norm(DN_ALPHA * x_ref[rows, :] + o_ref[rows, :], g_ref[...], b_ref[...])


def _moe(x2d, wrh, wrl, br, wg, wu, wd, g, bta, tm):
    t = x2d.shape[0]
    row = pl.BlockSpec((tm, D_MODEL), lambda i: (i, 0))
    consts = (wrh, wrl, br, wg, wu, wd, g, bta)
    return pl.pallas_call(
        _moe_kernel, out_shape=jax.ShapeDtypeStruct((t, D_MODEL), F32),
        grid=(t // tm,), in_specs=[row] + [_const_spec(c.shape) for c in consts], out_specs=row,
        scratch_shapes=[pltpu.VMEM((tm, D_MODEL), BF16), pltpu.VMEM((tm, LANES), F32)],
        compiler_params=_params(("parallel",)), name="moe")(x2d, *consts)


def _pad_cols(w, n):
    return jnp.pad(w, ((0, 0), (0, n - w.shape[1])))


def kernel(x, mem, positions, w_in, b_forget, diff_lambda, diff_subln, w_branch_a, w_branch_b, w_branch_c,
           w_gate, b_gate, w_out, w_xq, w_xk, w_xv, w_xo, w_route_group, b_route_group, w_route_expert,
           b_route_expert, w_expert_gate, w_expert_up, w_expert_down, ln_g, ln_b):
    b, s, d = x.shape
    t = b * s
    depth = w_in.shape[0]
    scale = HEAD_DIM ** -0.5

    half = HEAD_DIM // 2
    inv = ROPE_THETA ** (-jnp.arange(half, dtype=F32) / half)
    inv_row = jnp.tile(inv, LANES // half).reshape(1, LANES)
    pos = positions.astype(F32).reshape(t, 1)

    mem2d = mem.reshape(b * MEM_LEN, d)
    x2d = x.reshape(t, d)
    row = lambda v: v.reshape(1, -1).astype(F32)
    for l in range(depth):
        lam_init = 0.8 - 0.6 * math.exp(-0.3 * l)
        offs = [0, A_W, 2 * A_W, 3 * A_W, 3 * A_W + B_W, 3 * A_W + 2 * B_W, 3 * A_W + 3 * B_W]
        offs.append(offs[-1] + B_HEADS)
        offs += [offs[-1] + C_W, offs[-1] + 2 * C_W, offs[-1] + 3 * C_W]
        wl = w_in[l]
        seg = [wl[:, offs[k]:offs[k + 1]] for k in range(10)]
        qa, ka, va, qb, kb, vb, fb, qc, kc, vc = seg
        ws = [(qa * (scale * LOG2E)), ka, va.T, (qb * (scale * LOG2E)), kb, vb.T, _pad_cols(fb, LANES),
              (qc * (scale * LOG2E)), kc, vc]
        ws = [w.astype(BF16) for w in ws]
        bfp = _pad_cols(row(b_forget[l]), LANES)

        outs = _inproj(x2d, pos, inv_row, bfp, ws, tm=512)
        o_qa, o_ka, o_va, o_qb, o_kb, o_vb, o_lf = outs[:7]
        o_qc, o_kc, o_vc = outs[7:10], outs[10:13], outs[13:16]

        lf = diff_lambda[l].astype(F32)
        lam = jnp.exp(jnp.dot(lf[0], lf[1])) - jnp.exp(jnp.dot(lf[2], lf[3])) + lam_init
        lam_row = jnp.full((1, LANES), lam, F32)
        gain = row(diff_subln[l]) * (1.0 - lam_init)
        r3 = lambda v: v.reshape(b, s, v.shape[-1])
        oa = _flash(r3(o_qa), r3(o_ka), o_va, (lam_row, gain), forget=False, bq=2 * FLASH_BK)
        kb_aug = _fox_keys(r3(o_kb), r3(o_lf), tk=512)
        ob = _flash(r3(o_qb), kb_aug, o_vb, (), forget=True, bq=2 * FLASH_BK)
        ocs, lses = [], []
        for gi, (_, dil) in enumerate(C_PATTERNS):
            nq = min(4, s // (dil * C_BLOCK))
            o_g, lse_g = _dilated(o_qc[gi], o_kc[gi], o_vc[gi], b, dil, nq, nr=min(dil, 4 // nq))
            ocs.append(o_g)
            lses.append(lse_g)
        x2d = _merge(x2d, oa.reshape(t, A_W), ob.reshape(t, B_W), ocs, lses,
                     w_gate[l].astype(BF16), row(b_gate[l]), w_branch_a[l].astype(BF16),
                     w_branch_b[l].astype(BF16), w_branch_c[l].astype(BF16), w_out[l].astype(BF16),
                     row(ln_g[l, 0]), row(ln_b[l, 0]), tm=512)
        xscale = X_HEAD_DIM ** -0.5
        mk, mv = _memkv(mem2d, w_xk[l].astype(BF16), w_xv[l].astype(BF16), tm=MEM_LEN)
        x2d = _xattn(x2d.reshape(b, s, d), mk.reshape(b, MEM_LEN, d), mv.reshape(b, MEM_LEN, d),
                     (w_xq[l] * xscale).astype(BF16), w_xo[l].astype(BF16),
                     row(ln_g[l, 1]), row(ln_b[l, 1]), tm=512).reshape(t, d)
        wr = jnp.concatenate([w_route_group[l]] + [w_route_expert[l, gi] for gi in range(N_GROUPS)], axis=1)
        wr = _pad_cols(wr.astype(F32), LANES)
        wrh = wr.astype(BF16)
        wrl = (wr - wrh.astype(F32)).astype(BF16)
        br = _pad_cols(row(jnp.concatenate([b_route_group[l], b_route_expert[l].reshape(-1)])), LANES)
        ne = (N_EXPERTS,)
        x2d = _moe(x2d, wrh, wrl, br,
                   w_expert_gate[l].reshape(ne + w_expert_gate.shape[3:]).astype(BF16),
                   w_expert_up[l].reshape(ne + w_expert_up.shape[3:]).astype(BF16),
                   w_expert_down[l].reshape(ne + w_expert_down.shape[3:]).astype(BF16),
                   row(ln_g[l, 2]), row(ln_b[l, 2]), tm=1024)
    return x2d.reshape(b, s, d)
```

```python
import functools
import math

import jax
import jax.numpy as jnp
import numpy as np
from jax import lax
from jax.experimental import pallas as pl
from jax.experimental.pallas import tpu as pltpu

F32 = jnp.float32
BF16 = jnp.bfloat16

D_MODEL = 1024
HEAD_DIM = 64
ROPE_THETA = 10000.0
EPS = 1e-5
MEM_LEN = 256
A_HEADS = 4
B_HEADS = 8
C_PATTERNS = ((128, 1), (512, 4), (2048, 16))
C_GROUPS = 3
C_SLOTS = 4
C_BLOCK = 128
X_HEADS = 4
X_HEAD_DIM = D_MODEL // X_HEADS
N_GROUPS = 4
EXPERTS_PER_GROUP = 4
N_EXPERTS = N_GROUPS * EXPERTS_PER_GROUP
D_EXPERT = D_MODEL // 4
DEPTH = 2
DN_ALPHA = (2 * DEPTH) ** 0.25

LANES = 128
NEG = -1e30
LOG2E = math.log2(math.e)
VMEM_LIMIT = 56 * 1024 * 1024

A_W = A_HEADS * 2 * HEAD_DIM
B_W = B_HEADS * HEAD_DIM
C_GW = C_SLOTS * HEAD_DIM
C_W = C_GROUPS * C_GW
ONES_ROWS = 16
VA_ROWS = A_HEADS * (2 * HEAD_DIM + ONES_ROWS)
VB_ROWS = B_HEADS * (HEAD_DIM + ONES_ROWS)
SUB_ROWS = 256
FLASH_BK = 512


def _params(sem, flags=None):
    return pltpu.CompilerParams(dimension_semantics=sem, vmem_limit_bytes=VMEM_LIMIT, flags=flags)


def _const_spec(shape):
    nd = len(shape)
    return pl.BlockSpec(shape, lambda *_: (0,) * nd, pipeline_mode=pl.Buffered(1))


def _layer_norm(y, g, b):
    mu = jnp.mean(y, axis=-1, keepdims=True)
    yc = y - mu
    var = jnp.mean(yc * yc, axis=-1, keepdims=True)
    return yc * lax.rsqrt(var + EPS) * g + b


def _dot(a, b):
    return jnp.dot(a, b, preferred_element_type=F32)


def _dot_nt(a, b):
    return lax.dot_general(a, b, (((1,), (1,)), ((), ())), preferred_element_type=F32)


def _inproj_kernel(x_ref, pos_ref, inv_ref, bf_ref,
                   w_qa, w_ka, w_va, w_qb, w_kb, w_vb, w_fb, w_qc, w_kc, w_vc,
                   o_qa, o_ka, o_va, o_qb, o_kb, o_vb, o_lf,
                   o_qc0, o_qc1, o_qc2, o_kc0, o_kc1, o_kc2, o_vc0, o_vc1, o_vc2, perm_ref):
    xb = x_ref[...].astype(BF16)
    ang = pos_ref[...] * inv_ref[...]
    lane = lax.broadcasted_iota(jnp.int32, ang.shape, 1)
    lower = (lane & (HEAD_DIM - 1)) < HEAD_DIM // 2
    cos = jnp.cos(ang)
    sin = jnp.sin(ang)
    sin = jnp.where(lower, -sin, sin)

    def rope(t):
        sw = jnp.where(lower, pltpu.roll(t, LANES - HEAD_DIM // 2, 1), pltpu.roll(t, HEAD_DIM // 2, 1))
        return t * cos + sw * sin

    def roped(w_ref, outs, width):
        y = _dot(xb, w_ref[...])
        per_out = width // LANES
        for c in range(y.shape[1] // LANES):
            o = outs[c // per_out]
            lo = (c % per_out) * LANES
            o[:, lo:lo + LANES] = rope(y[:, c * LANES:(c + 1) * LANES]).astype(BF16)

    def plain(w_ref, outs, width):
        y = _dot(xb, w_ref[...])
        for c, o in enumerate(outs):
            o[...] = y[:, c * width:(c + 1) * width].astype(BF16)

    def transposed(wt_ref, o, width):
        yt = _dot_nt(wt_ref[...], xb)
        ones = jnp.ones((ONES_ROWS, FLASH_BK), F32)
        for c in range(o.shape[0]):
            cols = slice(c * FLASH_BK, (c + 1) * FLASH_BK)
            parts = []
            for r0 in range(0, yt.shape[0], width):
                parts += [yt[r0:r0 + width, cols], ones]
            o[c] = jnp.concatenate(parts, axis=0).astype(BF16)

    roped(w_qa, (o_qa,), A_W)
    roped(w_ka, (o_ka,), A_W)
    transposed(w_va, o_va, LANES)
    plain(w_qb, (o_qb,), B_W)
    plain(w_kb, (o_kb,), B_W)
    transposed(w_vb, o_vb, HEAD_DIM)
    def dilated(w_ref, outs, slot, use_rope):
        y = _dot(xb, w_ref[...])
        tm = y.shape[0]
        for gi, (_, d) in enumerate(C_PATTERNS):
            for h in range(C_GW // LANES):
                c = gi * (C_GW // LANES) + h
                t = y[:, c * LANES:(c + 1) * LANES]
                t = rope(t) if use_rope else t
                if d == 1:
                    outs[gi][:, h * LANES:(h + 1) * LANES] = t.astype(BF16)
                    continue
                sl = slot + (gi - 1) * 2 + h
                perm_ref[sl] = t
                for r in range(d):
                    lo = r * C_GW + h * LANES
                    outs[gi][:, lo:lo + LANES] = perm_ref.at[sl][pl.ds(r, tm // d, stride=d), :].astype(BF16)

    dilated(w_qc, (o_qc0, o_qc1, o_qc2), 0, True)
    dilated(w_kc, (o_kc0, o_kc1, o_kc2), 4, True)
    dilated(w_vc, (o_vc0, o_vc1, o_vc2), 8, False)
    z =_dot(xb, w_fb[...]) + bf_ref[...]
    o_lf[...] = jnp.minimum(z, 0.0) - jnp.log(1.0 + jnp.exp(-jnp.abs(z)))


def _inproj(x2d, pos, inv, bfp, ws, tm):
    t = x2d.shape[0]
    row = lambda n: pl.BlockSpec((tm, n), lambda i: (i, 0))
    tshape = lambda rows: jax.ShapeDtypeStruct((t // FLASH_BK, rows, FLASH_BK), BF16)
    tspec = lambda rows: pl.BlockSpec((tm // FLASH_BK, rows, FLASH_BK), lambda i: (i, 0, 0))
    rshape = lambda w: jax.ShapeDtypeStruct((t, w), BF16)
    out_shape = [rshape(A_W), rshape(A_W), tshape(VA_ROWS), rshape(B_W), rshape(B_W), tshape(VB_ROWS)]
    out_shape.append(jax.ShapeDtypeStruct((t, LANES), F32))
    out_specs = [row(A_W), row(A_W), tspec(VA_ROWS), row(B_W), row(B_W), tspec(VB_ROWS), row(LANES)]
    for _ in range(3):
        for _, d in C_PATTERNS:
            out_shape.append(jax.ShapeDtypeStruct((t // d, d * C_GW), BF16))
            out_specs.append(pl.BlockSpec((tm // d, d * C_GW), lambda i: (i, 0)))
    in_specs = [row(D_MODEL), row(1), _const_spec((1, LANES)), _const_spec((1, LANES))]
    in_specs += [_const_spec(w.shape) for w in ws]
    n_perm = 3 * (C_GROUPS - 1) * (C_GW // LANES)
    return pl.pallas_call(
        _inproj_kernel, out_shape=out_shape, grid=(t // tm,), in_specs=in_specs, out_specs=out_specs,
        scratch_shapes=[pltpu.VMEM((n_perm, tm, LANES), F32)],
        compiler_params=_params(("parallel",)), name="inproj")(x2d, pos, inv, bfp, *ws)


BIAS_PARTS = 3


def _split3(v):
    p0 = v.astype(BF16)
    r1 = v - p0.astype(F32)
    p1 = r1.astype(BF16)
    p2 = (r1 - p1.astype(F32)).astype(BF16)
    return p0, p1, p2


def _fox_keys_kernel(k_ref, lf_ref, tri_ref, sel_ref, o_ref, carry_ref):
    @pl.when(pl.program_id(1) == 0)
    def _reset():
        carry_ref[...] = jnp.zeros_like(carry_ref)

    tk = lf_ref.shape[0]
    tri = tri_ref[...]
    c = carry_ref[...] + sum(_dot(tri, part) for part in _split3(lf_ref[...]))
    carry_ref[...] = c[tk - 1:tk, :]
    lane = lax.broadcasted_iota(jnp.int32, (tk, LANES), 1)
    packed = sum(pltpu.roll(jnp.where(lane < B_HEADS, part.astype(F32), 0.0), B_HEADS * p, 1)
                 for p, part in enumerate(_split3(-LOG2E * c)))
    extra = _dot(packed.astype(BF16), sel_ref[...])
    low = lane < HEAD_DIM
    for h in range(B_HEADS):
        pair = k_ref[:, (h // 2) * LANES:(h // 2 + 1) * LANES].astype(F32)
        own = low if h % 2 == 0 else jnp.logical_not(low)
        sl = slice(h * LANES, (h + 1) * LANES)
        o_ref[:, sl] = jnp.where(own, pair, extra[:, sl]).astype(BF16)


def _fox_keys(k3d, lf3d, tk):
    b, s, w = k3d.shape
    tri = jnp.asarray(np.tril(np.ones((tk, tk), np.float32)), BF16)
    sel = np.zeros((LANES, B_HEADS * LANES), np.float32)
    for part in range(BIAS_PARTS):
        for head in range(B_HEADS):
            sel[part * B_HEADS + head, head * LANES + (HEAD_DIM if head % 2 == 0 else 0) + part] = 1.0
    sel = jnp.asarray(sel, BF16)
    blk = lambda n: pl.BlockSpec((None, tk, n), lambda bi, j: (bi, j, 0))
    return pl.pallas_call(
        _fox_keys_kernel, out_shape=jax.ShapeDtypeStruct((b, s, B_HEADS * LANES), BF16),
        grid=(b, s // tk), in_specs=[blk(w), blk(LANES), _const_spec(tri.shape), _const_spec(sel.shape)],
        out_specs=blk(B_HEADS * LANES), scratch_shapes=[pltpu.VMEM((1, LANES), F32)],
        compiler_params=_params(("parallel", "arbitrary")), name="fox_keys")(k3d, lf3d, tri, sel)


def _flash_kernel(forget, bq, *refs):
    bk = FLASH_BK
    qw = bk // 2
    if forget:
        q_ref, k_ref, vt_ref, o_ref, m_ref, acc_ref, st_ref = refs
    else:
        q_ref, k_ref, vt_ref, lam_ref, gain_ref, o_ref, m_ref, acc_ref, st_ref = refs
    dvp = acc_ref.shape[3]
    dv = dvp - ONES_ROWS
    nq = q_ref.shape[0] // bq
    parts = bq // qw
    lane = lax.broadcasted_iota(jnp.int32, (bq, LANES), 1)
    low = lane < HEAD_DIM
    if forget:
        fill = (jnp.where(lane < HEAD_DIM + BIAS_PARTS, 1.0, 0.0), jnp.where(lane < BIAS_PARTS, 1.0, 0.0))
    else:
        fill = (0.0, 0.0)
    triangle = (lax.broadcasted_iota(jnp.int32, (qw, qw), 0) <= lax.broadcasted_iota(jnp.int32, (qw, qw), 1))

    m_ref[...] = jnp.full(m_ref.shape, NEG, F32)
    acc_ref[...] = jnp.zeros(acc_ref.shape, F32)
    qts = {}

    def queries(i):
        if i not in qts:
            q = q_ref[i * bq:(i + 1) * bq, :].astype(F32)
            qts[i] = (jnp.where(low, q, fill[0]).T.astype(BF16), jnp.where(low, fill[1], q).T.astype(BF16))
        return qts[i]

    def live_rows(i, c, t):
        qb = i * parts + t
        blocks = max(0, min(bk // qw, qb - c * (bk // qw) + 1))
        return blocks * qw, blocks > 0 and c * (bk // qw) + blocks - 1 == qb

    def scores(i, c, buf):
        for s in range(2):
            for t in range(parts):
                rows, _ = live_rows(i, c, t)
                if rows:
                    kc = k_ref[c * bk:c * bk + rows, s * LANES:(s + 1) * LANES] if forget \
                        else k_ref[c * bk:c * bk + rows, :]
                    st_ref[buf, s, t, :rows, :] = _dot(kc, queries(i)[s][:, t * qw:(t + 1) * qw])

    def absorb(i, c, buf):
        for s in range(2):
            for t in range(parts):
                rows, diagonal = live_rows(i, c, t)
                if not rows:
                    continue
                st = st_ref[buf, s, t, :rows, :]
                if diagonal:
                    tri = jnp.where(triangle, st[rows - qw:], NEG)
                    st = tri if rows == qw else jnp.concatenate([st[:rows - qw], tri], axis=0)
                m_prev = m_ref[i, s, t]
                m_new = jnp.maximum(m_prev, jnp.max(st, axis=0, keepdims=True))
                a = jnp.exp2(m_prev - m_new)
                p = jnp.exp2(st - m_new).astype(BF16)
                m_ref[i, s, t] = m_new
                vt = vt_ref[c, s * dvp:(s + 1) * dvp, :rows] if forget else vt_ref[c, :, :rows]
                acc_ref[i, s, t] = a * acc_ref[i, s, t] + _dot(vt, p)

    def finish(i):
        for t in range(parts):
            o0 = acc_ref[i, 0, t, :dv, :] / acc_ref[i, 0, t, dv:dv + 1, :]
            o1 = acc_ref[i, 1, t, :dv, :] / acc_ref[i, 1, t, dv:dv + 1, :]
            rows = slice(i * bq + t * qw, i * bq + (t + 1) * qw)
            if forget:
                o_ref[rows, :] = jnp.concatenate([o0, o1], axis=0).T.astype(o_ref.dtype)
            else:
                o = (o0 - lam_ref[:, 0:1] * o1).T
                ms = jnp.mean(o * o, axis=-1, keepdims=True)
                o_ref[rows, :] = (o * lax.rsqrt(ms + EPS) * gain_ref[...]).astype(o_ref.dtype)

    units = [(i, c) for i in range(nq) for c in range((i + 1) * bq // bk)]
    nbuf = st_ref.shape[0]
    for n in range(min(nbuf - 1, len(units))):
        scores(*units[n], n % nbuf)
    for n, (i, c) in enumerate(units):
        if n + nbuf - 1 < len(units):
            scores(*units[n + nbuf - 1], (n + nbuf - 1) % nbuf)
        absorb(i, c, n % nbuf)
        if c == (i + 1) * bq // bk - 1:
            finish(i)


def _flash(q, k, vt, extra, forget, bq):
    b, s, w = q.shape
    g = w // LANES
    nk = s // FLASH_BK
    dvp = (HEAD_DIM if forget else LANES) + ONES_ROWS
    rows = vt.shape[1] // g
    kw = k.shape[2] // g
    qspec = pl.BlockSpec((None, s, LANES), lambda bi, gi: (bi, 0, gi))
    kspec = pl.BlockSpec((None, s, kw), lambda bi, gi: (bi, 0, gi))
    vspec = pl.BlockSpec((nk, rows, FLASH_BK), lambda bi, gi: (bi, gi, 0))
    assert bq == 2 * FLASH_BK
    bk = FLASH_BK
    nq = s // bq
    qw, parts = bk // 2, 2 * bq // bk
    scratch = [pltpu.VMEM((nq, 2, parts, 1, qw), F32), pltpu.VMEM((nq, 2, parts, dvp, qw), F32),
               pltpu.VMEM((3, 2, parts, bk, qw), F32)]
    especs = [] if forget else [_const_spec((1, LANES)), _const_spec((1, LANES))]
    return pl.pallas_call(
        functools.partial(_flash_kernel, forget, bq),
        out_shape=jax.ShapeDtypeStruct((b, s, w), BF16),
        grid=(b, g), in_specs=[qspec, kspec, vspec] + especs, out_specs=qspec,
        scratch_shapes=scratch,
        compiler_params=_params(("parallel", "parallel")),
        name="fox_attn" if forget else "diff_attn")(q, k, vt, *extra)


def _dilated_kernel(nq, q_ref, kp_ref, kc_ref, vp_ref, vc_ref, o_ref, lse_ref):
    n = C_BLOCK
    pairs = q_ref.shape[1] // LANES
    j = pl.program_id(2)
    kall = jnp.concatenate([kp_ref[...], kc_ref[...]], axis=0)
    vall = jnp.concatenate([vp_ref[...], vc_ref[...]], axis=0)
    qi = lax.broadcasted_iota(jnp.int32, (n, 2 * n), 0)
    kj = lax.broadcasted_iota(jnp.int32, (n, 2 * n), 1)
    dist = n + qi - kj
    window = (dist >= 0) & (dist <= n)
    first_key = jnp.where(j > 0, 0, n)
    lane = lax.broadcasted_iota(jnp.int32, (n, LANES), 1)
    low = lane < HEAD_DIM
    units = [(t, p, s) for t in range(nq) for p in range(pairs) for s in range(2)]
    scores = {}
    for t, p, s in units:
        sl = slice(p * LANES, (p + 1) * LANES)
        q128 = q_ref[t * n:(t + 1) * n, sl]
        zero = jnp.zeros_like(q128)
        qm = jnp.where(low, q128, zero) if s == 0 else jnp.where(low, zero, q128)
        mask = (window & (kj >= first_key)) if t == 0 else window
        scores[t, p, s] = jnp.where(mask, _dot_nt(qm, kall[t * n:(t + 2) * n, sl]), NEG)
    probs = {}
    for u in units:
        m = jnp.max(scores[u], axis=-1, keepdims=True)
        probs[u] = (jnp.exp2(scores[u] - m).astype(BF16), m)
    low_keys = lax.broadcasted_iota(jnp.int32, (2 * n, LANES), 1) < HEAD_DIM
    for t in range(nq):
        for p in range(pairs):
            sl = slice(p * LANES, (p + 1) * LANES)
            v128 = vall[t * n:(t + 2) * n, sl]
            one = jnp.ones_like(v128)
            os, ls = [], []
            for s in range(2):
                vs = jnp.where(low_keys, v128, one) if s == 0 else jnp.where(low_keys, one, v128)
                num = _dot(probs[t, p, s][0], vs)
                den = pltpu.roll(num, HEAD_DIM, 1)
                os.append(num / den)
                ls.append(probs[t, p, s][1] + jnp.log2(den))
            o_ref[t * n:(t + 1) * n, sl] = jnp.where(low, os[0], os[1])
            lse_ref[t * n:(t + 1) * n, sl] = jnp.where(low, ls[0], ls[1])


def _dilated(q, k, v, b, dilation, nq, nr):
    w = C_GW * nr
    n = C_BLOCK
    m_len = q.shape[0] // b
    nb = m_len // (n * nq)
    view = lambda t: t.reshape(b, m_len, dilation * C_GW)
    cur = pl.BlockSpec((None, n * nq, w), lambda bi, r, j: (bi, j, r))
    prev = pl.BlockSpec((None, n, w), lambda bi, r, j: (bi, jnp.maximum(j * nq - 1, 0), r))
    o, lse = pl.pallas_call(
        functools.partial(_dilated_kernel, nq),
        out_shape=[jax.ShapeDtypeStruct((b, m_len, dilation * C_GW), F32)] * 2,
        grid=(b, dilation // nr, nb), in_specs=[cur, prev, cur, prev, cur], out_specs=[cur, cur],
        compiler_params=_params(("parallel", "parallel", "arbitrary")),
        name=f"dilated_d{dilation}")(view(q), view(k), view(k), view(v), view(v))
    return o.reshape(b * m_len, dilation * C_GW), lse.reshape(b * m_len, dilation * C_GW)


def _merge_kernel(x_ref, oa_ref, ob_ref, oc0, oc1, oc2, ls0, ls1, ls2,
                  wg_ref, bg_ref, wa_ref, wb_ref, wc_ref, wo_ref, g_ref, b_ref, o_ref, perm_ref):
    x = x_ref[...]
    xb = x.astype(BF16)
    tm = x.shape[0]

    def token_major(blk_ref, gi, slot):
        d = C_PATTERNS[gi][1]
        if d == 1:
            return blk_ref[...]
        for h in range(C_GW // LANES):
            for r in range(d):
                lo = r * C_GW + h * LANES
                perm_ref.at[slot + h][pl.ds(r, tm // d, stride=d), :] = blk_ref[:, lo:lo + LANES]
        return jnp.concatenate([perm_ref[slot + h] for h in range(C_GW // LANES)], axis=1)

    os = [token_major(r, gi, 2 * gi) for gi, r in enumerate((oc0, oc1, oc2))]
    ls = [token_major(r, gi, 2 * C_GROUPS + 2 * gi) for gi, r in enumerate((ls0, ls1, ls2))]
    for r0 in range(0, tm, SUB_ROWS):
        rows = slice(r0, r0 + SUB_ROWS)
        l0, l1, l2 = (l[rows] for l in ls)
        m = jnp.maximum(jnp.maximum(l0, l1), l2)
        e0, e1, e2 = jnp.exp2(l0 - m), jnp.exp2(l1 - m), jnp.exp2(l2 - m)
        oc = (e0 * os[0][rows] + e1 * os[1][rows] + e2 * os[2][rows]) / (e0 + e1 + e2)
        branches = ((oa_ref[rows, :], wa_ref), (ob_ref[rows, :], wb_ref), (oc.astype(BF16), wc_ref))
        merged = None
        for k, (ob, w_ref) in enumerate(branches):
            sl = slice(k * D_MODEL, (k + 1) * D_MODEL)
            z = _dot(xb[rows], wg_ref[:, sl]) + bg_ref[:, sl]
            gate = 1.0 / (1.0 + jnp.exp(-z))
            term = gate * _dot(ob, w_ref[...])
            merged = term if merged is None else merged + term
        h = _dot(merged.astype(BF16), wo_ref[...])
        o_ref[rows, :] = _layer_norm(DN_ALPHA * x[rows] + h, g_ref[...], b_ref[...])


def _merge(x2d, oa, ob, ocs, lses, wg, bg, wa, wb, wc, wo, g, bta, tm):
    t = x2d.shape[0]
    row = lambda n: pl.BlockSpec((tm, n), lambda i: (i, 0))
    consts = (wg, bg, wa, wb, wc, wo, g, bta)
    cspecs = [pl.BlockSpec((tm // d, d * C_GW), lambda i: (i, 0)) for _, d in C_PATTERNS]
    in_specs = [row(D_MODEL), row(A_W), row(B_W)] + cspecs * 2 + [_const_spec(c.shape) for c in consts]
    return pl.pallas_call(
        _merge_kernel, out_shape=jax.ShapeDtypeStruct((t, D_MODEL), F32), grid=(t // tm,),
        in_specs=in_specs, out_specs=row(D_MODEL),
        scratch_shapes=[pltpu.VMEM((4 * C_GROUPS, tm, LANES), F32)],
        compiler_params=_params(("parallel",)), name="merge")(x2d, oa, ob, *ocs, *lses, *consts)


def _memkv_kernel(m_ref, wk_ref, wv_ref, k_ref, v_ref):
    mb = m_ref[...].astype(BF16)
    k_ref[...] = _dot(mb, wk_ref[...]).astype(BF16)
    v_ref[...] = _dot(mb, wv_ref[...]).astype(BF16)


def _memkv(mem2d, wk, wv, tm):
    t = mem2d.shape[0]
    row = pl.BlockSpec((tm, D_MODEL), lambda i: (i, 0))
    return pl.pallas_call(
        _memkv_kernel, out_shape=[jax.ShapeDtypeStruct((t, D_MODEL), BF16)] * 2, grid=(t // tm,),
        in_specs=[row, _const_spec(wk.shape), _const_spec(wv.shape)], out_specs=[row, row],
        compiler_params=_params(("parallel",)), name="memkv")(mem2d, wk, wv)


def _xattn_kernel(x_ref, k_ref, v_ref, wq_ref, wo_ref, g_ref, b_ref, o_ref):
    for r0 in range(0, x_ref.shape[0], SUB_ROWS):
        rows = slice(r0, r0 + SUB_ROWS)
        x = x_ref[rows, :]
        q = _dot(x.astype(BF16), wq_ref[...]).astype(BF16)
        heads = []
        for h in range(X_HEADS):
            sl = slice(h * X_HEAD_DIM, (h + 1) * X_HEAD_DIM)
            st = _dot_nt(q[:, sl], k_ref[:, sl])
            m = jnp.max(st, axis=-1, keepdims=True)
            e = jnp.exp(st - m)
            l = jnp.sum(e, axis=-1, keepdims=True)
            heads.append((_dot(e.astype(BF16), v_ref[:, sl]) / l).astype(BF16))
        o = jnp.concatenate(heads, axis=-1)
        h_out = _dot(o, wo_ref[...])
        o_ref[rows, :] = _layer_norm(DN_ALPHA * x + h_out, g_ref[...], b_ref[...])


def _xattn(x3d, k3d, v3d, wq, wo, g, bta, tm):
    b, s, d = x3d.shape
    xspec = pl.BlockSpec((None, tm, d), lambda bi, i: (bi, i, 0))
    mspec = pl.BlockSpec((None, MEM_LEN, d), lambda bi, i: (bi, 0, 0))
    consts = (wq, wo, g, bta)
    return pl.pallas_call(
        _xattn_kernel, out_shape=jax.ShapeDtypeStruct((b, s, d), F32), grid=(b, s // tm),
        in_specs=[xspec, mspec, mspec] + [_const_spec(c.shape) for c in consts], out_specs=xspec,
        compiler_params=_params(("parallel", "parallel")), name="xattn")(x3d, k3d, v3d, *consts)


MOE_EPB = 2
ROUTE_LANE0 = N_GROUPS


def _moe_kernel(x_ref, wrh_ref, wrl_ref, br_ref, wg_ref, wu_ref, wd_ref, g_ref, b_ref, o_ref,
                xb_ref, gate_ref):
    half = x_ref.shape[0] // 2
    lane = lax.broadcasted_iota(jnp.int32, (half, LANES), 1)
    lanef = lane.astype(F32)
    for r0 in (0, half):
        rows = slice(r0, r0 + half)
        x = x_ref[rows, :]
        xh = x.astype(BF16)
        xl = (x - xh.astype(F32)).astype(BF16)
        xb_ref[rows, :] = xh
        hh_hl = _dot(xh, jnp.concatenate([wrh_ref[...], wrl_ref[...]], axis=1))
        logits = (hh_hl[:, :LANES] + _dot(xl, wrh_ref[...]) + hh_hl[:, LANES:]) + br_ref[...]
        gl = jnp.where(lane < N_GROUPS, logits, NEG)
        gmax = jnp.max(gl, axis=-1, keepdims=True)
        g_gate = 1.0 / jnp.sum(jnp.exp(gl - gmax), axis=-1, keepdims=True)
        g_idx = jnp.min(jnp.where(gl == gmax, lanef, float(LANES)), axis=-1, keepdims=True)
        e_group = ((lane - ROUTE_LANE0) >> 2).astype(F32)
        sel = (lane >= ROUTE_LANE0) & (e_group == g_idx)
        el = jnp.where(sel, logits, NEG)
        top1 = jnp.max(el, axis=-1, keepdims=True)
        i1 = jnp.min(jnp.where(el == top1, lanef, float(LANES)), axis=-1, keepdims=True)
        el2 = jnp.where(lanef == i1, NEG, el)
        top2 = jnp.max(el2, axis=-1, keepdims=True)
        i2 = jnp.min(jnp.where(el2 == top2, lanef, float(LANES)), axis=-1, keepdims=True)
        r = jnp.exp(top2 - top1)
        w1 = 1.0 / (1.0 + r)
        w2 = r * w1
        gate_ref[rows, :] = jnp.where(lanef == i1, w1, jnp.where(lanef == i2, w2, 0.0)) * g_gate
        o_ref[rows, :] = jnp.zeros((half, D_MODEL), F32)

    for r0 in (0, half):
        rows = slice(r0, r0 + half)
        for e in range(N_EXPERTS):
            xb = xb_ref[rows, :]
            gcol = jnp.sum(jnp.where(lane == ROUTE_LANE0 + e, gate_ref[rows, :], 0.0), axis=-1, keepdims=True)
            a = _dot(xb, wg_ref[e])
            u = _dot(xb, wu_ref[e])
            h = (a / (1.0 + jnp.exp(-a))) * u * gcol
            o_ref[rows, :] += _dot(h.astype(BF16), wd_ref[e])
        o_ref[rows, :] = _layer_norm(DN_ALPHA * x_ref[rows, :] + o_ref[rows, :], g_ref[...], b_ref[...])


def _moe(x2d, wrh, wrl, br, wg, wu, wd, g, bta, tm):
    t = x2d.shape[0]
    row = pl.BlockSpec((tm, D_MODEL), lambda i: (i, 0))
    consts = (wrh, wrl, br, wg, wu, wd, g, bta)
    return pl.pallas_call(
        _moe_kernel, out_shape=jax.ShapeDtypeStruct((t, D_MODEL), F32),
        grid=(t // tm,), in_specs=[row] + [_const_spec(c.shape) for c in consts], out_specs=row,
        scratch_shapes=[pltpu.VMEM((tm, D_MODEL), BF16), pltpu.VMEM((tm, LANES), F32)],
        compiler_params=_params(("parallel",)), name="moe")(x2d, *consts)


def _pad_cols(w, n):
    return jnp.pad(w, ((0, 0), (0, n - w.shape[1])))


def kernel(x, mem, positions, w_in, b_forget, diff_lambda, diff_subln, w_branch_a, w_branch_b, w_branch_c,
           w_gate, b_gate, w_out, w_xq, w_xk, w_xv, w_xo, w_route_group, b_route_group, w_route_expert,
           b_route_expert, w_expert_gate, w_expert_up, w_expert_down, ln_g, ln_b):
    b, s, d = x.shape
    t = b * s
    depth = w_in.shape[0]
    scale = HEAD_DIM ** -0.5

    half = HEAD_DIM // 2
    inv = ROPE_THETA ** (-jnp.arange(half, dtype=F32) / half)
    inv_row = jnp.tile(inv, LANES // half).reshape(1, LANES)
    pos = positions.astype(F32).reshape(t, 1)

    mem2d = mem.reshape(b * MEM_LEN, d)
    x2d = x.reshape(t, d)
    row = lambda v: v.reshape(1, -1).astype(F32)
    for l in range(depth):
        lam_init = 0.8 - 0.6 * math.exp(-0.3 * l)
        offs = [0, A_W, 2 * A_W, 3 * A_W, 3 * A_W + B_W, 3 * A_W + 2 * B_W, 3 * A_W + 3 * B_W]
        offs.append(offs[-1] + B_HEADS)
        offs += [offs[-1] + C_W, offs[-1] + 2 * C_W, offs[-1] + 3 * C_W]
        wl = w_in[l]
        seg = [wl[:, offs[k]:offs[k + 1]] for k in range(10)]
        qa, ka, va, qb, kb, vb, fb, qc, kc, vc = seg
        ws = [(qa * (scale * LOG2E)), ka, va.T, (qb * (scale * LOG2E)), kb, vb.T, _pad_cols(fb, LANES),
              (qc * (scale * LOG2E)), kc, vc]
        ws = [w.astype(BF16) for w in ws]
        bfp = _pad_cols(row(b_forget[l]), LANES)

        outs = _inproj(x2d, pos, inv_row, bfp, ws, tm=512)
        o_qa, o_ka, o_va, o_qb, o_kb, o_vb, o_lf = outs[:7]
        o_qc, o_kc, o_vc = outs[7:10], outs[10:13], outs[13:16]

        lf = diff_lambda[l].astype(F32)
        lam = jnp.exp(jnp.dot(lf[0], lf[1])) - jnp.exp(jnp.dot(lf[2], lf[3])) + lam_init
        lam_row = jnp.full((1, LANES), lam, F32)
        gain = row(diff_subln[l]) * (1.0 - lam_init)
        r3 = lambda v: v.reshape(b, s, v.shape[-1])
        oa = _flash(r3(o_qa), r3(o_ka), o_va, (lam_row, gain), forget=False, bq=2 * FLASH_BK)
        kb_aug = _fox_keys(r3(o_kb), r3(o_lf), tk=512)
        ob = _flash(r3(o_qb), kb_aug, o_vb, (), forget=True, bq=2 * FLASH_BK)
        ocs, lses = [], []
        for gi, (_, dil) in enumerate(C_PATTERNS):
            nq = min(8 if dil < 16 else 4, s // (dil * C_BLOCK))
            o_g, lse_g = _dilated(o_qc[gi], o_kc[gi], o_vc[gi], b, dil, nq, nr=max(1, min(dil, 4 // nq)))
            ocs.append(o_g)
            lses.append(lse_g)
        x2d = _merge(x2d, oa.reshape(t, A_W), ob.reshape(t, B_W), ocs, lses,
                     w_gate[l].astype(BF16), row(b_gate[l]), w_branch_a[l].astype(BF16),
                     w_branch_b[l].astype(BF16), w_branch_c[l].astype(BF16), w_out[l].astype(BF16),
                     row(ln_g[l, 0]), row(ln_b[l, 0]), tm=512)
        xscale = X_HEAD_DIM ** -0.5
        mk, mv = _memkv(mem2d, w_xk[l].astype(BF16), w_xv[l].astype(BF16), tm=MEM_LEN)
        x2d = _xattn(x2d.reshape(b, s, d), mk.reshape(b, MEM_LEN, d), mv.reshape(b, MEM_LEN, d),
                     (w_xq[l] * xscale).astype(BF16), w_xo[l].astype(BF16),
                     row(ln_g[l, 1]), row(ln_b[l, 1]), tm=512).reshape(t, d)
        wr = jnp.concatenate([w_route_group[l]] + [w_route_expert[l, gi] for gi in range(N_GROUPS)], axis=1)
        wr = _pad_cols(wr.astype(F32), LANES)
        wrh = wr.astype(BF16)
        wrl = (wr - wrh.astype(F32)).astype(BF16)
        br = _pad_cols(row(jnp.concatenate([b_route_group[l], b_route_expert[l].reshape(-1)])), LANES)
        ne = (N_EXPERTS,)
        x2d = _moe(x2d, wrh, wrl, br,
                   w_expert_gate[l].reshape(ne + w_expert_gate.shape[3:]).astype(BF16),
                   w_expert_up[l].reshape(ne + w_expert_up.shape[3:]).astype(BF16),
                   w_expert_down[l].reshape(ne + w_expert_down.shape[3:]).astype(BF16),
                   row(ln_g[l, 2]), row(ln_b[l, 2]), tm=1024)
    return x2d.reshape(b, s, d)
```

```python
import functools
import math

import jax
import jax.numpy as jnp
import numpy as np
from jax import lax
from jax.experimental import pallas as pl
from jax.experimental.pallas import tpu as pltpu

F32 = jnp.float32
BF16 = jnp.bfloat16

D_MODEL = 1024
HEAD_DIM = 64
ROPE_THETA = 10000.0
EPS = 1e-5
MEM_LEN = 256
A_HEADS = 4
B_HEADS = 8
C_PATTERNS = ((128, 1), (512, 4), (2048, 16))
C_GROUPS = 3
C_SLOTS = 4
C_BLOCK = 128
X_HEADS = 4
X_HEAD_DIM = D_MODEL // X_HEADS
N_GROUPS = 4
EXPERTS_PER_GROUP = 4
N_EXPERTS = N_GROUPS * EXPERTS_PER_GROUP
D_EXPERT = D_MODEL // 4
DEPTH = 2
DN_ALPHA = (2 * DEPTH) ** 0.25

LANES = 128
NEG = -1e30
LOG2E = math.log2(math.e)
VMEM_LIMIT = 56 * 1024 * 1024

A_W = A_HEADS * 2 * HEAD_DIM
B_W = B_HEADS * HEAD_DIM
C_GW = C_SLOTS * HEAD_DIM
C_W = C_GROUPS * C_GW
ONES_ROWS = 16
VA_ROWS = A_HEADS * (2 * HEAD_DIM + ONES_ROWS)
VB_ROWS = B_HEADS * (HEAD_DIM + ONES_ROWS)
SUB_ROWS = 256
FLASH_BK = 512


def _params(sem, flags=None):
    return pltpu.CompilerParams(dimension_semantics=sem, vmem_limit_bytes=VMEM_LIMIT, flags=flags)


def _const_spec(shape):
    nd = len(shape)
    return pl.BlockSpec(shape, lambda *_: (0,) * nd, pipeline_mode=pl.Buffered(1))


def _layer_norm(y, g, b):
    mu = jnp.mean(y, axis=-1, keepdims=True)
    yc = y - mu
    var = jnp.mean(yc * yc, axis=-1, keepdims=True)
    return yc * lax.rsqrt(var + EPS) * g + b


def _dot(a, b):
    return jnp.dot(a, b, preferred_element_type=F32)


def _dot_nt(a, b):
    return lax.dot_general(a, b, (((1,), (1,)), ((), ())), preferred_element_type=F32)


def _inproj_kernel(x_ref, pos_ref, inv_ref, bf_ref,
                   w_qa, w_ka, w_va, w_qb, w_kb, w_vb, w_fb, w_qc, w_kc, w_vc,
                   o_qa, o_ka, o_va, o_qb, o_kb, o_vb, o_lf,
                   o_qc0, o_qc1, o_qc2, o_kc0, o_kc1, o_kc2, o_vc0, o_vc1, o_vc2, perm_ref):
    xb = x_ref[...].astype(BF16)
    ang = pos_ref[...] * inv_ref[...]
    lane = lax.broadcasted_iota(jnp.int32, ang.shape, 1)
    lower = (lane & (HEAD_DIM - 1)) < HEAD_DIM // 2
    cos = jnp.cos(ang)
    sin = jnp.sin(ang)
    sin = jnp.where(lower, -sin, sin)

    def rope(t):
        sw = jnp.where(lower, pltpu.roll(t, LANES - HEAD_DIM // 2, 1), pltpu.roll(t, HEAD_DIM // 2, 1))
        return t * cos + sw * sin

    def roped(w_ref, outs, width):
        y = _dot(xb, w_ref[...])
        per_out = width // LANES
        for c in range(y.shape[1] // LANES):
            o = outs[c // per_out]
            lo = (c % per_out) * LANES
            o[:, lo:lo + LANES] = rope(y[:, c * LANES:(c + 1) * LANES]).astype(BF16)

    def plain(w_ref, outs, width):
        y = _dot(xb, w_ref[...])
        for c, o in enumerate(outs):
            o[...] = y[:, c * width:(c + 1) * width].astype(BF16)

    def transposed(wt_ref, o, width):
        yt = _dot_nt(wt_ref[...], xb)
        ones = jnp.ones((ONES_ROWS, FLASH_BK), F32)
        for c in range(o.shape[0]):
            cols = slice(c * FLASH_BK, (c + 1) * FLASH_BK)
            parts = []
            for r0 in range(0, yt.shape[0], width):
                parts += [yt[r0:r0 + width, cols], ones]
            o[c] = jnp.concatenate(parts, axis=0).astype(BF16)

    roped(w_qa, (o_qa,), A_W)
    roped(w_ka, (o_ka,), A_W)
    transposed(w_va, o_va, LANES)
    plain(w_qb, (o_qb,), B_W)
    plain(w_kb, (o_kb,), B_W)
    transposed(w_vb, o_vb, HEAD_DIM)
    def dilated(w_ref, outs, slot, use_rope):
        y = _dot(xb, w_ref[...])
        tm = y.shape[0]
        for gi, (_, d) in enumerate(C_PATTERNS):
            for h in range(C_GW // LANES):
                c = gi * (C_GW // LANES) + h
                t = y[:, c * LANES:(c + 1) * LANES]
                t = rope(t) if use_rope else t
                if d == 1:
                    outs[gi][:, h * LANES:(h + 1) * LANES] = t.astype(BF16)
                    continue
                sl = slot + (gi - 1) * 2 + h
                perm_ref[sl] = t
                for r in range(d):
                    lo = r * C_GW + h * LANES
                    outs[gi][:, lo:lo + LANES] = perm_ref.at[sl][pl.ds(r, tm // d, stride=d), :].astype(BF16)

    dilated(w_qc, (o_qc0, o_qc1, o_qc2), 0, True)
    dilated(w_kc, (o_kc0, o_kc1, o_kc2), 4, True)
    dilated(w_vc, (o_vc0, o_vc1, o_vc2), 8, False)
    z =_dot(xb, w_fb[...]) + bf_ref[...]
    o_lf[...] = jnp.minimum(z, 0.0) - jnp.log(1.0 + jnp.exp(-jnp.abs(z)))


def _inproj(x2d, pos, inv, bfp, ws, tm):
    t = x2d.shape[0]
    row = lambda n: pl.BlockSpec((tm, n), lambda i: (i, 0))
    tshape = lambda rows: jax.ShapeDtypeStruct((t // FLASH_BK, rows, FLASH_BK), BF16)
    tspec = lambda rows: pl.BlockSpec((tm // FLASH_BK, rows, FLASH_BK), lambda i: (i, 0, 0))
    rshape = lambda w: jax.ShapeDtypeStruct((t, w), BF16)
    out_shape = [rshape(A_W), rshape(A_W), tshape(VA_ROWS), rshape(B_W), rshape(B_W), tshape(VB_ROWS)]
    out_shape.append(jax.ShapeDtypeStruct((t, LANES), F32))
    out_specs = [row(A_W), row(A_W), tspec(VA_ROWS), row(B_W), row(B_W), tspec(VB_ROWS), row(LANES)]
    for _ in range(3):
        for _, d in C_PATTERNS:
            out_shape.append(jax.ShapeDtypeStruct((t // d, d * C_GW), BF16))
            out_specs.append(pl.BlockSpec((tm // d, d * C_GW), lambda i: (i, 0)))
    in_specs = [row(D_MODEL), row(1), _const_spec((1, LANES)), _const_spec((1, LANES))]
    in_specs += [_const_spec(w.shape) for w in ws]
    n_perm = 3 * (C_GROUPS - 1) * (C_GW // LANES)
    return pl.pallas_call(
        _inproj_kernel, out_shape=out_shape, grid=(t // tm,), in_specs=in_specs, out_specs=out_specs,
        scratch_shapes=[pltpu.VMEM((n_perm, tm, LANES), F32)],
        compiler_params=_params(("parallel",)), name="inproj")(x2d, pos, inv, bfp, *ws)


BIAS_PARTS = 3


def _split3(v):
    p0 = v.astype(BF16)
    r1 = v - p0.astype(F32)
    p1 = r1.astype(BF16)
    p2 = (r1 - p1.astype(F32)).astype(BF16)
    return p0, p1, p2


def _fox_keys_kernel(k_ref, lf_ref, tri_ref, sel_ref, o_ref, carry_ref):
    @pl.when(pl.program_id(1) == 0)
    def _reset():
        carry_ref[...] = jnp.zeros_like(carry_ref)

    tk = lf_ref.shape[0]
    tri = tri_ref[...]
    c = carry_ref[...] + sum(_dot(tri, part) for part in _split3(lf_ref[...]))
    carry_ref[...] = c[tk - 1:tk, :]
    lane = lax.broadcasted_iota(jnp.int32, (tk, LANES), 1)
    packed = sum(pltpu.roll(jnp.where(lane < B_HEADS, part.astype(F32), 0.0), B_HEADS * p, 1)
                 for p, part in enumerate(_split3(-LOG2E * c)))
    extra = _dot(packed.astype(BF16), sel_ref[...])
    low = lane < HEAD_DIM
    for h in range(B_HEADS):
        pair = k_ref[:, (h // 2) * LANES:(h // 2 + 1) * LANES].astype(F32)
        own = low if h % 2 == 0 else jnp.logical_not(low)
        sl = slice(h * LANES, (h + 1) * LANES)
        o_ref[:, sl] = jnp.where(own, pair, extra[:, sl]).astype(BF16)


def _fox_keys(k3d, lf3d, tk):
    b, s, w = k3d.shape
    tri = jnp.asarray(np.tril(np.ones((tk, tk), np.float32)), BF16)
    sel = np.zeros((LANES, B_HEADS * LANES), np.float32)
    for part in range(BIAS_PARTS):
        for head in range(B_HEADS):
            sel[part * B_HEADS + head, head * LANES + (HEAD_DIM if head % 2 == 0 else 0) + part] = 1.0
    sel = jnp.asarray(sel, BF16)
    blk = lambda n: pl.BlockSpec((None, tk, n), lambda bi, j: (bi, j, 0))
    return pl.pallas_call(
        _fox_keys_kernel, out_shape=jax.ShapeDtypeStruct((b, s, B_HEADS * LANES), BF16),
        grid=(b, s // tk), in_specs=[blk(w), blk(LANES), _const_spec(tri.shape), _const_spec(sel.shape)],
        out_specs=blk(B_HEADS * LANES), scratch_shapes=[pltpu.VMEM((1, LANES), F32)],
        compiler_params=_params(("parallel", "arbitrary")), name="fox_keys")(k3d, lf3d, tri, sel)


def _flash_kernel(forget, bq, *refs):
    bk = FLASH_BK
    qw = bk // 2
    if forget:
        q_ref, k_ref, vt_ref, o_ref, m_ref, acc_ref, st_ref = refs
    else:
        q_ref, k_ref, vt_ref, lam_ref, gain_ref, o_ref, m_ref, acc_ref, st_ref = refs
    dvp = acc_ref.shape[3]
    dv = dvp - ONES_ROWS
    nq = q_ref.shape[0] // bq
    parts = bq // qw
    lane = lax.broadcasted_iota(jnp.int32, (bq, LANES), 1)
    low = lane < HEAD_DIM
    if forget:
        fill = (jnp.where(lane < HEAD_DIM + BIAS_PARTS, 1.0, 0.0), jnp.where(lane < BIAS_PARTS, 1.0, 0.0))
    else:
        fill = (0.0, 0.0)
    triangle = (lax.broadcasted_iota(jnp.int32, (qw, qw), 0) <= lax.broadcasted_iota(jnp.int32, (qw, qw), 1))

    m_ref[...] = jnp.full(m_ref.shape, NEG, F32)
    acc_ref[...] = jnp.zeros(acc_ref.shape, F32)
    qts = {}

    def queries(i):
        if i not in qts:
            q = q_ref[i * bq:(i + 1) * bq, :].astype(F32)
            qts[i] = (jnp.where(low, q, fill[0]).T.astype(BF16), jnp.where(low, fill[1], q).T.astype(BF16))
        return qts[i]

    def live_rows(i, c, t):
        qb = i * parts + t
        blocks = max(0, min(bk // qw, qb - c * (bk // qw) + 1))
        return blocks * qw, blocks > 0 and c * (bk // qw) + blocks - 1 == qb

    def scores(i, c, buf):
        for s in range(2):
            for t in range(parts):
                rows, _ = live_rows(i, c, t)
                if rows:
                    kc = k_ref[c * bk:c * bk + rows, s * LANES:(s + 1) * LANES] if forget \
                        else k_ref[c * bk:c * bk + rows, :]
                    st_ref[buf, s, t, :rows, :] = _dot(kc, queries(i)[s][:, t * qw:(t + 1) * qw])

    def absorb(i, c, buf):
        for s in range(2):
            for t in range(parts):
                rows, diagonal = live_rows(i, c, t)
                if not rows:
                    continue
                st = st_ref[buf, s, t, :rows, :]
                if diagonal:
                    tri = jnp.where(triangle, st[rows - qw:], NEG)
                    st = tri if rows == qw else jnp.concatenate([st[:rows - qw], tri], axis=0)
                m_prev = m_ref[i, s, t]
                m_new = jnp.maximum(m_prev, jnp.max(st, axis=0, keepdims=True))
                a = jnp.exp2(m_prev - m_new)
                p = jnp.exp2(st - m_new).astype(BF16)
                m_ref[i, s, t] = m_new
                vt = vt_ref[c, s * dvp:(s + 1) * dvp, :rows] if forget else vt_ref[c, :, :rows]
                acc_ref[i, s, t] = a * acc_ref[i, s, t] + _dot(vt, p)

    def finish(i):
        for t in range(parts):
            o0 = acc_ref[i, 0, t, :dv, :] / acc_ref[i, 0, t, dv:dv + 1, :]
            o1 = acc_ref[i, 1, t, :dv, :] / acc_ref[i, 1, t, dv:dv + 1, :]
            rows = slice(i * bq + t * qw, i * bq + (t + 1) * qw)
            if forget:
                o_ref[rows, :] = jnp.concatenate([o0, o1], axis=0).T.astype(o_ref.dtype)
            else:
                o = (o0 - lam_ref[:, 0:1] * o1).T
                ms = jnp.mean(o * o, axis=-1, keepdims=True)
                o_ref[rows, :] = (o * lax.rsqrt(ms + EPS) * gain_ref[...]).astype(o_ref.dtype)

    units = [(i, c) for i in range(nq) for c in range((i + 1) * bq // bk)]
    nbuf = st_ref.shape[0]
    for n in range(min(nbuf - 1, len(units))):
        scores(*units[n], n % nbuf)
    for n, (i, c) in enumerate(units):
        if n + nbuf - 1 < len(units):
            scores(*units[n + nbuf - 1], (n + nbuf - 1) % nbuf)
        absorb(i, c, n % nbuf)
        if c == (i + 1) * bq // bk - 1:
            finish(i)


def _flash(q, k, vt, extra, forget, bq):
    b, s, w = q.shape
    g = w // LANES
    nk = s // FLASH_BK
    dvp = (HEAD_DIM if forget else LANES) + ONES_ROWS
    rows = vt.shape[1] // g
    kw = k.shape[2] // g
    qspec = pl.BlockSpec((None, s, LANES), lambda bi, gi: (bi, 0, gi))
    kspec = pl.BlockSpec((None, s, kw), lambda bi, gi: (bi, 0, gi))
    vspec = pl.BlockSpec((nk, rows, FLASH_BK), lambda bi, gi: (bi, gi, 0))
    assert bq == 2 * FLASH_BK
    bk = FLASH_BK
    nq = s // bq
    qw, parts = bk // 2, 2 * bq // bk
    scratch = [pltpu.VMEM((nq, 2, parts, 1, qw), F32), pltpu.VMEM((nq, 2, parts, dvp, qw), F32),
               pltpu.VMEM((3, 2, parts, bk, qw), F32)]
    especs = [] if forget else [_const_spec((1, LANES)), _const_spec((1, LANES))]
    return pl.pallas_call(
        functools.partial(_flash_kernel, forget, bq),
        out_shape=jax.ShapeDtypeStruct((b, s, w), BF16),
        grid=(b, g), in_specs=[qspec, kspec, vspec] + especs, out_specs=qspec,
        scratch_shapes=scratch,
        compiler_params=_params(("parallel", "parallel")),
        name="fox_attn" if forget else "diff_attn")(q, k, vt, *extra)


def _dilated_kernel(nq, q_ref, kp_ref, kc_ref, vp_ref, vc_ref, o_ref, lse_ref):
    n = C_BLOCK
    pairs = q_ref.shape[1] // LANES
    j = pl.program_id(2)
    kall = jnp.concatenate([kp_ref[...], kc_ref[...]], axis=0)
    vall = jnp.concatenate([vp_ref[...], vc_ref[...]], axis=0)
    qi = lax.broadcasted_iota(jnp.int32, (n, 2 * n), 0)
    kj = lax.broadcasted_iota(jnp.int32, (n, 2 * n), 1)
    dist = n + qi - kj
    window = (dist >= 0) & (dist <= n)
    first_key = jnp.where(j > 0, 0, n)
    lane = lax.broadcasted_iota(jnp.int32, (n, LANES), 1)
    low = lane < HEAD_DIM
    units = [(t, p, s) for t in range(nq) for p in range(pairs) for s in range(2)]
    scores = {}
    for t, p, s in units:
        sl = slice(p * LANES, (p + 1) * LANES)
        q128 = q_ref[t * n:(t + 1) * n, sl]
        zero = jnp.zeros_like(q128)
        qm = jnp.where(low, q128, zero) if s == 0 else jnp.where(low, zero, q128)
        mask = (window & (kj >= first_key)) if t == 0 else window
        scores[t, p, s] = jnp.where(mask, _dot_nt(qm, kall[t * n:(t + 2) * n, sl]), NEG)
    probs = {}
    for u in units:
        m = jnp.max(scores[u], axis=-1, keepdims=True)
        probs[u] = (jnp.exp2(scores[u] - m).astype(BF16), m)
    low_keys = lax.broadcasted_iota(jnp.int32, (2 * n, LANES), 1) < HEAD_DIM
    for t in range(nq):
        for p in range(pairs):
            sl = slice(p * LANES, (p + 1) * LANES)
            v128 = vall[t * n:(t + 2) * n, sl]
            one = jnp.ones_like(v128)
            os, ls = [], []
            for s in range(2):
                vs = jnp.where(low_keys, v128, one) if s == 0 else jnp.where(low_keys, one, v128)
                num = _dot(probs[t, p, s][0], vs)
                den = pltpu.roll(num, HEAD_DIM, 1)
                os.append(num / den)
                ls.append(probs[t, p, s][1] + jnp.log2(den))
            o_ref[t * n:(t + 1) * n, sl] = jnp.where(low, os[0], os[1])
            lse_ref[t * n:(t + 1) * n, sl] = jnp.where(low, ls[0], ls[1])


def _dilated(q, k, v, b, dilation, nq, nr):
    w = C_GW * nr
    n = C_BLOCK
    m_len = q.shape[0] // b
    nb = m_len // (n * nq)
    view = lambda t: t.reshape(b, m_len, dilation * C_GW)
    cur = pl.BlockSpec((None, n * nq, w), lambda bi, r, j: (bi, j, r))
    prev = pl.BlockSpec((None, n, w), lambda bi, r, j: (bi, jnp.maximum(j * nq - 1, 0), r))
    o, lse = pl.pallas_call(
        functools.partial(_dilated_kernel, nq),
        out_shape=[jax.ShapeDtypeStruct((b, m_len, dilation * C_GW), F32)] * 2,
        grid=(b, dilation // nr, nb), in_specs=[cur, prev, cur, prev, cur], out_specs=[cur, cur],
        compiler_params=_params(("parallel", "parallel", "arbitrary")),
        name=f"dilated_d{dilation}")(view(q), view(k), view(k), view(v), view(v))
    return o.reshape(b * m_len, dilation * C_GW), lse.reshape(b * m_len, dilation * C_GW)


def _merge_kernel(x_ref, oa_ref, ob_ref, oc0, oc1, oc2, ls0, ls1, ls2,
                  wg_ref, bg_ref, wa_ref, wb_ref, wc_ref, wo_ref, g_ref, b_ref, o_ref, perm_ref):
    x = x_ref[...]
    xb = x.astype(BF16)
    tm = x.shape[0]

    def token_major(blk_ref, gi, slot):
        d = C_PATTERNS[gi][1]
        if d == 1:
            return blk_ref[...]
        for h in range(C_GW // LANES):
            for r in range(d):
                lo = r * C_GW + h * LANES
                perm_ref.at[slot + h][pl.ds(r, tm // d, stride=d), :] = blk_ref[:, lo:lo + LANES]
        return jnp.concatenate([perm_ref[slot + h] for h in range(C_GW // LANES)], axis=1)

    os = [token_major(r, gi, 2 * gi) for gi, r in enumerate((oc0, oc1, oc2))]
    ls = [token_major(r, gi, 2 * C_GROUPS + 2 * gi) for gi, r in enumerate((ls0, ls1, ls2))]
    for r0 in range(0, tm, SUB_ROWS):
        rows = slice(r0, r0 + SUB_ROWS)
        l0, l1, l2 = (l[rows] for l in ls)
        m = jnp.maximum(jnp.maximum(l0, l1), l2)
        e0, e1, e2 = jnp.exp2(l0 - m), jnp.exp2(l1 - m), jnp.exp2(l2 - m)
        oc = (e0 * os[0][rows] + e1 * os[1][rows] + e2 * os[2][rows]) / (e0 + e1 + e2)
        branches = ((oa_ref[rows, :], wa_ref), (ob_ref[rows, :], wb_ref), (oc.astype(BF16), wc_ref))
        merged = None
        for k, (ob, w_ref) in enumerate(branches):
            sl = slice(k * D_MODEL, (k + 1) * D_MODEL)
            z = _dot(xb[rows], wg_ref[:, sl]) + bg_ref[:, sl]
            gate = 1.0 / (1.0 + jnp.exp(-z))
            term = gate * _dot(ob, w_ref[...])
            merged = term if merged is None else merged + term
        h = _dot(merged.astype(BF16), wo_ref[...])
        o_ref[rows, :] = _layer_norm(DN_ALPHA * x[rows] + h, g_ref[...], b_ref[...])


def _merge(x2d, oa, ob, ocs, lses, wg, bg, wa, wb, wc, wo, g, bta, tm):
    t = x2d.shape[0]
    row = lambda n: pl.BlockSpec((tm, n), lambda i: (i, 0))
    consts = (wg, bg, wa, wb, wc, wo, g, bta)
    cspecs = [pl.BlockSpec((tm // d, d * C_GW), lambda i: (i, 0)) for _, d in C_PATTERNS]
    in_specs = [row(D_MODEL), row(A_W), row(B_W)] + cspecs * 2 + [_const_spec(c.shape) for c in consts]
    return pl.pallas_call(
        _merge_kernel, out_shape=jax.ShapeDtypeStruct((t, D_MODEL), F32), grid=(t // tm,),
        in_specs=in_specs, out_specs=row(D_MODEL),
        scratch_shapes=[pltpu.VMEM((4 * C_GROUPS, tm, LANES), F32)],
        compiler_params=_params(("parallel",)), name="merge")(x2d, oa, ob, *ocs, *lses, *consts)


def _memkv_kernel(m_ref, wk_ref, wv_ref, k_ref, v_ref):
    mb = m_ref[...].astype(BF16)
    k_ref[...] = _dot(mb, wk_ref[...]).astype(BF16)
    v_ref[...] = _dot(mb, wv_ref[...]).astype(BF16)


def _memkv(mem2d, wk, wv, tm):
    t = mem2d.shape[0]
    row = pl.BlockSpec((tm, D_MODEL), lambda i: (i, 0))
    return pl.pallas_call(
        _memkv_kernel, out_shape=[jax.ShapeDtypeStruct((t, D_MODEL), BF16)] * 2, grid=(t // tm,),
        in_specs=[row, _const_spec(wk.shape), _const_spec(wv.shape)], out_specs=[row, row],
        compiler_params=_params(("parallel",)), name="memkv")(mem2d, wk, wv)


def _xattn_kernel(x_ref, k_ref, v_ref, wq_ref, wo_ref, g_ref, b_ref, o_ref):
    for r0 in range(0, x_ref.shape[0], SUB_ROWS):
        rows = slice(r0, r0 + SUB_ROWS)
        x = x_ref[rows, :]
        q = _dot(x.astype(BF16), wq_ref[...]).astype(BF16)
        heads = []
        for h in range(X_HEADS):
            sl = slice(h * X_HEAD_DIM, (h + 1) * X_HEAD_DIM)
            st = _dot_nt(q[:, sl], k_ref[:, sl])
            m = jnp.max(st, axis=-1, keepdims=True)
            e = jnp.exp(st - m)
            l = jnp.sum(e, axis=-1, keepdims=True)
            heads.append((_dot(e.astype(BF16), v_ref[:, sl]) / l).astype(BF16))
        o = jnp.concatenate(heads, axis=-1)
        h_out = _dot(o, wo_ref[...])
        o_ref[rows, :] = _layer_norm(DN_ALPHA * x + h_out, g_ref[...], b_ref[...])


def _xattn(x3d, k3d, v3d, wq, wo, g, bta, tm):
    b, s, d = x3d.shape
    xspec = pl.BlockSpec((None, tm, d), lambda bi, i: (bi, i, 0))
    mspec = pl.BlockSpec((None, MEM_LEN, d), lambda bi, i: (bi, 0, 0))
    consts = (wq, wo, g, bta)
    return pl.pallas_call(
        _xattn_kernel, out_shape=jax.ShapeDtypeStruct((b, s, d), F32), grid=(b, s // tm),
        in_specs=[xspec, mspec, mspec] + [_const_spec(c.shape) for c in consts], out_specs=xspec,
        compiler_params=_params(("parallel", "parallel")), name="xattn")(x3d, k3d, v3d, *consts)


MOE_EPB = 2
ROUTE_LANE0 = N_GROUPS


def _moe_kernel(x_ref, wrh_ref, wrl_ref, br_ref, wg_ref, wu_ref, wd_ref, g_ref, b_ref, o_ref,
                xb_ref, gate_ref):
    half = x_ref.shape[0] // 2
    lane = lax.broadcasted_iota(jnp.int32, (half, LANES), 1)
    lanef = lane.astype(F32)
    for r0 in (0, half):
        rows = slice(r0, r0 + half)
        x = x_ref[rows, :]
        xh = x.astype(BF16)
        xl = (x - xh.astype(F32)).astype(BF16)
        xb_ref[rows, :] = xh
        hh_hl = _dot(xh, jnp.concatenate([wrh_ref[...], wrl_ref[...]], axis=1))
        logits = (hh_hl[:, :LANES] + _dot(xl, wrh_ref[...]) + hh_hl[:, LANES:]) + br_ref[...]
        gl = jnp.where(lane < N_GROUPS, logits, NEG)
        gmax = jnp.max(gl, axis=-1, keepdims=True)
        g_gate = 1.0 / jnp.sum(jnp.exp(gl - gmax), axis=-1, keepdims=True)
        g_idx = jnp.min(jnp.where(gl == gmax, lanef, float(LANES)), axis=-1, keepdims=True)
        e_group = ((lane - ROUTE_LANE0) >> 2).astype(F32)
        sel = (lane >= ROUTE_LANE0) & (e_group == g_idx)
        el = jnp.where(sel, logits, NEG)
        top1 = jnp.max(el, axis=-1, keepdims=True)
        i1 = jnp.min(jnp.where(el == top1, lanef, float(LANES)), axis=-1, keepdims=True)
        el2 = jnp.where(lanef == i1, NEG, el)
        top2 = jnp.max(el2, axis=-1, keepdims=True)
        i2 = jnp.min(jnp.where(el2 == top2, lanef, float(LANES)), axis=-1, keepdims=True)
        r = jnp.exp(top2 - top1)
        w1 = 1.0 / (1.0 + r)
        w2 = r * w1
        gate_ref[rows, :] = jnp.where(lanef == i1, w1, jnp.where(lanef == i2, w2, 0.0)) * g_gate
        o_ref[rows, :] = jnp.zeros((half, D_MODEL), F32)

    for r0 in (0, half):
        rows = slice(r0, r0 + half)
        for e in range(N_EXPERTS):
            xb = xb_ref[rows, :]
            gcol = jnp.sum(jnp.where(lane == ROUTE_LANE0 + e, gate_ref[rows, :], 0.0), axis=-1, keepdims=True)
            a = _dot(xb, wg_ref[e])
            u = _dot(xb, wu_ref[e])
            h = (a / (1.0 + jnp.exp(-a))) * u * gcol
            o_ref[rows, :] += _dot(h.astype(BF16), wd_ref[e])
        o_ref[rows, :] = _layer_norm(DN_ALPHA * x_ref[rows, :] + o_ref[rows, :], g_ref[...], b_ref[...])


def _moe(x2d, wrh, wrl, br, wg, wu, wd, g, bta, tm):
    t = x2d.shape[0]
    row = pl.BlockSpec((tm, D_MODEL), lambda i: (i, 0))
    consts = (wrh, wrl, br, wg, wu, wd, g, bta)
    return pl.pallas_call(
        _moe_kernel, out_shape=jax.ShapeDtypeStruct((t, D_MODEL), F32),
        grid=(t // tm,), in_specs=[row] + [_const_spec(c.shape) for c in consts], out_specs=row,
        scratch_shapes=[pltpu.VMEM((tm, D_MODEL), BF16), pltpu.VMEM((tm, LANES), F32)],
        compiler_params=_params(("parallel",)), name="moe")(x2d, *consts)


def _pad_cols(w, n):
    return jnp.pad(w, ((0, 0), (0, n - w.shape[1])))


def kernel(x, mem, positions, w_in, b_forget, diff_lambda, diff_subln, w_branch_a, w_branch_b, w_branch_c,
           w_gate, b_gate, w_out, w_xq, w_xk, w_xv, w_xo, w_route_group, b_route_group, w_route_expert,
           b_route_expert, w_expert_gate, w_expert_up, w_expert_down, ln_g, ln_b):
    b, s, d = x.shape
    t = b * s
    depth = w_in.shape[0]
    scale = HEAD_DIM ** -0.5

    half = HEAD_DIM // 2
    inv = ROPE_THETA ** (-jnp.arange(half, dtype=F32) / half)
    inv_row = jnp.tile(inv, LANES // half).reshape(1, LANES)
    pos = positions.astype(F32).reshape(t, 1)

    mem2d = mem.reshape(b * MEM_LEN, d)
    x2d = x.reshape(t, d)
    row = lambda v: v.reshape(1, -1).astype(F32)
    for l in range(depth):
        lam_init = 0.8 - 0.6 * math.exp(-0.3 * l)
        offs = [0, A_W, 2 * A_W, 3 * A_W, 3 * A_W + B_W, 3 * A_W + 2 * B_W, 3 * A_W + 3 * B_W]
        offs.append(offs[-1] + B_HEADS)
        offs += [offs[-1] + C_W, offs[-1] + 2 * C_W, offs[-1] + 3 * C_W]
        wl = w_in[l]
        seg = [wl[:, offs[k]:offs[k + 1]] for k in range(10)]
        qa, ka, va, qb, kb, vb, fb, qc, kc, vc = seg
        ws = [(qa * (scale * LOG2E)), ka, va.T, (qb * (scale * LOG2E)), kb, vb.T, _pad_cols(fb, LANES),
              (qc * (scale * LOG2E)), kc, vc]
        ws = [w.astype(BF16) for w in ws]
        bfp = _pad_cols(row(b_forget[l]), LANES)

        outs = _inproj(x2d, pos, inv_row, bfp, ws, tm=512)
        o_qa, o_ka, o_va, o_qb, o_kb, o_vb, o_lf = outs[:7]
        o_qc, o_kc, o_vc = outs[7:10], outs[10:13], outs[13:16]

        lf = diff_lambda[l].astype(F32)
        lam = jnp.exp(jnp.dot(lf[0], lf[1])) - jnp.exp(jnp.dot(lf[2], lf[3])) + lam_init
        lam_row = jnp.full((1, LANES), lam, F32)
        gain = row(diff_subln[l]) * (1.0 - lam_init)
        r3 = lambda v: v.reshape(b, s, v.shape[-1])
        oa = _flash(r3(o_qa), r3(o_ka), o_va, (lam_row, gain), forget=False, bq=2 * FLASH_BK)
        kb_aug = _fox_keys(r3(o_kb), r3(o_lf), tk=512)
        ob = _flash(r3(o_qb), kb_aug, o_vb, (), forget=True, bq=2 * FLASH_BK)
        ocs, lses = [], []
        for gi, (_, dil) in enumerate(C_PATTERNS):
            nq = min(8, s // (dil * C_BLOCK))
            o_g, lse_g = _dilated(o_qc[gi], o_kc[gi], o_vc[gi], b, dil, nq, nr=max(1, min(dil, 8 // nq)))
            ocs.append(o_g)
            lses.append(lse_g)
        x2d = _merge(x2d, oa.reshape(t, A_W), ob.reshape(t, B_W), ocs, lses,
                     w_gate[l].astype(BF16), row(b_gate[l]), w_branch_a[l].astype(BF16),
                     w_branch_b[l].astype(BF16), w_branch_c[l].astype(BF16), w_out[l].astype(BF16),
                     row(ln_g[l, 0]), row(ln_b[l, 0]), tm=512)
        xscale = X_HEAD_DIM ** -0.5
        mk, mv = _memkv(mem2d, w_xk[l].astype(BF16), w_xv[l].astype(BF16), tm=MEM_LEN)
        x2d = _xattn(x2d.reshape(b, s, d), mk.reshape(b, MEM_LEN, d), mv.reshape(b, MEM_LEN, d),
                     (w_xq[l] * xscale).astype(BF16), w_xo[l].astype(BF16),
                     row(ln_g[l, 1]), row(ln_b[l, 1]), tm=512).reshape(t, d)
        wr = jnp.concatenate([w_route_group[l]] + [w_route_expert[l, gi] for gi in range(N_GROUPS)], axis=1)
        wr = _pad_cols(wr.astype(F32), LANES)
        wrh = wr.astype(BF16)
        wrl = (wr - wrh.astype(F32)).astype(BF16)
        br = _pad_cols(row(jnp.concatenate([b_route_group[l], b_route_expert[l].reshape(-1)])), LANES)
        ne = (N_EXPERTS,)
        x2d = _moe(x2d, wrh, wrl, br,
                   w_expert_gate[l].reshape(ne + w_expert_gate.shape[3:]).astype(BF16),
                   w_expert_up[l].reshape(ne + w_expert_up.shape[3:]).astype(BF16),
                   w_expert_down[l].reshape(ne + w_expert_down.shape[3:]).astype(BF16),
                   row(ln_g[l, 2]), row(ln_b[l, 2]), tm=1024)
    return x2d.reshape(b, s, d)
```

```python
import functools
import math

import jax
import jax.numpy as jnp
import numpy as np
from jax import lax
from jax.experimental import pallas as pl
from jax.experimental.pallas import tpu as pltpu

F32 = jnp.float32
BF16 = jnp.bfloat16

D_MODEL = 1024
HEAD_DIM = 64
ROPE_THETA = 10000.0
EPS = 1e-5
MEM_LEN = 256
A_HEADS = 4
B_HEADS = 8
C_PATTERNS = ((128, 1), (512, 4), (2048, 16))
C_GROUPS = 3
C_SLOTS = 4
C_BLOCK = 128
X_HEADS = 4
X_HEAD_DIM = D_MODEL // X_HEADS
N_GROUPS = 4
EXPERTS_PER_GROUP = 4
N_EXPERTS = N_GROUPS * EXPERTS_PER_GROUP
D_EXPERT = D_MODEL // 4
DEPTH = 2
DN_ALPHA = (2 * DEPTH) ** 0.25

LANES = 128
NEG = -1e30
LOG2E = math.log2(math.e)
VMEM_LIMIT = 56 * 1024 * 1024

A_W = A_HEADS * 2 * HEAD_DIM
B_W = B_HEADS * HEAD_DIM
C_GW = C_SLOTS * HEAD_DIM
C_W = C_GROUPS * C_GW
ONES_ROWS = 16
VA_ROWS = A_HEADS * (2 * HEAD_DIM + ONES_ROWS)
VB_ROWS = B_HEADS * (HEAD_DIM + ONES_ROWS)
SUB_ROWS = 256
FLASH_BK = 512


def _params(sem, flags=None):
    return pltpu.CompilerParams(dimension_semantics=sem, vmem_limit_bytes=VMEM_LIMIT, flags=flags)


def _const_spec(shape):
    nd = len(shape)
    return pl.BlockSpec(shape, lambda *_: (0,) * nd, pipeline_mode=pl.Buffered(1))


def _layer_norm(y, g, b):
    mu = jnp.mean(y, axis=-1, keepdims=True)
    yc = y - mu
    var = jnp.mean(yc * yc, axis=-1, keepdims=True)
    return yc * lax.rsqrt(var + EPS) * g + b


def _dot(a, b):
    return jnp.dot(a, b, preferred_element_type=F32)


def _dot_nt(a, b):
    return lax.dot_general(a, b, (((1,), (1,)), ((), ())), preferred_element_type=F32)


def _inproj_kernel(x_ref, pos_ref, inv_ref, bf_ref,
                   w_qa, w_ka, w_va, w_qb, w_kb, w_vb, w_fb, w_qc, w_kc, w_vc,
                   o_qa, o_ka, o_va, o_qb, o_kb, o_vb, o_lf,
                   o_qc0, o_qc1, o_qc2, o_kc0, o_kc1, o_kc2, o_vc0, o_vc1, o_vc2, perm_ref):
    xb = x_ref[...].astype(BF16)
    ang = pos_ref[...] * inv_ref[...]
    lane = lax.broadcasted_iota(jnp.int32, ang.shape, 1)
    lower = (lane & (HEAD_DIM - 1)) < HEAD_DIM // 2
    cos = jnp.cos(ang)
    sin = jnp.sin(ang)
    sin = jnp.where(lower, -sin, sin)

    def rope(t):
        sw = jnp.where(lower, pltpu.roll(t, LANES - HEAD_DIM // 2, 1), pltpu.roll(t, HEAD_DIM // 2, 1))
        return t * cos + sw * sin

    def roped(w_ref, outs, width):
        y = _dot(xb, w_ref[...])
        per_out = width // LANES
        for c in range(y.shape[1] // LANES):
            o = outs[c // per_out]
            lo = (c % per_out) * LANES
            o[:, lo:lo + LANES] = rope(y[:, c * LANES:(c + 1) * LANES]).astype(BF16)

    def plain(w_ref, outs, width):
        y = _dot(xb, w_ref[...])
        for c, o in enumerate(outs):
            o[...] = y[:, c * width:(c + 1) * width].astype(BF16)

    def transposed(wt_ref, o, width):
        yt = _dot_nt(wt_ref[...], xb)
        ones = jnp.ones((ONES_ROWS, FLASH_BK), F32)
        for c in range(o.shape[0]):
            cols = slice(c * FLASH_BK, (c + 1) * FLASH_BK)
            parts = []
            for r0 in range(0, yt.shape[0], width):
                parts += [yt[r0:r0 + width, cols], ones]
            o[c] = jnp.concatenate(parts, axis=0).astype(BF16)

    roped(w_qa, (o_qa,), A_W)
    roped(w_ka, (o_ka,), A_W)
    transposed(w_va, o_va, LANES)
    plain(w_qb, (o_qb,), B_W)
    plain(w_kb, (o_kb,), B_W)
    transposed(w_vb, o_vb, HEAD_DIM)
    def dilated(w_ref, outs, slot, use_rope):
        y = _dot(xb, w_ref[...])
        tm = y.shape[0]
        for gi, (_, d) in enumerate(C_PATTERNS):
            for h in range(C_GW // LANES):
                c = gi * (C_GW // LANES) + h
                t = y[:, c * LANES:(c + 1) * LANES]
                t = rope(t) if use_rope else t
                if d == 1:
                    outs[gi][:, h * LANES:(h + 1) * LANES] = t.astype(BF16)
                    continue
                sl = slot + (gi - 1) * 2 + h
                perm_ref[sl] = t
                for r in range(d):
                    lo = r * C_GW + h * LANES
                    outs[gi][:, lo:lo + LANES] = perm_ref.at[sl][pl.ds(r, tm // d, stride=d), :].astype(BF16)

    dilated(w_qc, (o_qc0, o_qc1, o_qc2), 0, True)
    dilated(w_kc, (o_kc0, o_kc1, o_kc2), 4, True)
    dilated(w_vc, (o_vc0, o_vc1, o_vc2), 8, False)
    z =_dot(xb, w_fb[...]) + bf_ref[...]
    o_lf[...] = jnp.minimum(z, 0.0) - jnp.log(1.0 + jnp.exp(-jnp.abs(z)))


def _inproj(x2d, pos, inv, bfp, ws, tm):
    t = x2d.shape[0]
    row = lambda n: pl.BlockSpec((tm, n), lambda i: (i, 0))
    tshape = lambda rows: jax.ShapeDtypeStruct((t // FLASH_BK, rows, FLASH_BK), BF16)
    tspec = lambda rows: pl.BlockSpec((tm // FLASH_BK, rows, FLASH_BK), lambda i: (i, 0, 0))
    rshape = lambda w: jax.ShapeDtypeStruct((t, w), BF16)
    out_shape = [rshape(A_W), rshape(A_W), tshape(VA_ROWS), rshape(B_W), rshape(B_W), tshape(VB_ROWS)]
    out_shape.append(jax.ShapeDtypeStruct((t, LANES), F32))
    out_specs = [row(A_W), row(A_W), tspec(VA_ROWS), row(B_W), row(B_W), tspec(VB_ROWS), row(LANES)]
    for _ in range(3):
        for _, d in C_PATTERNS:
            out_shape.append(jax.ShapeDtypeStruct((t // d, d * C_GW), BF16))
            out_specs.append(pl.BlockSpec((tm // d, d * C_GW), lambda i: (i, 0)))
    in_specs = [row(D_MODEL), row(1), _const_spec((1, LANES)), _const_spec((1, LANES))]
    in_specs += [_const_spec(w.shape) for w in ws]
    n_perm = 3 * (C_GROUPS - 1) * (C_GW // LANES)
    return pl.pallas_call(
        _inproj_kernel, out_shape=out_shape, grid=(t // tm,), in_specs=in_specs, out_specs=out_specs,
        scratch_shapes=[pltpu.VMEM((n_perm, tm, LANES), F32)],
        compiler_params=_params(("parallel",)), name="inproj")(x2d, pos, inv, bfp, *ws)


BIAS_PARTS = 3


def _split3(v):
    p0 = v.astype(BF16)
    r1 = v - p0.astype(F32)
    p1 = r1.astype(BF16)
    p2 = (r1 - p1.astype(F32)).astype(BF16)
    return p0, p1, p2


def _fox_keys_kernel(k_ref, lf_ref, tri_ref, sel_ref, o_ref, carry_ref):
    @pl.when(pl.program_id(1) == 0)
    def _reset():
        carry_ref[...] = jnp.zeros_like(carry_ref)

    tk = lf_ref.shape[0]
    tri = tri_ref[...]
    c = carry_ref[...] + sum(_dot(tri, part) for part in _split3(lf_ref[...]))
    carry_ref[...] = c[tk - 1:tk, :]
    lane = lax.broadcasted_iota(jnp.int32, (tk, LANES), 1)
    packed = sum(pltpu.roll(jnp.where(lane < B_HEADS, part.astype(F32), 0.0), B_HEADS * p, 1)
                 for p, part in enumerate(_split3(-LOG2E * c)))
    extra = _dot(packed.astype(BF16), sel_ref[...])
    low = lane < HEAD_DIM
    for h in range(B_HEADS):
        pair = k_ref[:, (h // 2) * LANES:(h // 2 + 1) * LANES].astype(F32)
        own = low if h % 2 == 0 else jnp.logical_not(low)
        sl = slice(h * LANES, (h + 1) * LANES)
        o_ref[:, sl] = jnp.where(own, pair, extra[:, sl]).astype(BF16)


def _fox_keys(k3d, lf3d, tk):
    b, s, w = k3d.shape
    tri = jnp.asarray(np.tril(np.ones((tk, tk), np.float32)), BF16)
    sel = np.zeros((LANES, B_HEADS * LANES), np.float32)
    for part in range(BIAS_PARTS):
        for head in range(B_HEADS):
            sel[part * B_HEADS + head, head * LANES + (HEAD_DIM if head % 2 == 0 else 0) + part] = 1.0
    sel = jnp.asarray(sel, BF16)
    blk = lambda n: pl.BlockSpec((None, tk, n), lambda bi, j: (bi, j, 0))
    return pl.pallas_call(
        _fox_keys_kernel, out_shape=jax.ShapeDtypeStruct((b, s, B_HEADS * LANES), BF16),
        grid=(b, s // tk), in_specs=[blk(w), blk(LANES), _const_spec(tri.shape), _const_spec(sel.shape)],
        out_specs=blk(B_HEADS * LANES), scratch_shapes=[pltpu.VMEM((1, LANES), F32)],
        compiler_params=_params(("parallel", "arbitrary")), name="fox_keys")(k3d, lf3d, tri, sel)


def _flash_kernel(forget, bq, *refs):
    bk = FLASH_BK
    qw = bk // 2
    if forget:
        q_ref, k_ref, vt_ref, o_ref, m_ref, acc_ref, st_ref = refs
    else:
        q_ref, k_ref, vt_ref, lam_ref, gain_ref, o_ref, m_ref, acc_ref, st_ref = refs
    dvp = acc_ref.shape[3]
    dv = dvp - ONES_ROWS
    nq = q_ref.shape[0] // bq
    parts = bq // qw
    lane = lax.broadcasted_iota(jnp.int32, (bq, LANES), 1)
    low = lane < HEAD_DIM
    if forget:
        fill = (jnp.where(lane < HEAD_DIM + BIAS_PARTS, 1.0, 0.0), jnp.where(lane < BIAS_PARTS, 1.0, 0.0))
    else:
        fill = (0.0, 0.0)
    triangle = (lax.broadcasted_iota(jnp.int32, (qw, qw), 0) <= lax.broadcasted_iota(jnp.int32, (qw, qw), 1))

    m_ref[...] = jnp.full(m_ref.shape, NEG, F32)
    acc_ref[...] = jnp.zeros(acc_ref.shape, F32)
    qts = {}

    def queries(i):
        if i not in qts:
            q = q_ref[i * bq:(i + 1) * bq, :].astype(F32)
            qts[i] = (jnp.where(low, q, fill[0]).T.astype(BF16), jnp.where(low, fill[1], q).T.astype(BF16))
        return qts[i]

    def live_rows(i, c, t):
        qb = i * parts + t
        blocks = max(0, min(bk // qw, qb - c * (bk // qw) + 1))
        return blocks * qw, blocks > 0 and c * (bk // qw) + blocks - 1 == qb

    def scores(i, c, buf):
        for s in range(2):
            for t in range(parts):
                rows, _ = live_rows(i, c, t)
                if rows:
                    kc = k_ref[c * bk:c * bk + rows, s * LANES:(s + 1) * LANES] if forget \
                        else k_ref[c * bk:c * bk + rows, :]
                    st_ref[buf, s, t, :rows, :] = _dot(kc, queries(i)[s][:, t * qw:(t + 1) * qw])

    def absorb(i, c, buf):
        for s in range(2):
            for t in range(parts):
                rows, diagonal = live_rows(i, c, t)
                if not rows:
                    continue
                st = st_ref[buf, s, t, :rows, :]
                if diagonal:
                    tri = jnp.where(triangle, st[rows - qw:], NEG)
                    st = tri if rows == qw else jnp.concatenate([st[:rows - qw], tri], axis=0)
                m_prev = m_ref[i, s, t]
                m_new = jnp.maximum(m_prev, jnp.max(st, axis=0, keepdims=True))
                a = jnp.exp2(m_prev - m_new)
                p = jnp.exp2(st - m_new).astype(BF16)
                m_ref[i, s, t] = m_new
                vt = vt_ref[c, s * dvp:(s + 1) * dvp, :rows] if forget else vt_ref[c, :, :rows]
                acc_ref[i, s, t] = a * acc_ref[i, s, t] + _dot(vt, p)

    def finish(i):
        for t in range(parts):
            o0 = acc_ref[i, 0, t, :dv, :] / acc_ref[i, 0, t, dv:dv + 1, :]
            o1 = acc_ref[i, 1, t, :dv, :] / acc_ref[i, 1, t, dv:dv + 1, :]
            rows = slice(i * bq + t * qw, i * bq + (t + 1) * qw)
            if forget:
                o_ref[rows, :] = jnp.concatenate([o0, o1], axis=0).T.astype(o_ref.dtype)
            else:
                o = (o0 - lam_ref[:, 0:1] * o1).T
                ms = jnp.mean(o * o, axis=-1, keepdims=True)
                o_ref[rows, :] = (o * lax.rsqrt(ms + EPS) * gain_ref[...]).astype(o_ref.dtype)

    units = [(i, c) for i in range(nq) for c in range((i + 1) * bq // bk)]
    nbuf = st_ref.shape[0]
    for n in range(min(nbuf - 1, len(units))):
        scores(*units[n], n % nbuf)
    for n, (i, c) in enumerate(units):
        if n + nbuf - 1 < len(units):
            scores(*units[n + nbuf - 1], (n + nbuf - 1) % nbuf)
        absorb(i, c, n % nbuf)
        if c == (i + 1) * bq // bk - 1:
            finish(i)


def _flash(q, k, vt, extra, forget, bq):
    b, s, w = q.shape
    g = w // LANES
    nk = s // FLASH_BK
    dvp = (HEAD_DIM if forget else LANES) + ONES_ROWS
    rows = vt.shape[1] // g
    kw = k.shape[2] // g
    qspec = pl.BlockSpec((None, s, LANES), lambda bi, gi: (bi, 0, gi))
    kspec = pl.BlockSpec((None, s, kw), lambda bi, gi: (bi, 0, gi))
    vspec = pl.BlockSpec((nk, rows, FLASH_BK), lambda bi, gi: (bi, gi, 0))
    assert bq == 2 * FLASH_BK
    bk = FLASH_BK
    nq = s // bq
    qw, parts = bk // 2, 2 * bq // bk
    scratch = [pltpu.VMEM((nq, 2, parts, 1, qw), F32), pltpu.VMEM((nq, 2, parts, dvp, qw), F32),
               pltpu.VMEM((3, 2, parts, bk, qw), F32)]
    especs = [] if forget else [_const_spec((1, LANES)), _const_spec((1, LANES))]
    return pl.pallas_call(
        functools.partial(_flash_kernel, forget, bq),
        out_shape=jax.ShapeDtypeStruct((b, s, w), BF16),
        grid=(b, g), in_specs=[qspec, kspec, vspec] + especs, out_specs=qspec,
        scratch_shapes=scratch,
        compiler_params=_params(("parallel", "parallel")),
        name="fox_attn" if forget else "diff_attn")(q, k, vt, *extra)


def _dilated_kernel(nq, q_ref, kp_ref, kc_ref, vp_ref, vc_ref, o_ref, lse_ref):
    n = C_BLOCK
    pairs = q_ref.shape[1] // LANES
    j = pl.program_id(2)
    kall = jnp.concatenate([kp_ref[...], kc_ref[...]], axis=0)
    vall = jnp.concatenate([vp_ref[...], vc_ref[...]], axis=0)
    qi = lax.broadcasted_iota(jnp.int32, (n, 2 * n), 0)
    kj = lax.broadcasted_iota(jnp.int32, (n, 2 * n), 1)
    dist = n + qi - kj
    window = (dist >= 0) & (dist <= n)
    first_key = jnp.where(j > 0, 0, n)
    lane = lax.broadcasted_iota(jnp.int32, (n, LANES), 1)
    low = lane < HEAD_DIM
    units = [(t, p, s) for t in range(nq) for p in range(pairs) for s in range(2)]
    scores = {}
    for t, p, s in units:
        sl = slice(p * LANES, (p + 1) * LANES)
        q128 = q_ref[t * n:(t + 1) * n, sl]
        zero = jnp.zeros_like(q128)
        qm = jnp.where(low, q128, zero) if s == 0 else jnp.where(low, zero, q128)
        mask = (window & (kj >= first_key)) if t == 0 else window
        scores[t, p, s] = jnp.where(mask, _dot_nt(qm, kall[t * n:(t + 2) * n, sl]), NEG)
    probs = {}
    for u in units:
        m = jnp.max(scores[u], axis=-1, keepdims=True)
        probs[u] = (jnp.exp2(scores[u] - m).astype(BF16), m)
    low_keys = lax.broadcasted_iota(jnp.int32, (2 * n, LANES), 1) < HEAD_DIM
    for t in range(nq):
        for p in range(pairs):
            sl = slice(p * LANES, (p + 1) * LANES)
            v128 = vall[t * n:(t + 2) * n, sl]
            one = jnp.ones_like(v128)
            os, ls = [], []
            for s in range(2):
                vs = jnp.where(low_keys, v128, one) if s == 0 else jnp.where(low_keys, one, v128)
                num = _dot(probs[t, p, s][0], vs)
                den = pltpu.roll(num, HEAD_DIM, 1)
                os.append(num / den)
                ls.append(probs[t, p, s][1] + jnp.log2(den))
            o_ref[t * n:(t + 1) * n, sl] = jnp.where(low, os[0], os[1])
            lse_ref[t * n:(t + 1) * n, sl] = jnp.where(low, ls[0], ls[1])


def _dilated(q, k, v, b, dilation, nq, nr):
    w = C_GW * nr
    n = C_BLOCK
    m_len = q.shape[0] // b
    nb = m_len // (n * nq)
    view = lambda t: t.reshape(b, m_len, dilation * C_GW)
    cur = pl.BlockSpec((None, n * nq, w), lambda bi, r, j: (bi, j, r))
    prev = pl.BlockSpec((None, n, w), lambda bi, r, j: (bi, jnp.maximum(j * nq - 1, 0), r))
    o, lse = pl.pallas_call(
        functools.partial(_dilated_kernel, nq),
        out_shape=[jax.ShapeDtypeStruct((b, m_len, dilation * C_GW), F32)] * 2,
        grid=(b, dilation // nr, nb), in_specs=[cur, prev, cur, prev, cur], out_specs=[cur, cur],
        compiler_params=_params(("parallel", "parallel", "arbitrary")),
        name=f"dilated_d{dilation}")(view(q), view(k), view(k), view(v), view(v))
    return o.reshape(b * m_len, dilation * C_GW), lse.reshape(b * m_len, dilation * C_GW)


def _merge_kernel(x_ref, oa_ref, ob_ref, oc0, oc1, oc2, ls0, ls1, ls2,
                  wg_ref, bg_ref, wa_ref, wb_ref, wc_ref, wo_ref, g_ref, b_ref, o_ref, perm_ref):
    x = x_ref[...]
    xb = x.astype(BF16)
    tm = x.shape[0]

    def token_major(blk_ref, gi, slot):
        d = C_PATTERNS[gi][1]
        if d == 1:
            return blk_ref[...]
        for h in range(C_GW // LANES):
            for r in range(d):
                lo = r * C_GW + h * LANES
                perm_ref.at[slot + h][pl.ds(r, tm // d, stride=d), :] = blk_ref[:, lo:lo + LANES]
        return jnp.concatenate([perm_ref[slot + h] for h in range(C_GW // LANES)], axis=1)

    os = [token_major(r, gi, 2 * gi) for gi, r in enumerate((oc0, oc1, oc2))]
    ls = [token_major(r, gi, 2 * C_GROUPS + 2 * gi) for gi, r in enumerate((ls0, ls1, ls2))]
    for r0 in range(0, tm, SUB_ROWS):
        rows = slice(r0, r0 + SUB_ROWS)
        l0, l1, l2 = (l[rows] for l in ls)
        m = jnp.maximum(jnp.maximum(l0, l1), l2)
        e0, e1, e2 = jnp.exp2(l0 - m), jnp.exp2(l1 - m), jnp.exp2(l2 - m)
        oc = (e0 * os[0][rows] + e1 * os[1][rows] + e2 * os[2][rows]) / (e0 + e1 + e2)
        branches = ((oa_ref[rows, :], wa_ref), (ob_ref[rows, :], wb_ref), (oc.astype(BF16), wc_ref))
        merged = None
        for k, (ob, w_ref) in enumerate(branches):
            sl = slice(k * D_MODEL, (k + 1) * D_MODEL)
            z = _dot(xb[rows], wg_ref[:, sl]) + bg_ref[:, sl]
            gate = 1.0 / (1.0 + jnp.exp(-z))
            term = gate * _dot(ob, w_ref[...])
            merged = term if merged is None else merged + term
        h = _dot(merged.astype(BF16), wo_ref[...])
        o_ref[rows, :] = _layer_norm(DN_ALPHA * x[rows] + h, g_ref[...], b_ref[...])


def _merge(x2d, oa, ob, ocs, lses, wg, bg, wa, wb, wc, wo, g, bta, tm):
    t = x2d.shape[0]
    row = lambda n: pl.BlockSpec((tm, n), lambda i: (i, 0))
    consts = (wg, bg, wa, wb, wc, wo, g, bta)
    cspecs = [pl.BlockSpec((tm // d, d * C_GW), lambda i: (i, 0)) for _, d in C_PATTERNS]
    in_specs = [row(D_MODEL), row(A_W), row(B_W)] + cspecs * 2 + [_const_spec(c.shape) for c in consts]
    return pl.pallas_call(
        _merge_kernel, out_shape=jax.ShapeDtypeStruct((t, D_MODEL), F32), grid=(t // tm,),
        in_specs=in_specs, out_specs=row(D_MODEL),
        scratch_shapes=[pltpu.VMEM((4 * C_GROUPS, tm, LANES), F32)],
        compiler_params=_params(("parallel",)), name="merge")(x2d, oa, ob, *ocs, *lses, *consts)


def _memkv_kernel(m_ref, wk_ref, wv_ref, k_ref, v_ref):
    mb = m_ref[...].astype(BF16)
    k_ref[...] = _dot(mb, wk_ref[...]).astype(BF16)
    v_ref[...] = _dot(mb, wv_ref[...]).astype(BF16)


def _memkv(mem2d, wk, wv, tm):
    t = mem2d.shape[0]
    row = pl.BlockSpec((tm, D_MODEL), lambda i: (i, 0))
    return pl.pallas_call(
        _memkv_kernel, out_shape=[jax.ShapeDtypeStruct((t, D_MODEL), BF16)] * 2, grid=(t // tm,),
        in_specs=[row, _const_spec(wk.shape), _const_spec(wv.shape)], out_specs=[row, row],
        compiler_params=_params(("parallel",)), name="memkv")(mem2d, wk, wv)


def _xattn_kernel(x_ref, k_ref, v_ref, wq_ref, wo_ref, g_ref, b_ref, o_ref):
    for r0 in range(0, x_ref.shape[0], SUB_ROWS):
        rows = slice(r0, r0 + SUB_ROWS)
        x = x_ref[rows, :]
        q = _dot(x.astype(BF16), wq_ref[...]).astype(BF16)
        heads = []
        for h in range(X_HEADS):
            sl = slice(h * X_HEAD_DIM, (h + 1) * X_HEAD_DIM)
            st = _dot_nt(q[:, sl], k_ref[:, sl])
            m = jnp.max(st, axis=-1, keepdims=True)
            e = jnp.exp(st - m)
            l = jnp.sum(e, axis=-1, keepdims=True)
            heads.append((_dot(e.astype(BF16), v_ref[:, sl]) / l).astype(BF16))
        o = jnp.concatenate(heads, axis=-1)
        h_out = _dot(o, wo_ref[...])
        o_ref[rows, :] = _layer_norm(DN_ALPHA * x + h_out, g_ref[...], b_ref[...])


def _xattn(x3d, k3d, v3d, wq, wo, g, bta, tm):
    b, s, d = x3d.shape
    xspec = pl.BlockSpec((None, tm, d), lambda bi, i: (bi, i, 0))
    mspec = pl.BlockSpec((None, MEM_LEN, d), lambda bi, i: (bi, 0, 0))
    consts = (wq, wo, g, bta)
    return pl.pallas_call(
        _xattn_kernel, out_shape=jax.ShapeDtypeStruct((b, s, d), F32), grid=(b, s // tm),
        in_specs=[xspec, mspec, mspec] + [_const_spec(c.shape) for c in consts], out_specs=xspec,
        compiler_params=_params(("parallel", "parallel")), name="xattn")(x3d, k3d, v3d, *consts)


MOE_EPB = 2
ROUTE_LANE0 = N_GROUPS


def _moe_kernel(x_ref, wrh_ref, wrl_ref, br_ref, wg_ref, wu_ref, wd_ref, g_ref, b_ref, o_ref,
                xb_ref, gate_ref):
    half = x_ref.shape[0] // 2
    lane = lax.broadcasted_iota(jnp.int32, (half, LANES), 1)
    lanef = lane.astype(F32)
    for r0 in (0, half):
        rows = slice(r0, r0 + half)
        x = x_ref[rows, :]
        xh = x.astype(BF16)
        xl = (x - xh.astype(F32)).astype(BF16)
        xb_ref[rows, :] = xh
        hh_hl = _dot(xh, jnp.concatenate([wrh_ref[...], wrl_ref[...]], axis=1))
        logits = (hh_hl[:, :LANES] + _dot(xl, wrh_ref[...]) + hh_hl[:, LANES:]) + br_ref[...]
        gl = jnp.where(lane < N_GROUPS, logits, NEG)
        gmax = jnp.max(gl, axis=-1, keepdims=True)
        g_gate = 1.0 / jnp.sum(jnp.exp(gl - gmax), axis=-1, keepdims=True)
        g_idx = jnp.min(jnp.where(gl == gmax, lanef, float(LANES)), axis=-1, keepdims=True)
        e_group = ((lane - ROUTE_LANE0) >> 2).astype(F32)
        sel = (lane >= ROUTE_LANE0) & (e_group == g_idx)
        el = jnp.where(sel, logits, NEG)
        top1 = jnp.max(el, axis=-1, keepdims=True)
        i1 = jnp.min(jnp.where(el == top1, lanef, float(LANES)), axis=-1, keepdims=True)
        el2 = jnp.where(lanef == i1, NEG, el)
        top2 = jnp.max(el2, axis=-1, keepdims=True)
        i2 = jnp.min(jnp.where(el2 == top2, lanef, float(LANES)), axis=-1, keepdims=True)
        r = jnp.exp(top2 - top1)
        w1 = 1.0 / (1.0 + r)
        w2 = r * w1
        gate_ref[rows, :] = jnp.where(lanef == i1, w1, jnp.where(lanef == i2, w2, 0.0)) * g_gate
        o_ref[rows, :] = jnp.zeros((half, D_MODEL), F32)

    for r0 in (0, half):
        rows = slice(r0, r0 + half)
        for e in range(N_EXPERTS):
            xb = xb_ref[rows, :]
            gcol = jnp.sum(jnp.where(lane == ROUTE_LANE0 + e, gate_ref[rows, :], 0.0), axis=-1, keepdims=True)
            a = _dot(xb, wg_ref[e])
            u = _dot(xb, wu_ref[e])
            h = (a / (1.0 + jnp.exp(-a))) * u * gcol
            o_ref[rows, :] += _dot(h.astype(BF16), wd_ref[e])
        o_ref[rows, :] = _layer_norm(DN_ALPHA * x_ref[rows, :] + o_ref[rows, :], g_ref[...], b_ref[...])


def _moe(x2d, wrh, wrl, br, wg, wu, wd, g, bta, tm):
    t = x2d.shape[0]
    row = pl.BlockSpec((tm, D_MODEL), lambda i: (i, 0))
    consts = (wrh, wrl, br, wg, wu, wd, g, bta)
    return pl.pallas_call(
        _moe_kernel, out_shape=jax.ShapeDtypeStruct((t, D_MODEL), F32),
        grid=(t // tm,), in_specs=[row] + [_const_spec(c.shape) for c in consts], out_specs=row,
        scratch_shapes=[pltpu.VMEM((tm, D_MODEL), BF16), pltpu.VMEM((tm, LANES), F32)],
        compiler_params=_params(("parallel",)), name="moe")(x2d, *consts)


def _pad_cols(w, n):
    return jnp.pad(w, ((0, 0), (0, n - w.shape[1])))


def kernel(x, mem, positions, w_in, b_forget, diff_lambda, diff_subln, w_branch_a, w_branch_b, w_branch_c,
           w_gate, b_gate, w_out, w_xq, w_xk, w_xv, w_xo, w_route_group, b_route_group, w_route_expert,
           b_route_expert, w_expert_gate, w_expert_up, w_expert_down, ln_g, ln_b):
    b, s, d = x.shape
    t = b * s
    depth = w_in.shape[0]
    scale = HEAD_DIM ** -0.5

    half = HEAD_DIM // 2
    inv = ROPE_THETA ** (-jnp.arange(half, dtype=F32) / half)
    inv_row = jnp.tile(inv, LANES // half).reshape(1, LANES)
    pos = positions.astype(F32).reshape(t, 1)

    mem2d = mem.reshape(b * MEM_LEN, d)
    x2d = x.reshape(t, d)
    row = lambda v: v.reshape(1, -1).astype(F32)
    for l in range(depth):
        lam_init = 0.8 - 0.6 * math.exp(-0.3 * l)
        offs = [0, A_W, 2 * A_W, 3 * A_W, 3 * A_W + B_W, 3 * A_W + 2 * B_W, 3 * A_W + 3 * B_W]
        offs.append(offs[-1] + B_HEADS)
        offs += [offs[-1] + C_W, offs[-1] + 2 * C_W, offs[-1] + 3 * C_W]
        wl = w_in[l]
        seg = [wl[:, offs[k]:offs[k + 1]] for k in range(10)]
        qa, ka, va, qb, kb, vb, fb, qc, kc, vc = seg
        ws = [(qa * (scale * LOG2E)), ka, va.T, (qb * (scale * LOG2E)), kb, vb.T, _pad_cols(fb, LANES),
              (qc * (scale * LOG2E)), kc, vc]
        ws = [w.astype(BF16) for w in ws]
        bfp = _pad_cols(row(b_forget[l]), LANES)

        outs = _inproj(x2d, pos, inv_row, bfp, ws, tm=512)
        o_qa, o_ka, o_va, o_qb, o_kb, o_vb, o_lf = outs[:7]
        o_qc, o_kc, o_vc = outs[7:10], outs[10:13], outs[13:16]

        lf = diff_lambda[l].astype(F32)
        lam = jnp.exp(jnp.dot(lf[0], lf[1])) - jnp.exp(jnp.dot(lf[2], lf[3])) + lam_init
        lam_row = jnp.full((1, LANES), lam, F32)
        gain = row(diff_subln[l]) * (1.0 - lam_init)
        r3 = lambda v: v.reshape(b, s, v.shape[-1])
        oa = _flash(r3(o_qa), r3(o_ka), o_va, (lam_row, gain), forget=False, bq=2 * FLASH_BK)
        kb_aug = _fox_keys(r3(o_kb), r3(o_lf), tk=512)
        ob = _flash(r3(o_qb), kb_aug, o_vb, (), forget=True, bq=2 * FLASH_BK)
        ocs, lses = [], []
        for gi, (_, dil) in enumerate(C_PATTERNS):
            nq = min(16, s // (dil * C_BLOCK))
            o_g, lse_g = _dilated(o_qc[gi], o_kc[gi], o_vc[gi], b, dil, nq, nr=max(1, min(dil, 16 // nq)))
            ocs.append(o_g)
            lses.append(lse_g)
        x2d = _merge(x2d, oa.reshape(t, A_W), ob.reshape(t, B_W), ocs, lses,
                     w_gate[l].astype(BF16), row(b_gate[l]), w_branch_a[l].astype(BF16),
                     w_branch_b[l].astype(BF16), w_branch_c[l].astype(BF16), w_out[l].astype(BF16),
                     row(ln_g[l, 0]), row(ln_b[l, 0]), tm=512)
        xscale = X_HEAD_DIM ** -0.5
        mk, mv = _memkv(mem2d, w_xk[l].astype(BF16), w_xv[l].astype(BF16), tm=MEM_LEN)
        x2d = _xattn(x2d.reshape(b, s, d), mk.reshape(b, MEM_LEN, d), mv.reshape(b, MEM_LEN, d),
                     (w_xq[l] * xscale).astype(BF16), w_xo[l].astype(BF16),
                     row(ln_g[l, 1]), row(ln_b[l, 1]), tm=512).reshape(t, d)
        wr = jnp.concatenate([w_route_group[l]] + [w_route_expert[l, gi] for gi in range(N_GROUPS)], axis=1)
        wr = _pad_cols(wr.astype(F32), LANES)
        wrh = wr.astype(BF16)
        wrl = (wr - wrh.astype(F32)).astype(BF16)
        br = _pad_cols(row(jnp.concatenate([b_route_group[l], b_route_expert[l].reshape(-1)])), LANES)
        ne = (N_EXPERTS,)
        x2d = _moe(x2d, wrh, wrl, br,
                   w_expert_gate[l].reshape(ne + w_expert_gate.shape[3:]).astype(BF16),
                   w_expert_up[l].reshape(ne + w_expert_up.shape[3:]).astype(BF16),
                   w_expert_down[l].reshape(ne + w_expert_down.shape[3:]).astype(BF16),
                   row(ln_g[l, 2]), row(ln_b[l, 2]), tm=1024)
    return x2d.reshape(b, s, d)
```
